```python
import math
import jax
import jax.numpy as jnp
from jax import lax
import numpy as np

D_MODEL = 2048
BATCH = 4
SEQ = 2048
DEPTH = 4

GRID_W = 64
CTX_LEN = 256
HEAD_DIM = 128
Q_BLOCK = 128
ROPE_THETA = 10000.0
NORM_EPS = 1e-6
NEG_INF = -1e30

A_HEADS = 4
A_KV_HEADS = 2
A_GROUP = A_HEADS // A_KV_HEADS
B_HEADS = 4
B_KV_HEADS = 2
B_GROUP = B_HEADS // B_KV_HEADS
WINDOW = 128
C_HEADS = 4
C_QK_DIM = HEAD_DIM // 2
D_HEADS = 4
MLA_Q_RANK = 512
MLA_KV_RANK = 512
MLA_NOPE = 128
MLA_ROPE = 64
MLA_V = 128

N_BRANCH = 4
BRANCH_W = 4 * HEAD_DIM
_FFN_RAW = -(-8 * D_MODEL // 3)
FFN_HIDDEN = 256 * (-(-_FFN_RAW // 256))

ALPHA = (2 * DEPTH) ** 0.25
BETA = (8 * DEPTH) ** -0.25

IN_SIZES = (
    A_HEADS * HEAD_DIM, A_KV_HEADS * HEAD_DIM, A_KV_HEADS * HEAD_DIM,
    B_HEADS * HEAD_DIM, B_KV_HEADS * HEAD_DIM, B_KV_HEADS * HEAD_DIM,
    C_HEADS * 2 * C_QK_DIM, C_HEADS * 2 * C_QK_DIM, C_HEADS * HEAD_DIM,
    MLA_Q_RANK, MLA_KV_RANK, MLA_ROPE,
    N_BRANCH * D_MODEL,
)
D_IN = sum(IN_SIZES)

kernel_name = 'hybrid_parallel_mixer_dit_block'


def _standardise(x):
    xf = x.astype(jnp.float32)
    mu = jnp.mean(xf, axis=-1, keepdims=True)
    var = jnp.mean(jnp.square(xf - mu), axis=-1, keepdims=True)
    return (xf - mu) * lax.rsqrt(var + NORM_EPS)


def layer_norm(x, g, b):
    return (_standardise(x) * g.astype(jnp.float32) + b.astype(jnp.float32)).astype(x.dtype)


def rms_norm(x, g):
    xf = x.astype(jnp.float32)
    y = xf * lax.rsqrt(jnp.mean(xf * xf, axis=-1, keepdims=True) + NORM_EPS)
    return (y * g.astype(jnp.float32)).astype(x.dtype)


def axial_tables(pos_row, pos_col, rot_dim):
    axis_dim = rot_dim // 2
    freqs = ROPE_THETA ** (-jnp.arange(0, axis_dim, 2, dtype=jnp.float32) / axis_dim)
    ang_r = pos_row.astype(jnp.float32)[:, None] * freqs[None, :]
    ang_c = pos_col.astype(jnp.float32)[:, None] * freqs[None, :]
    return (jnp.cos(ang_r), jnp.sin(ang_r), jnp.cos(ang_c), jnp.sin(ang_c))


def _rot_half(x, cos, sin):
    h = x.shape[-1] // 2
    x1, x2 = x[..., :h], x[..., h:]
    return jnp.concatenate([x1 * cos - x2 * sin, x2 * cos + x1 * sin], axis=-1)


def axial_rope(x, tabs):
    cr, sr, cc, sc = tabs
    extra = (1,) * (x.ndim - 3)
    shp = lambda t: t.reshape((t.shape[0],) + extra + (t.shape[1],))
    h = x.shape[-1] // 2
    xf = x.astype(jnp.float32)
    out = jnp.concatenate([_rot_half(xf[..., :h], shp(cr), shp(sr)),
                           _rot_half(xf[..., h:], shp(cc), shp(sc))], axis=-1)
    return out.astype(x.dtype)


def softmax32(s):
    return jax.nn.softmax(s.astype(jnp.float32), axis=-1)


def block_sweep(fn, *qs):
    T = qs[0].shape[1]
    nb = T // Q_BLOCK
    blocks = tuple(jnp.moveaxis(q.reshape((q.shape[0], nb, Q_BLOCK) + q.shape[2:]), 1, 0) for q in qs)
    out = lax.map(lambda a: fn(*a), blocks)
    out = jnp.moveaxis(out, 0, 1)
    return out.reshape((out.shape[0], T) + out.shape[3:])


def gqa_core(q, k, v):
    s = jnp.einsum('bqhgd,bshd->bhgqs', q, k) * (q.shape[-1] ** -0.5)
    p = softmax32(s).astype(v.dtype)
    return jnp.einsum('bhgqs,bshd->bqhgd', p, v)


def sink_attention(q, k, v, sink):
    s = jnp.einsum('bqhgd,bshd->bhgqs', q, k) * (q.shape[-1] ** -0.5)
    sk = jnp.broadcast_to(sink.astype(jnp.float32).reshape(1, B_KV_HEADS, B_GROUP, 1, 1), s.shape[:-1] + (1,))
    p = softmax32(jnp.concatenate([s.astype(jnp.float32), sk], axis=-1))[..., :-1].astype(v.dtype)
    return jnp.einsum('bhgqs,bshd->bqhgd', p, v)


def band_mask(nb):
    r = jnp.arange(Q_BLOCK)[None, :, None]
    j = jnp.arange(3 * Q_BLOCK)[None, None, :]
    n = jnp.arange(nb)[:, None, None]
    qpos = n * Q_BLOCK + r
    kpos = n * Q_BLOCK - WINDOW + j
    return (jnp.abs(qpos - kpos) <= WINDOW) & (kpos >= 0) & (kpos < nb * Q_BLOCK)


def window_sink_attention(q, k_lat, v_lat, k_ctx, v_ctx, sink):
    B_, T_, KVH, G, d = q.shape
    nb = T_ // Q_BLOCK
    pad = ((0, 0), (WINDOW, WINDOW), (0, 0), (0, 0))
    kp = jnp.pad(k_lat, pad).reshape(B_, nb + 2, Q_BLOCK, KVH, d)
    vp = jnp.pad(v_lat, pad).reshape(B_, nb + 2, Q_BLOCK, KVH, d)
    kb = jnp.concatenate([kp[:, :-2], kp[:, 1:-1], kp[:, 2:]], axis=2)
    vb = jnp.concatenate([vp[:, :-2], vp[:, 1:-1], vp[:, 2:]], axis=2)
    qb = q.reshape(B_, nb, Q_BLOCK, KVH, G, d)
    scale = d ** -0.5
    s_band = jnp.einsum('bnqhgd,bnkhd->bhgnqk', qb, kb).astype(jnp.float32) * scale
    s_band = jnp.where(band_mask(nb), s_band, NEG_INF)
    s_ctx = jnp.einsum('bnqhgd,bchd->bhgnqc', qb, k_ctx).astype(jnp.float32) * scale
    sk = jnp.broadcast_to(sink.astype(jnp.float32).reshape(1, KVH, G, 1, 1, 1), s_band.shape[:-1] + (1,))
    p = softmax32(jnp.concatenate([s_band, s_ctx, sk], axis=-1))
    nk = 3 * Q_BLOCK
    p_band = p[..., :nk].astype(v_lat.dtype)
    p_ctx = p[..., nk:nk + k_ctx.shape[1]].astype(v_lat.dtype)
    o = jnp.einsum('bhgnqk,bnkhd->bnqhgd', p_band, vb) + jnp.einsum('bhgnqc,bchd->bnqhgd', p_ctx, v_ctx)
    return o.reshape(B_, T_, KVH, G, d)


def diff_core(q, k, v, lam):
    s = jnp.einsum('bqhcd,bshcd->bhcqs', q, k) * (q.shape[-1] ** -0.5)
    p = softmax32(s)
    w = (p[:, :, 0] - lam * p[:, :, 1]).astype(v.dtype)
    return jnp.einsum('bhqs,bshd->bqhd', w, v)


def diff_finish(o, g, lam_init):
    return rms_norm(o, g) * (1.0 - lam_init)


def mla_core(q_nope, q_rope, k_nope, k_rope, v):
    s = (jnp.einsum('bqhd,bshd->bhqs', q_nope, k_nope)
         + jnp.einsum('bqhd,bsd->bhqs', q_rope, k_rope)) * ((MLA_NOPE + MLA_ROPE) ** -0.5)
    p = softmax32(s).astype(v.dtype)
    return jnp.einsum('bhqs,bshd->bqhd', p, v)


def project_heads(h, w_in, a_qk_norm, mla_q_norm, mla_w_q_up, mla_kv_norm, mla_w_kv_up, tabs128, tabs64):
    B_, T_ = h.shape[0], h.shape[1]
    u = h @ w_in
    offs = np.cumsum(IN_SIZES)[:-1].tolist()
    aq, ak, av, bq, bk, bv, cq, ck, cv, dqa, dkva, dkr, gt = jnp.split(u, offs, axis=-1)
    rope128 = (lambda t: axial_rope(t, tabs128)) if tabs128 is not None else (lambda t: t)
    rope64 = (lambda t: axial_rope(t, tabs64)) if tabs64 is not None else (lambda t: t)
    dq = (rms_norm(dqa, mla_q_norm) @ mla_w_q_up).reshape(B_, T_, D_HEADS, MLA_NOPE + MLA_ROPE)
    dkv = (rms_norm(dkva, mla_kv_norm) @ mla_w_kv_up).reshape(B_, T_, D_HEADS, MLA_NOPE + MLA_V)
    return {
        'aq': rope128(rms_norm(aq.reshape(B_, T_, A_KV_HEADS, A_GROUP, HEAD_DIM), a_qk_norm[0])),
        'ak': rope128(rms_norm(ak.reshape(B_, T_, A_KV_HEADS, HEAD_DIM), a_qk_norm[1])),
        'av': av.reshape(B_, T_, A_KV_HEADS, HEAD_DIM),
        'bq': rope128(bq.reshape(B_, T_, B_KV_HEADS, B_GROUP, HEAD_DIM)),
        'bk': rope128(bk.reshape(B_, T_, B_KV_HEADS, HEAD_DIM)),
        'bv': bv.reshape(B_, T_, B_KV_HEADS, HEAD_DIM),
        'cq': rope64(cq.reshape(B_, T_, C_HEADS, 2, C_QK_DIM)),
        'ck': rope64(ck.reshape(B_, T_, C_HEADS, 2, C_QK_DIM)),
        'cv': cv.reshape(B_, T_, C_HEADS, HEAD_DIM),
        'dq_nope': dq[..., :MLA_NOPE],
        'dq_rope': rope64(dq[..., MLA_NOPE:]),
        'dk_nope': dkv[..., :MLA_NOPE],
        'dv': dkv[..., MLA_NOPE:],
        'dk_rope': rope64(dkr),
        'gates': jax.nn.sigmoid(gt).reshape(B_, T_, N_BRANCH, D_MODEL),
    }


def latent_mixers(L, C, sink, lam, c_subln, lam_init):
    B_, T_ = L['av'].shape[0], L['av'].shape[1]
    cat = lambda a, b: jnp.concatenate([a, b], axis=1)
    kA, vA = cat(C['ak'], L['ak']), cat(C['av'], L['av'])
    oA = block_sweep(lambda qb: gqa_core(qb, kA, vA), L['aq'])
    oB = window_sink_attention(L['bq'], L['bk'], L['bv'], C['bk'], C['bv'], sink)
    kC, vC = cat(C['ck'], L['ck']), cat(C['cv'], L['cv'])
    oC = diff_finish(block_sweep(lambda qb: diff_core(qb, kC, vC, lam), L['cq']), c_subln, lam_init)
    kDn, kDr, vD = cat(C['dk_nope'], L['dk_nope']), cat(C['dk_rope'], L['dk_rope']), cat(C['dv'], L['dv'])
    oD = block_sweep(lambda qn, qr: mla_core(qn, qr, kDn, kDr, vD), L['dq_nope'], L['dq_rope'])
    return [o.reshape(B_, T_, BRANCH_W) for o in (oA, oB, oC, oD)]


def context_mixers(C, sink, lam, c_subln, lam_init):
    B_, S_ = C['av'].shape[0], C['av'].shape[1]
    oA = gqa_core(C['aq'], C['ak'], C['av'])
    oB = sink_attention(C['bq'], C['bk'], C['bv'], sink)
    oC = diff_finish(diff_core(C['cq'], C['ck'], C['cv'], lam), c_subln, lam_init)
    oD = mla_core(C['dq_nope'], C['dq_rope'], C['dk_nope'], C['dk_rope'], C['dv'])
    return [o.reshape(B_, S_, BRANCH_W) for o in (oA, oB, oC, oD)]


def merge_branches(outs, gates, w_branch, w_out):
    y = gates[:, :, 0] * (outs[0] @ w_branch[0])
    for m in range(1, N_BRANCH):
        y = y + gates[:, :, m] * (outs[m] @ w_branch[m])
    return y @ w_out


def swiglu(h, w_in, w_out):
    g, u = jnp.split(h @ w_in, 2, axis=-1)
    return (jax.nn.silu(g) * u) @ w_out


def setup_inputs(seed: int = 0) -> dict:
    key = jax.random.key(seed)
    ks = jax.random.split(key, 24)
    nrm = lambda k, shape, s: jax.random.normal(k, shape, jnp.float32) * s
    L, D = DEPTH, D_MODEL
    return {
        'x': nrm(ks[0], (BATCH, SEQ, D), 1.0),
        'c': nrm(ks[1], (BATCH, D), 1.0),
        'ctx': nrm(ks[2], (BATCH, CTX_LEN, D), 1.0),
        'c_ctx': nrm(ks[3], (D,), 1.0),
        'w_ada': nrm(ks[4], (L, D, 6 * D), 0.5 * D ** -0.5),
        'b_ada': nrm(ks[5], (L, 6 * D), 0.02),
        'w_in': nrm(ks[6], (L, D, D_IN), D ** -0.5),
        'a_qk_norm': 1.0 + nrm(ks[7], (L, 2, HEAD_DIM), 0.02),
        'b_sink': nrm(ks[8], (L, B_HEADS), 0.5),
        'c_lambda': nrm(ks[9], (L, 4, C_QK_DIM), 0.1),
        'c_subln': 1.0 + nrm(ks[10], (L, HEAD_DIM), 0.02),
        'mla_q_norm': 1.0 + nrm(ks[11], (L, MLA_Q_RANK), 0.02),
        'mla_w_q_up': nrm(ks[12], (L, MLA_Q_RANK, D_HEADS * (MLA_NOPE + MLA_ROPE)), MLA_Q_RANK ** -0.5),
        'mla_kv_norm': 1.0 + nrm(ks[13], (L, MLA_KV_RANK), 0.02),
        'mla_w_kv_up': nrm(ks[14], (L, MLA_KV_RANK, D_HEADS * (MLA_NOPE + MLA_V)), MLA_KV_RANK ** -0.5),
        'w_branch': nrm(ks[15], (L, N_BRANCH, BRANCH_W, D), BRANCH_W ** -0.5),
        'w_out': nrm(ks[16], (L, D, D), BETA * D ** -0.5),
        'ln_mix_g': 1.0 + nrm(ks[17], (L, D), 0.02),
        'ln_mix_b': nrm(ks[18], (L, D), 0.02),
        'w_ffn_in': nrm(ks[19], (L, D, 2 * FFN_HIDDEN), D ** -0.5),
        'w_ffn_out': nrm(ks[20], (L, FFN_HIDDEN, D), BETA * FFN_HIDDEN ** -0.5),
        'ln_ffn_g': 1.0 + nrm(ks[21], (L, D), 0.02),
        'ln_ffn_b': nrm(ks[22], (L, D), 0.02),
    }


def reference(x, c, ctx, c_ctx, w_ada, b_ada, w_in, a_qk_norm, b_sink, c_lambda, c_subln,
              mla_q_norm, mla_w_q_up, mla_kv_norm, mla_w_kv_up, w_branch, w_out,
              ln_mix_g, ln_mix_b, w_ffn_in, w_ffn_out, ln_ffn_g, ln_ffn_b):
    T = x.shape[1]
    ROWS = T // GRID_W
    pos_row = jnp.repeat(jnp.arange(ROWS, dtype=jnp.int32), GRID_W)
    pos_col = jnp.tile(jnp.arange(GRID_W, dtype=jnp.int32), ROWS)
    tabs128 = axial_tables(pos_row, pos_col, HEAD_DIM)
    tabs64 = axial_tables(pos_row, pos_col, C_QK_DIM)

    x = _standardise(x).astype(x.dtype)
    xc = _standardise(ctx).astype(ctx.dtype)
    silu_c = jax.nn.silu(c)
    silu_cc = jax.nn.silu(c_ctx)

    for l in range(DEPTH):
        last = l == DEPTH - 1
        ada = silu_c @ w_ada[l] + b_ada[l]
        ada_c = silu_cc @ w_ada[l] + b_ada[l]
        sh_a, sc_a, g_a, sh_f, sc_f, g_f = jnp.split(ada[:, None, :], 6, axis=-1)
        csh_a, csc_a, cg_a, csh_f, csc_f, cg_f = jnp.split(ada_c, 6, axis=-1)
        lam_init = 0.8 - 0.6 * math.exp(-0.3 * l)
        lq1, lk1, lq2, lk2 = (c_lambda[l, i].astype(jnp.float32) for i in range(4))
        lam = jnp.exp(jnp.sum(lq1 * lk1)) - jnp.exp(jnp.sum(lq2 * lk2)) + lam_init

        proj = lambda h, t128, t64: project_heads(h, w_in[l], a_qk_norm[l], mla_q_norm[l], mla_w_q_up[l],
                                                  mla_kv_norm[l], mla_w_kv_up[l], t128, t64)
        Lh = proj(x * (1.0 + sc_a) + sh_a, tabs128, tabs64)
        Ch = proj(xc * (1.0 + csc_a) + csh_a, None, None)

        outs = latent_mixers(Lh, Ch, b_sink[l], lam, c_subln[l], lam_init)
        mix = merge_branches(outs, Lh['gates'], w_branch[l], w_out[l])
        x = layer_norm(ALPHA * x + g_a * mix, ln_mix_g[l], ln_mix_b[l])
        ffn = swiglu(x * (1.0 + sc_f) + sh_f, w_ffn_in[l], w_ffn_out[l])
        x = layer_norm(ALPHA * x + g_f * ffn, ln_ffn_g[l], ln_ffn_b[l])

        if not last:
            couts = context_mixers(Ch, b_sink[l], lam, c_subln[l], lam_init)
            cmix = merge_branches(couts, Ch['gates'], w_branch[l], w_out[l])
            xc = layer_norm(ALPHA * xc + cg_a * cmix, ln_mix_g[l], ln_mix_b[l])
            cffn = swiglu(xc * (1.0 + csc_f) + csh_f, w_ffn_in[l], w_ffn_out[l])
            xc = layer_norm(ALPHA * xc + cg_f * cffn, ln_ffn_g[l], ln_ffn_b[l])
    return x
```

```python
import functools
import math

import numpy as np
import jax
import jax.numpy as jnp
from jax import lax
from jax.experimental import pallas as pl
from jax.experimental.pallas import tpu as pltpu

F32 = jnp.float32
BF16 = jnp.bfloat16

D_MODEL = 2048
GRID_W = 64
HEAD_DIM = 128
WINDOW = 128
ROPE_THETA = 10000.0
NORM_EPS = 1e-6
NEG_INF = -1e30
N_HEADS = 4
KV_HEADS = 2
C_QK_DIM = 64
MLA_RANK = 512
MLA_NOPE = 128
MLA_ROPE = 64
MLA_QK_PAD = 256
BRANCH_W = 512
N_BRANCH = 4
FFN_HIDDEN = 5632
QKV_W = 4672
LANE = 128

UP_W = 5120
PROJ_TN = 512
COL_AQ, COL_AK, COL_AV = 0, 512, 768
COL_BQ, COL_BK, COL_BV = 1024, 1536, 1792
COL_CQ, COL_CK, COL_CV = 2048, 2560, 3072
COL_DQA, COL_DKVA, COL_DKR = 3584, 4096, 4608

TM = 1024
TE = 256
TQ = 256
MIB = 1024 * 1024


def _cp(sem, vmem_mib):
    return pltpu.CompilerParams(dimension_semantics=sem, vmem_limit_bytes=vmem_mib * MIB)


def _dot(a, b):
    return jnp.dot(a, b, preferred_element_type=F32)


def _dot_nt(a, b):
    return lax.dot_general(a, b, (((1,), (1,)), ((), ())), preferred_element_type=F32)


def _ada_kernel(c_ref, w_ref, b_ref, o_ref):
    c = c_ref[...]
    a = (c * (0.5 * jnp.tanh(0.5 * c) + 0.5)).astype(BF16)
    o_ref[0] = _dot(a, w_ref[0].astype(BF16)) + b_ref[0]


def _ada_all(c_all, w_ada, b_ada):
    depth, d, n6 = w_ada.shape
    tn = 1024
    return pl.pallas_call(
        _ada_kernel,
        grid=(depth, n6 // tn),
        in_specs=[pl.BlockSpec((8, d), lambda l, j: (0, 0)),
                  pl.BlockSpec((1, d, tn), lambda l, j: (l, 0, j)),
                  pl.BlockSpec((1, 1, tn), lambda l, j: (l, 0, j))],
        out_specs=pl.BlockSpec((1, 8, tn), lambda l, j: (l, 0, j)),
        out_shape=jax.ShapeDtypeStruct((depth, 8, n6), F32),
        compiler_params=_cp(("arbitrary", "arbitrary"), 40),
        name="ada",
    )(c_all, w_ada, b_ada.reshape(depth, 1, n6))


def _standardise(z):
    mu = jnp.mean(z, axis=-1, keepdims=True)
    zc = z - mu
    var = jnp.mean(zc * zc, axis=-1, keepdims=True)
    return zc * lax.rsqrt(var + NORM_EPS)


def _init_kernel(x_ref, ctx_ref, sc_ref, sh_ref, xo_ref, ho_ref, *, n_lat):
    i = pl.program_id(0)

    def body(src):
        xs = _standardise(src[...])
        xo_ref[...] = xs
        ho_ref[...] = (xs * sc_ref[0] + sh_ref[0]).astype(BF16)

    @pl.when(i < n_lat)
    def _():
        body(x_ref)

    @pl.when(i >= n_lat)
    def _():
        body(ctx_ref)


def _init(x2, ctx2, sc_tbl, sh_tbl):
    n_lat_rows, d = x2.shape
    n_ctx_rows = ctx2.shape[0]
    n_lat = n_lat_rows // TE
    n_all = (n_lat_rows + n_ctx_rows) // TE
    tbl = pl.BlockSpec((1, 1, d), lambda i: ((i * TE) // TM, 0, 0))
    row = pl.BlockSpec((TE, d), lambda i: (i, 0))
    return pl.pallas_call(
        functools.partial(_init_kernel, n_lat=n_lat),
        grid=(n_all,),
        in_specs=[pl.BlockSpec((TE, d), lambda i: (jnp.minimum(i, n_lat - 1), 0)),
                  pl.BlockSpec((TE, d), lambda i: (jnp.maximum(i - n_lat, 0), 0)),
                  tbl, tbl],
        out_specs=[row, row],
        out_shape=[jax.ShapeDtypeStruct((n_lat_rows + n_ctx_rows, d), F32),
                   jax.ShapeDtypeStruct((n_lat_rows + n_ctx_rows, d), BF16)],
        compiler_params=_cp(("arbitrary",), 32),
        name="init_norm",
    )(x2, ctx2, sc_tbl, sh_tbl)


def _ln_kernel(x_ref, d_ref, gate_ref, lng_ref, lnb_ref, *rest, alpha, with_h):
    z = alpha * x_ref[...] + gate_ref[0] * d_ref[...].astype(F32)
    xn = _standardise(z) * lng_ref[...] + lnb_ref[...]
    if with_h:
        sc_ref, sh_ref, xo_ref, ho_ref = rest
        xo_ref[...] = xn
        ho_ref[...] = (xn * sc_ref[0] + sh_ref[0]).astype(BF16)
    else:
        (xo_ref,) = rest
        xo_ref[...] = xn


def _post_ln(x, delta, gate_tbl, ln_g, ln_b, sc_tbl, sh_tbl, *, n_rows, alpha, out_rows):
    d = x.shape[1]
    with_h = sc_tbl is not None
    tbl = pl.BlockSpec((1, 1, d), lambda i: ((i * TE) // TM, 0, 0))
    row = pl.BlockSpec((TE, d), lambda i: (i, 0))
    vec = pl.BlockSpec((1, d), lambda i: (0, 0))
    in_specs = [row, row, tbl, vec, vec]
    args = [x, delta, gate_tbl, ln_g.reshape(1, d), ln_b.reshape(1, d)]
    out_specs = [row]
    out_shape = [jax.ShapeDtypeStruct((out_rows, d), F32)]
    if with_h:
        in_specs += [tbl, tbl]
        args += [sc_tbl, sh_tbl]
        out_specs.append(row)
        out_shape.append(jax.ShapeDtypeStruct((out_rows, d), BF16))
    res = pl.pallas_call(
        functools.partial(_ln_kernel, alpha=alpha, with_h=with_h),
        grid=(n_rows // TE,),
        in_specs=in_specs, out_specs=out_specs, out_shape=out_shape,
        compiler_params=_cp(("arbitrary",), 32),
        name="post_ln",
    )(*args)
    return res if with_h else (res[0], None)


def _swap_select(x, near, far, bit):
    lane = lax.broadcasted_iota(jnp.int32, x.shape, 1)
    return jnp.where((lane & bit) == 0, pltpu.roll(x, far, 1), pltpu.roll(x, near, 1))


def _rope128(x, cos, sin):
    return x * cos + _swap_select(x, 32, LANE - 32, 32) * sin


def _rope64(x, cos, sin):
    return x * cos + _swap_select(x, 16, LANE - 16, 16) * sin


def _rms(x, g):
    return x * lax.rsqrt(jnp.mean(x * x, axis=-1, keepdims=True) + NORM_EPS) * g


def _proj_kernel(h_ref, w_ref, tab_ref, aqk_ref, mq_ref, mkv_ref, o_ref, u_ref):
    j = pl.program_id(1)
    narrow = COL_DKR // PROJ_TN

    @pl.when(j != narrow)
    def _():
        u_ref[...] = _dot(h_ref[...], w_ref[...].astype(BF16))

    @pl.when(j == narrow)
    def _():
        u_ref[:, :LANE] = _dot(h_ref[...], w_ref[:, :LANE].astype(BF16))

    def head(k):
        return u_ref[:, k * LANE:(k + 1) * LANE]

    def put(k, val):
        o_ref[:, k * LANE:(k + 1) * LANE] = val.astype(BF16)

    def tabs128():
        return tab_ref[0], tab_ref[1]

    def tabs64():
        return tab_ref[2], tab_ref[3]

    @pl.when(j == COL_AQ // PROJ_TN)
    def _():
        cos, sin = tabs128()
        for k in range(4):
            put(k, _rope128(_rms(head(k), aqk_ref[0:1, :]), cos, sin) * (HEAD_DIM ** -0.5))

    @pl.when(j == COL_AK // PROJ_TN)
    def _():
        cos, sin = tabs128()
        for k in range(2):
            put(k, _rope128(_rms(head(k), aqk_ref[1:2, :]), cos, sin))
        for k in range(2, 4):
            put(k, head(k))

    @pl.when(j == COL_BQ // PROJ_TN)
    def _():
        cos, sin = tabs128()
        for k in range(4):
            put(k, _rope128(head(k), cos, sin) * (HEAD_DIM ** -0.5))

    @pl.when(j == COL_BK // PROJ_TN)
    def _():
        cos, sin = tabs128()
        for k in range(2):
            put(k, _rope128(head(k), cos, sin))
        for k in range(2, 4):
            put(k, head(k))

    @pl.when(j == COL_CQ // PROJ_TN)
    def _():
        cos, sin = tabs64()
        for k in range(4):
            put(k, _rope64(head(k), cos, sin) * (C_QK_DIM ** -0.5))

    @pl.when(j == COL_CK // PROJ_TN)
    def _():
        cos, sin = tabs64()
        for k in range(4):
            put(k, _rope64(head(k), cos, sin))

    @pl.when(j == COL_CV // PROJ_TN)
    def _():
        o_ref[...] = u_ref[...].astype(BF16)

    @pl.when(j == COL_DQA // PROJ_TN)
    def _():
        o_ref[...] = _rms(u_ref[...], mq_ref[...]).astype(BF16)

    @pl.when(j == COL_DKVA // PROJ_TN)
    def _():
        o_ref[...] = _rms(u_ref[...], mkv_ref[...]).astype(BF16)

    @pl.when(j == narrow)
    def _():
        cos, sin = tabs64()
        x = head(0)
        lane = lax.broadcasted_iota(jnp.int32, x.shape, 1)
        put(0, jnp.where(lane < MLA_ROPE, _rope64(x, cos, sin), 0.0))
        o_ref[:, LANE:] = jnp.zeros((o_ref.shape[0], PROJ_TN - LANE), BF16)


def _tab_index(i, n_lat_tiles, tiles_per_seq):
    return jnp.where(i < n_lat_tiles, i % tiles_per_seq, tiles_per_seq)


def _proj(h, w_in_l, tabs, aqk, mq, mkv, *, n_lat_tiles, tiles_per_seq):
    n, d = h.shape
    nt = n // TM
    return pl.pallas_call(
        _proj_kernel,
        grid=(nt, UP_W // PROJ_TN),
        in_specs=[pl.BlockSpec((TM, d), lambda i, j: (i, 0)),
                  pl.BlockSpec((d, PROJ_TN), lambda i, j: (0, j)),
                  pl.BlockSpec((4, TM, LANE), lambda i, j: (0, _tab_index(i, n_lat_tiles, tiles_per_seq), 0)),
                  pl.BlockSpec((2, LANE), lambda i, j: (0, 0)),
                  pl.BlockSpec((1, MLA_RANK), lambda i, j: (0, 0)),
                  pl.BlockSpec((1, MLA_RANK), lambda i, j: (0, 0))],
        out_specs=pl.BlockSpec((TM, PROJ_TN), lambda i, j: (i, j)),
        out_shape=jax.ShapeDtypeStruct((n, UP_W), BF16),
        scratch_shapes=[pltpu.VMEM((TM, PROJ_TN), F32)],
        compiler_params=_cp(("arbitrary", "arbitrary"), 48),
        name="proj",
    )(h, w_in_l, tabs, aqk, mq.reshape(1, MLA_RANK), mkv.reshape(1, MLA_RANK))


def _mla_up_kernel(qa_ref, kva_ref, kr_ref, wq_ref, wkv_ref, tab_ref, dq_ref, dk_ref, dv_ref):
    cos, sin = tab_ref[2], tab_ref[3]
    q = _dot(qa_ref[...], wq_ref[...].astype(BF16))
    kv = _dot(kva_ref[...], wkv_ref[...].astype(BF16))
    scale = (MLA_NOPE + MLA_ROPE) ** -0.5
    kr = kr_ref[...]
    for hh in range(N_HEADS):
        base = hh * MLA_QK_PAD
        dq_ref[:, base:base + LANE] = (q[:, base:base + LANE] * scale).astype(BF16)
        dq_ref[:, base + LANE:base + 2 * LANE] = (
            _rope64(q[:, base + LANE:base + 2 * LANE], cos, sin) * scale).astype(BF16)
        dk_ref[:, base:base + LANE] = kv[:, hh * LANE:(hh + 1) * LANE].astype(BF16)
        dk_ref[:, base + LANE:base + 2 * LANE] = kr
    dv_ref[...] = kv[:, N_HEADS * LANE:].astype(BF16)


def _mla_up(up, wq_pad, wkv_perm, tabs, *, n_lat_tiles, tiles_per_seq):
    n = up.shape[0]
    nt = n // TM
    wide = N_HEADS * MLA_QK_PAD
    return pl.pallas_call(
        _mla_up_kernel,
        grid=(nt,),
        in_specs=[pl.BlockSpec((TM, MLA_RANK), lambda i: (i, COL_DQA // MLA_RANK)),
                  pl.BlockSpec((TM, MLA_RANK), lambda i: (i, COL_DKVA // MLA_RANK)),
                  pl.BlockSpec((TM, LANE), lambda i: (i, COL_DKR // LANE)),
                  pl.BlockSpec((MLA_RANK, wide), lambda i: (0, 0)),
                  pl.BlockSpec((MLA_RANK, wide), lambda i: (0, 0)),
                  pl.BlockSpec((4, TM, LANE), lambda i: (0, _tab_index(i, n_lat_tiles, tiles_per_seq), 0))],
        out_specs=[pl.BlockSpec((TM, wide), lambda i: (i, 0)),
                   pl.BlockSpec((TM, wide), lambda i: (i, 0)),
                   pl.BlockSpec((TM, N_HEADS * LANE), lambda i: (i, 0))],
        out_shape=[jax.ShapeDtypeStruct((n, wide), BF16),
                   jax.ShapeDtypeStruct((n, wide), BF16),
                   jax.ShapeDtypeStruct((n, N_HEADS * LANE), BF16)],
        compiler_params=_cp(("arbitrary",), 48),
        name="mla_up",
    )(up, up, up, wq_pad, wkv_perm, tabs)


def _gate_kernel(h_ref, w_ref, o_ref):
    g = _dot(h_ref[...], w_ref[...])
    o_ref[...] = (0.5 * jnp.tanh(0.5 * g) + 0.5).astype(BF16)


def _gates(h, w_gate_bf16):
    n, d = h.shape
    wn = w_gate_bf16.shape[1]
    tn = 1024
    return pl.pallas_call(
        _gate_kernel,
        grid=(n // TM, wn // tn),
        in_specs=[pl.BlockSpec((TM, d), lambda i, j: (i, 0)),
                  pl.BlockSpec((d, tn), lambda i, j: (0, j))],
        out_specs=pl.BlockSpec((TM, tn), lambda i, j: (i, j)),
        out_shape=jax.ShapeDtypeStruct((n, wn), BF16),
        compiler_params=_cp(("arbitrary", "arbitrary"), 48),
        name="gates",
    )(h, w_gate_bf16)


def _attend(q, pieces, extra=None):
    scores = []
    for k, _, mask in pieces:
        s = _dot_nt(q, k)
        if mask is not None:
            s = jnp.where(mask, s, NEG_INF)
        scores.append(s)
    m = functools.reduce(jnp.maximum, [jnp.max(s, axis=-1, keepdims=True) for s in scores])
    if extra is not None:
        m = jnp.maximum(m, extra)
    den = None
    out = None
    for s, (_, v, _) in zip(scores, pieces):
        e = jnp.exp(s - m)
        part = jnp.sum(e, axis=-1, keepdims=True)
        pv = _dot(e.astype(BF16), v)
        den = part if den is None else den + part
        out = pv if out is None else out + pv
    if extra is not None:
        den = den + jnp.exp(extra - m)
    return out / den


def _heads(ref, k, width=LANE):
    return ref[:, k * width:(k + 1) * width]


def _attn_a_kernel(q_ref, kl_ref, vl_ref, kc_ref, vc_ref, o_ref, *, nq):
    t = pl.program_id(1)

    def run(with_latent):
        for kvh in range(KV_HEADS):
            q2 = jnp.concatenate([_heads(q_ref, 2 * kvh), _heads(q_ref, 2 * kvh + 1)], axis=0)
            pieces = [(_heads(kc_ref, kvh), _heads(vc_ref, kvh), None)]
            if with_latent:
                pieces.append((_heads(kl_ref, kvh), _heads(vl_ref, kvh), None))
            o = _attend(q2, pieces)
            o_ref[:, (2 * kvh) * LANE:(2 * kvh + 1) * LANE] = o[:TQ].astype(BF16)
            o_ref[:, (2 * kvh + 1) * LANE:(2 * kvh + 2) * LANE] = o[TQ:].astype(BF16)

    @pl.when(t < nq)
    def _():
        run(True)

    @pl.when(t >= nq)
    def _():
        run(False)


def _attn_b_kernel(q_ref, kl_ref, vl_ref, kc_ref, vc_ref, sink_ref, o_ref, *, nq, seq, win_k):
    t = pl.program_id(1)

    def run(with_latent):
        if with_latent:
            q0 = t * TQ
            start = pl.multiple_of(jnp.clip(q0 - WINDOW, 0, seq - win_k), LANE)
            r = lax.broadcasted_iota(jnp.int32, (2 * TQ, win_k), 0)
            c = lax.broadcasted_iota(jnp.int32, (2 * TQ, win_k), 1)
            qpos = q0 + jnp.where(r >= TQ, r - TQ, r)
            mask = jnp.abs(qpos - (start + c)) <= WINDOW
        for kvh in range(KV_HEADS):
            q2 = jnp.concatenate([_heads(q_ref, 2 * kvh), _heads(q_ref, 2 * kvh + 1)], axis=0)
            r1 = lax.broadcasted_iota(jnp.int32, (2 * TQ, 1), 0)
            sink = jnp.where(r1 >= TQ, sink_ref[0:1, 2 * kvh + 1:2 * kvh + 2],
                             sink_ref[0:1, 2 * kvh:2 * kvh + 1])
            pieces = [(_heads(kc_ref, kvh), _heads(vc_ref, kvh), None)]
            if with_latent:
                kw = kl_ref[pl.ds(start, win_k), kvh * LANE:(kvh + 1) * LANE]
                vw = vl_ref[pl.ds(start, win_k), kvh * LANE:(kvh + 1) * LANE]
                pieces.append((kw, vw, mask))
            o = _attend(q2, pieces, extra=sink)
            o_ref[:, (2 * kvh) * LANE:(2 * kvh + 1) * LANE] = o[:TQ].astype(BF16)
            o_ref[:, (2 * kvh + 1) * LANE:(2 * kvh + 2) * LANE] = o[TQ:].astype(BF16)

    @pl.when(t < nq)
    def _():
        run(True)

    @pl.when(t >= nq)
    def _():
        run(False)


def _attn_c_kernel(q_ref, kl_ref, vl_ref, kc_ref, vc_ref, lam_ref, g_ref, o_ref, *, nq, lam_init):
    t = pl.program_id(1)
    lam = (jnp.exp(jnp.sum(lam_ref[0:1, :] * lam_ref[1:2, :], axis=-1, keepdims=True))
           - jnp.exp(jnp.sum(lam_ref[2:3, :] * lam_ref[3:4, :], axis=-1, keepdims=True)) + lam_init)

    def run(with_latent):
        lane = lax.broadcasted_iota(jnp.int32, (TQ, LANE), 1)
        for hh in range(N_HEADS):
            q = _heads(q_ref, hh)
            zero = jnp.zeros_like(q)
            q2 = jnp.concatenate([jnp.where(lane < C_QK_DIM, q, zero),
                                  jnp.where(lane >= C_QK_DIM, q, zero)], axis=0)
            pieces = [(_heads(kc_ref, hh), _heads(vc_ref, hh), None)]
            if with_latent:
                pieces.append((_heads(kl_ref, hh), _heads(vl_ref, hh), None))
            o = _attend(q2, pieces)
            o = o[:TQ] - lam * o[TQ:]
            o = _rms(o, g_ref[...]) * (1.0 - lam_init)
            o_ref[:, hh * LANE:(hh + 1) * LANE] = o.astype(BF16)

    @pl.when(t < nq)
    def _():
        run(True)

    @pl.when(t >= nq)
    def _():
        run(False)


def _attn_d_kernel(q_ref, kl_ref, vl_ref, kc_ref, vc_ref, o_ref, *, nq):
    t = pl.program_id(1)

    def run(with_latent):
        for hh in range(N_HEADS):
            q = _heads(q_ref, hh, MLA_QK_PAD)
            pieces = [(_heads(kc_ref, hh, MLA_QK_PAD), _heads(vc_ref, hh), None)]
            if with_latent:
                pieces.append((_heads(kl_ref, hh, MLA_QK_PAD), _heads(vl_ref, hh), None))
            o_ref[:, hh * LANE:(hh + 1) * LANE] = _attend(q, pieces).astype(BF16)

    @pl.when(t < nq)
    def _():
        run(True)

    @pl.when(t >= nq)
    def _():
        run(False)


def _attention(kern, name, q_src, k_src, v_src, extras, *, batch, seq, ctx, with_ctx_queries,
               q_w, q_blk, k_w, k_blk, v_w, v_blk):
    assert ctx == TQ
    n = q_src.shape[0]
    nq = seq // TQ
    steps = nq + (1 if with_ctx_queries else 0)
    lat_blocks = batch * seq // TQ

    def qrow(b, t):
        return jnp.where(t < nq, b * nq + t, lat_blocks + b)

    in_specs = [pl.BlockSpec((TQ, q_w), lambda b, t: (qrow(b, t), q_blk)),
                pl.BlockSpec((seq, k_w), lambda b, t: (b, k_blk)),
                pl.BlockSpec((seq, v_w), lambda b, t: (b, v_blk)),
                pl.BlockSpec((ctx, k_w), lambda b, t: (batch * seq // ctx + b, k_blk)),
                pl.BlockSpec((ctx, v_w), lambda b, t: (batch * seq // ctx + b, v_blk))]
    args = [q_src, k_src, v_src, k_src, v_src]
    for e in extras:
        in_specs.append(pl.BlockSpec(e.shape, lambda b, t: (0, 0)))
        args.append(e)
    return pl.pallas_call(
        functools.partial(kern, nq=nq),
        grid=(batch, steps),
        in_specs=in_specs,
        out_specs=pl.BlockSpec((TQ, BRANCH_W), lambda b, t: (qrow(b, t), 0)),
        out_shape=jax.ShapeDtypeStruct((n, BRANCH_W), BF16),
        compiler_params=_cp(("arbitrary", "arbitrary"), 48),
        name=name,
    )(*args)


def _merge_kernel(oa_ref, ob_ref, oc_ref, od_ref, ga_ref, gb_ref, gc_ref, gd_ref, w_ref, y_ref):
    acc = None
    for m, (o_ref, g_ref) in enumerate(((oa_ref, ga_ref), (ob_ref, gb_ref), (oc_ref, gc_ref), (od_ref, gd_ref))):
        p = g_ref[...].astype(F32) * _dot(o_ref[...], w_ref[m].astype(BF16))
        acc = p if acc is None else acc + p
    y_ref[...] = acc.astype(BF16)


def _merge(outs, gates, w_branch_l, *, n_tiles):
    n = outs[0].shape[0]
    d = w_branch_l.shape[2]
    tn = 512
    nb = d // tn
    o_spec = pl.BlockSpec((TM, BRANCH_W), lambda i, j: (i, 0))
    g_specs = [pl.BlockSpec((TM, tn), functools.partial(lambda i, j, m: (i, m * nb + j), m=m))
               for m in range(N_BRANCH)]
    return pl.pallas_call(
        _merge_kernel,
        grid=(n_tiles, nb),
        in_specs=[o_spec] * N_BRANCH + g_specs + [pl.BlockSpec((N_BRANCH, BRANCH_W, tn), lambda i, j: (0, 0, j))],
        out_specs=pl.BlockSpec((TM, tn), lambda i, j: (i, j)),
        out_shape=jax.ShapeDtypeStruct((n, d), BF16),
        compiler_params=_cp(("arbitrary", "arbitrary"), 48),
        name="merge",
    )(*outs, gates, gates, gates, gates, w_branch_l)


def _matmul_kernel(a_ref, w_ref, o_ref):
    o_ref[...] = _dot(a_ref[...], w_ref[...].astype(BF16)).astype(o_ref.dtype)


def _matmul(a, w, *, n_tiles, tn=512, name="matmul"):
    n, k = a.shape
    d = w.shape[1]
    return pl.pallas_call(
        _matmul_kernel,
        grid=(n_tiles, d // tn),
        in_specs=[pl.BlockSpec((TM, k), lambda i, j: (i, 0)),
                  pl.BlockSpec((k, tn), lambda i, j: (0, j))],
        out_specs=pl.BlockSpec((TM, tn), lambda i, j: (i, j)),
        out_shape=jax.ShapeDtypeStruct((n, d), BF16),
        compiler_params=_cp(("arbitrary", "arbitrary"), 48),
        name=name,
    )(a, w)


def _ffn_kernel(h_ref, wg_ref, wu_ref, wo_ref, o_ref, acc_ref):
    f = pl.program_id(1)
    h = h_ref[...]
    g = _dot(h, wg_ref[...].astype(BF16))
    u = _dot(h, wu_ref[...].astype(BF16))
    a = (g * (0.5 * jnp.tanh(0.5 * g) + 0.5) * u).astype(BF16)
    part = _dot(a, wo_ref[...].astype(BF16))

    @pl.when(f == 0)
    def _():
        acc_ref[...] = part

    @pl.when(f > 0)
    def _():
        acc_ref[...] += part

    @pl.when(f == pl.num_programs(1) - 1)
    def _():
        o_ref[...] = acc_ref[...].astype(BF16)


def _ffn(h, w_ffn_in_l, w_ffn_out_l, *, n_tiles):
    n, d = h.shape
    hidden = w_ffn_out_l.shape[0]
    tf = 256
    nf = hidden // tf
    return pl.pallas_call(
        _ffn_kernel,
        grid=(n_tiles, nf),
        in_specs=[pl.BlockSpec((TM, d), lambda i, f: (i, 0)),
                  pl.BlockSpec((d, tf), lambda i, f: (0, f)),
                  pl.BlockSpec((d, tf), lambda i, f: (0, nf + f)),
                  pl.BlockSpec((tf, d), lambda i, f: (f, 0))],
        out_specs=pl.BlockSpec((TM, d), lambda i, f: (i, 0)),
        out_shape=jax.ShapeDtypeStruct((n, d), BF16),
        scratch_shapes=[pltpu.VMEM((TM, d), F32)],
        compiler_params=_cp(("arbitrary", "arbitrary"), 56),
        name="ffn",
    )(h, w_ffn_in_l, w_ffn_in_l, w_ffn_out_l)


def _rope_tables(seq, pad_rows):
    t = jnp.arange(seq, dtype=jnp.int32)
    pos_row = (t // GRID_W).astype(F32)[:, None]
    pos_col = (t % GRID_W).astype(F32)[:, None]

    def one(rot_dim, reps):
        axis_dim = rot_dim // 2
        freqs = ROPE_THETA ** (-jnp.arange(0, axis_dim, 2, dtype=F32) / axis_dim)
        ar, ac = pos_row * freqs[None, :], pos_col * freqs[None, :]
        cos = jnp.concatenate([jnp.cos(ar), jnp.cos(ar), jnp.cos(ac), jnp.cos(ac)], axis=-1)
        sin = jnp.concatenate([-jnp.sin(ar), jnp.sin(ar), -jnp.sin(ac), jnp.sin(ac)], axis=-1)
        return jnp.tile(cos, (1, reps)), jnp.tile(sin, (1, reps))

    c128, s128 = one(HEAD_DIM, 1)
    c64, s64 = one(C_QK_DIM, 2)
    ones = jnp.ones((pad_rows, LANE), F32)
    zeros = jnp.zeros((pad_rows, LANE), F32)
    return jnp.stack([jnp.concatenate([c128, ones]), jnp.concatenate([s128, zeros]),
                      jnp.concatenate([c64, ones]), jnp.concatenate([s64, zeros])])


def kernel(x, c, ctx, c_ctx, w_ada, b_ada, w_in, a_qk_norm, b_sink, c_lambda, c_subln, mla_q_norm, mla_w_q_up, mla_kv_norm, mla_w_kv_up, w_branch, w_out, ln_mix_g, ln_mix_b, w_ffn_in, w_ffn_out, ln_ffn_g, ln_ffn_b):
    batch, seq, d = x.shape
    ctx_len = ctx.shape[1]
    depth = w_ada.shape[0]
    assert d == D_MODEL and seq % TM == 0 and (batch * ctx_len) % TM == 0 and batch + 1 <= 8
    n_lat_rows = batch * seq
    n_rows = n_lat_rows + batch * ctx_len
    tiles_per_seq = seq // TM
    n_lat_tiles = n_lat_rows // TM
    n_tiles = n_rows // TM
    alpha = (2 * depth) ** 0.25

    c_all = jnp.concatenate([c, c_ctx[None, :], jnp.zeros((8 - batch - 1, d), F32)], axis=0)
    ada = _ada_all(c_all, w_ada, b_ada)
    tile_row = np.array([b for b in range(batch) for _ in range(tiles_per_seq)]
                        + [batch] * (n_tiles - n_lat_tiles), np.int32)

    def table(l, chunk, plus_one=False):
        v = ada[l, :, chunk * d:(chunk + 1) * d][tile_row]
        return ((1.0 + v) if plus_one else v)[:, None, :]

    tabs = _rope_tables(seq, TM)
    x_all, h = _init(x.reshape(n_lat_rows, d), ctx.reshape(batch * ctx_len, d),
                     table(0, 1, True), table(0, 0))

    win_k = min(TQ + 2 * WINDOW, seq)
    sink_pad = jnp.zeros((depth, 8, LANE), F32).at[:, 0, :N_HEADS].set(b_sink)
    wq_pad = jnp.pad(mla_w_q_up.reshape(depth, MLA_RANK, N_HEADS, MLA_NOPE + MLA_ROPE),
                     ((0, 0), (0, 0), (0, 0), (0, MLA_QK_PAD - MLA_NOPE - MLA_ROPE))
                     ).reshape(depth, MLA_RANK, N_HEADS * MLA_QK_PAD)
    wkv_pad = jnp.pad(mla_w_kv_up.reshape(depth, MLA_RANK, N_HEADS, 2, LANE).transpose(0, 1, 3, 2, 4)
                      .reshape(depth, MLA_RANK, 2 * N_HEADS * LANE),
                      ((0, 0), (0, 0), (0, N_HEADS * MLA_QK_PAD - 2 * N_HEADS * LANE)))

    for l in range(depth):
        last = l == depth - 1
        act_tiles = n_lat_tiles if last else n_tiles
        act_rows = act_tiles * TM
        lam_init = 0.8 - 0.6 * math.exp(-0.3 * l)

        up = _proj(h, w_in[l], tabs, a_qk_norm[l], mla_q_norm[l], mla_kv_norm[l],
                   n_lat_tiles=n_lat_tiles, tiles_per_seq=tiles_per_seq)
        dq, dk, dv = _mla_up(up, wq_pad[l], wkv_pad[l], tabs,
                             n_lat_tiles=n_lat_tiles, tiles_per_seq=tiles_per_seq)
        gates = _gates(h, w_in[l][:, QKV_W:].astype(BF16))

        common = dict(batch=batch, seq=seq, ctx=ctx_len, with_ctx_queries=not last)
        o_a = _attention(_attn_a_kernel, "attn_a", up, up, up, [], **common,
                         q_w=512, q_blk=COL_AQ // 512, k_w=256, k_blk=COL_AK // 256, v_w=256, v_blk=COL_AV // 256)
        o_b = _attention(functools.partial(_attn_b_kernel, seq=seq, win_k=win_k), "attn_b", up, up, up,
                         [sink_pad[l]], **common,
                         q_w=512, q_blk=COL_BQ // 512, k_w=256, k_blk=COL_BK // 256, v_w=256, v_blk=COL_BV // 256)
        o_c = _attention(functools.partial(_attn_c_kernel, lam_init=lam_init), "attn_c", up, up, up,
                         [c_lambda[l], c_subln[l].reshape(1, LANE)], **common,
                         q_w=512, q_blk=COL_CQ // 512, k_w=512, k_blk=COL_CK // 512, v_w=512, v_blk=COL_CV // 512)
        o_d = _attention(_attn_d_kernel, "attn_d", dq, dk, dv, [], **common,
                         q_w=N_HEADS * MLA_QK_PAD, q_blk=0, k_w=N_HEADS * MLA_QK_PAD, k_blk=0,
                         v_w=N_HEADS * LANE, v_blk=0)

        y = _merge([o_a, o_b, o_c, o_d], gates, w_branch[l], n_tiles=act_tiles)
        mix = _matmul(y, w_out[l], n_tiles=act_tiles, name="out_proj")
        x_all, h2 = _post_ln(x_all, mix, table(l, 2), ln_mix_g[l], ln_mix_b[l],
                             table(l, 4, True), table(l, 3),
                             n_rows=act_rows, alpha=alpha, out_rows=n_rows)
        ffn = _ffn(h2, w_ffn_in[l], w_ffn_out[l], n_tiles=act_tiles)
        if last:
            x_all, _ = _post_ln(x_all, ffn, table(l, 5), ln_ffn_g[l], ln_ffn_b[l], None, None,
                                n_rows=act_rows, alpha=alpha, out_rows=n_lat_rows)
        else:
            x_all, h = _post_ln(x_all, ffn, table(l, 5), ln_ffn_g[l], ln_ffn_b[l],
                                table(l + 1, 1, True), table(l + 1, 0),
                                n_rows=act_rows, alpha=alpha, out_rows=n_rows)
    return x_all.reshape(batch, seq, d)
```

```python
import functools
import math

import numpy as np
import jax
import jax.numpy as jnp
from jax import lax
from jax.experimental import pallas as pl
from jax.experimental.pallas import tpu as pltpu

F32 = jnp.float32
BF16 = jnp.bfloat16

D_MODEL = 2048
GRID_W = 64
HEAD_DIM = 128
WINDOW = 128
ROPE_THETA = 10000.0
NORM_EPS = 1e-6
NEG_INF = -1e30
N_HEADS = 4
KV_HEADS = 2
C_QK_DIM = 64
MLA_RANK = 512
MLA_NOPE = 128
MLA_ROPE = 64
MLA_QK_PAD = 256
BRANCH_W = 512
N_BRANCH = 4
FFN_HIDDEN = 5632
QKV_W = 4672
LANE = 128

UP_W = 5120
PROJ_TN = 512
COL_AQ, COL_AK, COL_AV = 0, 512, 768
COL_BQ, COL_BK, COL_BV = 1024, 1536, 1792
COL_CQ, COL_CK, COL_CV = 2048, 2560, 3072
COL_DQA, COL_DKVA, COL_DKR = 3584, 4096, 4608

TM = 1024
TE = 256
TQ = 256
FFN_ACC_TN = 256
MIB = 1024 * 1024


def _cp(sem, vmem_mib):
    return pltpu.CompilerParams(dimension_semantics=sem, vmem_limit_bytes=vmem_mib * MIB)


def _dot(a, b):
    return jnp.dot(a, b, preferred_element_type=F32)


def _dot_nt(a, b):
    return lax.dot_general(a, b, (((1,), (1,)), ((), ())), preferred_element_type=F32)


def _ada_kernel(c_ref, w_ref, b_ref, o_ref):
    k = pl.program_id(1)
    c = c_ref[...]
    a = (c * (0.5 * jnp.tanh(0.5 * c) + 0.5)).astype(BF16)
    part = _dot(a, w_ref[...].astype(BF16))

    @pl.when(k == 0)
    def _():
        o_ref[...] = part + b_ref[...]

    @pl.when(k > 0)
    def _():
        o_ref[...] += part


def _ada_all(c_all, w_ada, b_ada):
    depth, d, n6 = w_ada.shape
    tk = 128
    return pl.pallas_call(
        _ada_kernel,
        grid=(depth, d // tk),
        in_specs=[pl.BlockSpec((8, tk), lambda l, k: (0, k)),
                  pl.BlockSpec((None, tk, n6), lambda l, k: (l, k, 0)),
                  pl.BlockSpec((None, 1, n6), lambda l, k: (l, 0, 0))],
        out_specs=pl.BlockSpec((None, 8, n6), lambda l, k: (l, 0, 0)),
        out_shape=jax.ShapeDtypeStruct((depth, 8, n6), F32),
        compiler_params=_cp(("arbitrary", "arbitrary"), 40),
        name="ada",
    )(c_all, w_ada, b_ada.reshape(depth, 1, n6))


def _standardise(z):
    mu = jnp.mean(z, axis=-1, keepdims=True)
    zc = z - mu
    var = jnp.mean(zc * zc, axis=-1, keepdims=True)
    return zc * lax.rsqrt(var + NORM_EPS)


def _init_kernel(x_ref, ctx_ref, sc_ref, sh_ref, xo_ref, ho_ref, *, n_lat):
    i = pl.program_id(0)

    def body(src):
        xs = _standardise(src[...])
        xo_ref[...] = xs
        ho_ref[...] = (xs * sc_ref[0] + sh_ref[0]).astype(BF16)

    @pl.when(i < n_lat)
    def _():
        body(x_ref)

    @pl.when(i >= n_lat)
    def _():
        body(ctx_ref)


def _init(x2, ctx2, sc_tbl, sh_tbl):
    n_lat_rows, d = x2.shape
    n_ctx_rows = ctx2.shape[0]
    n_lat = n_lat_rows // TE
    n_all = (n_lat_rows + n_ctx_rows) // TE
    tbl = pl.BlockSpec((1, 1, d), lambda i: ((i * TE) // TM, 0, 0))
    row = pl.BlockSpec((TE, d), lambda i: (i, 0))
    return pl.pallas_call(
        functools.partial(_init_kernel, n_lat=n_lat),
        grid=(n_all,),
        in_specs=[pl.BlockSpec((TE, d), lambda i: (jnp.minimum(i, n_lat - 1), 0)),
                  pl.BlockSpec((TE, d), lambda i: (jnp.maximum(i - n_lat, 0), 0)),
                  tbl, tbl],
        out_specs=[row, row],
        out_shape=[jax.ShapeDtypeStruct((n_lat_rows + n_ctx_rows, d), F32),
                   jax.ShapeDtypeStruct((n_lat_rows + n_ctx_rows, d), BF16)],
        compiler_params=_cp(("arbitrary",), 32),
        name="init_norm",
    )(x2, ctx2, sc_tbl, sh_tbl)


def _ln_kernel(x_ref, d_ref, gate_ref, lng_ref, lnb_ref, *rest, alpha, with_h):
    z = alpha * x_ref[...] + gate_ref[0] * d_ref[...].astype(F32)
    xn = _standardise(z) * lng_ref[...] + lnb_ref[...]
    if with_h:
        sc_ref, sh_ref, xo_ref, ho_ref = rest
        xo_ref[...] = xn
        ho_ref[...] = (xn * sc_ref[0] + sh_ref[0]).astype(BF16)
    else:
        (xo_ref,) = rest
        xo_ref[...] = xn


def _post_ln(x, delta, gate_tbl, ln_g, ln_b, sc_tbl, sh_tbl, *, n_rows, alpha, out_rows):
    d = x.shape[1]
    with_h = sc_tbl is not None
    tbl = pl.BlockSpec((1, 1, d), lambda i: ((i * TE) // TM, 0, 0))
    row = pl.BlockSpec((TE, d), lambda i: (i, 0))
    vec = pl.BlockSpec((1, d), lambda i: (0, 0))
    in_specs = [row, row, tbl, vec, vec]
    args = [x, delta, gate_tbl, ln_g.reshape(1, d), ln_b.reshape(1, d)]
    out_specs = [row]
    out_shape = [jax.ShapeDtypeStruct((out_rows, d), F32)]
    if with_h:
        in_specs += [tbl, tbl]
        args += [sc_tbl, sh_tbl]
        out_specs.append(row)
        out_shape.append(jax.ShapeDtypeStruct((out_rows, d), BF16))
    res = pl.pallas_call(
        functools.partial(_ln_kernel, alpha=alpha, with_h=with_h),
        grid=(n_rows // TE,),
        in_specs=in_specs, out_specs=out_specs, out_shape=out_shape,
        compiler_params=_cp(("arbitrary",), 32),
        name="post_ln",
    )(*args)
    return res if with_h else (res[0], None)


def _swap_select(x, near, far, bit):
    lane = lax.broadcasted_iota(jnp.int32, x.shape, 1)
    return jnp.where((lane & bit) == 0, pltpu.roll(x, far, 1), pltpu.roll(x, near, 1))


def _rope128(x, cos, sin):
    return x * cos + _swap_select(x, 32, LANE - 32, 32) * sin


def _rope64(x, cos, sin):
    return x * cos + _swap_select(x, 16, LANE - 16, 16) * sin


def _rms(x, g):
    return x * lax.rsqrt(jnp.mean(x * x, axis=-1, keepdims=True) + NORM_EPS) * g


def _proj_kernel(h_ref, w_ref, tab_ref, aqk_ref, mq_ref, mkv_ref, o_ref, u_ref):
    j = pl.program_id(1)
    narrow = COL_DKR // PROJ_TN

    @pl.when(j != narrow)
    def _():
        u_ref[...] = _dot(h_ref[...], w_ref[...].astype(BF16))

    @pl.when(j == narrow)
    def _():
        u_ref[:, :LANE] = _dot(h_ref[...], w_ref[:, :LANE].astype(BF16))

    def head(k):
        return u_ref[:, k * LANE:(k + 1) * LANE]

    def put(k, val):
        o_ref[:, k * LANE:(k + 1) * LANE] = val.astype(BF16)

    def tabs128():
        return tab_ref[0], tab_ref[1]

    def tabs64():
        return tab_ref[2], tab_ref[3]

    @pl.when(j == COL_AQ // PROJ_TN)
    def _():
        cos, sin = tabs128()
        for k in range(4):
            put(k, _rope128(_rms(head(k), aqk_ref[0:1, :]), cos, sin) * (HEAD_DIM ** -0.5))

    @pl.when(j == COL_AK // PROJ_TN)
    def _():
        cos, sin = tabs128()
        for k in range(2):
            put(k, _rope128(_rms(head(k), aqk_ref[1:2, :]), cos, sin))
        for k in range(2, 4):
            put(k, head(k))

    @pl.when(j == COL_BQ // PROJ_TN)
    def _():
        cos, sin = tabs128()
        for k in range(4):
            put(k, _rope128(head(k), cos, sin) * (HEAD_DIM ** -0.5))

    @pl.when(j == COL_BK // PROJ_TN)
    def _():
        cos, sin = tabs128()
        for k in range(2):
            put(k, _rope128(head(k), cos, sin))
        for k in range(2, 4):
            put(k, head(k))

    @pl.when(j == COL_CQ // PROJ_TN)
    def _():
        cos, sin = tabs64()
        for k in range(4):
            put(k, _rope64(head(k), cos, sin) * (C_QK_DIM ** -0.5))

    @pl.when(j == COL_CK // PROJ_TN)
    def _():
        cos, sin = tabs64()
        for k in range(4):
            put(k, _rope64(head(k), cos, sin))

    @pl.when(j == COL_CV // PROJ_TN)
    def _():
        o_ref[...] = u_ref[...].astype(BF16)

    @pl.when(j == COL_DQA // PROJ_TN)
    def _():
        o_ref[...] = _rms(u_ref[...], mq_ref[...]).astype(BF16)

    @pl.when(j == COL_DKVA // PROJ_TN)
    def _():
        o_ref[...] = _rms(u_ref[...], mkv_ref[...]).astype(BF16)

    @pl.when(j == narrow)
    def _():
        cos, sin = tabs64()
        x = head(0)
        lane = lax.broadcasted_iota(jnp.int32, x.shape, 1)
        put(0, jnp.where(lane < MLA_ROPE, _rope64(x, cos, sin), 0.0))
        o_ref[:, LANE:] = jnp.zeros((o_ref.shape[0], PROJ_TN - LANE), BF16)


def _tab_index(i, n_lat_tiles, tiles_per_seq):
    return jnp.where(i < n_lat_tiles, i % tiles_per_seq, tiles_per_seq)


def _proj(h, w_in, tabs, aqk, mq, mkv, *, layer, n_lat_tiles, tiles_per_seq):
    n, d = h.shape
    nt = n // TM
    return pl.pallas_call(
        _proj_kernel,
        grid=(nt, UP_W // PROJ_TN),
        in_specs=[pl.BlockSpec((TM, d), lambda i, j: (i, 0)),
                  pl.BlockSpec((None, d, PROJ_TN), lambda i, j: (layer, 0, j)),
                  pl.BlockSpec((4, TM, LANE), lambda i, j: (0, _tab_index(i, n_lat_tiles, tiles_per_seq), 0)),
                  pl.BlockSpec((2, LANE), lambda i, j: (0, 0)),
                  pl.BlockSpec((1, MLA_RANK), lambda i, j: (0, 0)),
                  pl.BlockSpec((1, MLA_RANK), lambda i, j: (0, 0))],
        out_specs=pl.BlockSpec((TM, PROJ_TN), lambda i, j: (i, j)),
        out_shape=jax.ShapeDtypeStruct((n, UP_W), BF16),
        scratch_shapes=[pltpu.VMEM((TM, PROJ_TN), F32)],
        compiler_params=_cp(("arbitrary", "arbitrary"), 48),
        name="proj",
    )(h, w_in, tabs, aqk, mq.reshape(1, MLA_RANK), mkv.reshape(1, MLA_RANK))


def _mla_up_kernel(qa_ref, kva_ref, kr_ref, wq_ref, wkv_ref, tab_ref, dq_ref, dk_ref, dv_ref):
    cos, sin = tab_ref[2], tab_ref[3]
    q = _dot(qa_ref[...], wq_ref[...].astype(BF16))
    kv = _dot(kva_ref[...], wkv_ref[...].astype(BF16))
    scale = (MLA_NOPE + MLA_ROPE) ** -0.5
    kr = kr_ref[...]
    for hh in range(N_HEADS):
        base = hh * MLA_QK_PAD
        dq_ref[:, base:base + LANE] = (q[:, base:base + LANE] * scale).astype(BF16)
        dq_ref[:, base + LANE:base + 2 * LANE] = (
            _rope64(q[:, base + LANE:base + 2 * LANE], cos, sin) * scale).astype(BF16)
        dk_ref[:, base:base + LANE] = kv[:, hh * LANE:(hh + 1) * LANE].astype(BF16)
        dk_ref[:, base + LANE:base + 2 * LANE] = kr
    dv_ref[...] = kv[:, N_HEADS * LANE:].astype(BF16)


def _mla_up(up, wq_pad, wkv_perm, tabs, *, layer, n_lat_tiles, tiles_per_seq):
    n = up.shape[0]
    nt = n // TM
    wide = N_HEADS * MLA_QK_PAD
    return pl.pallas_call(
        _mla_up_kernel,
        grid=(nt,),
        in_specs=[pl.BlockSpec((TM, MLA_RANK), lambda i: (i, COL_DQA // MLA_RANK)),
                  pl.BlockSpec((TM, MLA_RANK), lambda i: (i, COL_DKVA // MLA_RANK)),
                  pl.BlockSpec((TM, LANE), lambda i: (i, COL_DKR // LANE)),
                  pl.BlockSpec((None, MLA_RANK, wide), lambda i: (layer, 0, 0)),
                  pl.BlockSpec((None, MLA_RANK, wide), lambda i: (layer, 0, 0)),
                  pl.BlockSpec((4, TM, LANE), lambda i: (0, _tab_index(i, n_lat_tiles, tiles_per_seq), 0))],
        out_specs=[pl.BlockSpec((TM, wide), lambda i: (i, 0)),
                   pl.BlockSpec((TM, wide), lambda i: (i, 0)),
                   pl.BlockSpec((TM, N_HEADS * LANE), lambda i: (i, 0))],
        out_shape=[jax.ShapeDtypeStruct((n, wide), BF16),
                   jax.ShapeDtypeStruct((n, wide), BF16),
                   jax.ShapeDtypeStruct((n, N_HEADS * LANE), BF16)],
        compiler_params=_cp(("arbitrary",), 48),
        name="mla_up",
    )(up, up, up, wq_pad, wkv_perm, tabs)


def _gate_kernel(h_ref, w_ref, o_ref):
    g = _dot(h_ref[...], w_ref[...])
    o_ref[...] = (0.5 * jnp.tanh(0.5 * g) + 0.5).astype(BF16)


def _gates(h, w_gate_bf16, *, layer):
    n, d = h.shape
    wn = w_gate_bf16.shape[2]
    tn = 1024
    return pl.pallas_call(
        _gate_kernel,
        grid=(n // TM, wn // tn),
        in_specs=[pl.BlockSpec((TM, d), lambda i, j: (i, 0)),
                  pl.BlockSpec((None, d, tn), lambda i, j: (layer, 0, j))],
        out_specs=pl.BlockSpec((TM, tn), lambda i, j: (i, j)),
        out_shape=jax.ShapeDtypeStruct((n, wn), BF16),
        compiler_params=_cp(("arbitrary", "arbitrary"), 48),
        name="gates",
    )(h, w_gate_bf16)


def _attend(q, pieces, extra=None):
    scores = []
    for k, _, mask in pieces:
        s = _dot_nt(q, k)
        if mask is not None:
            s = jnp.where(mask, s, NEG_INF)
        scores.append(s)
    m = functools.reduce(jnp.maximum, [jnp.max(s, axis=-1, keepdims=True) for s in scores])
    if extra is not None:
        m = jnp.maximum(m, extra)
    den = None
    out = None
    for s, (_, v, _) in zip(scores, pieces):
        e = jnp.exp(s - m)
        part = jnp.sum(e, axis=-1, keepdims=True)
        pv = _dot(e.astype(BF16), v)
        den = part if den is None else den + part
        out = pv if out is None else out + pv
    if extra is not None:
        den = den + jnp.exp(extra - m)
    return out / den


def _heads(ref, k, width=LANE):
    return ref[:, k * width:(k + 1) * width]


def _attn_a_kernel(q_ref, kl_ref, vl_ref, kc_ref, vc_ref, o_ref, *, nq):
    t = pl.program_id(1)

    def run(with_latent):
        for kvh in range(KV_HEADS):
            q2 = jnp.concatenate([_heads(q_ref, 2 * kvh), _heads(q_ref, 2 * kvh + 1)], axis=0)
            pieces = [(_heads(kc_ref, kvh), _heads(vc_ref, kvh), None)]
            if with_latent:
                pieces.append((_heads(kl_ref, kvh), _heads(vl_ref, kvh), None))
            o = _attend(q2, pieces)
            o_ref[:, (2 * kvh) * LANE:(2 * kvh + 1) * LANE] = o[:TQ].astype(BF16)
            o_ref[:, (2 * kvh + 1) * LANE:(2 * kvh + 2) * LANE] = o[TQ:].astype(BF16)

    @pl.when(t < nq)
    def _():
        run(True)

    @pl.when(t >= nq)
    def _():
        run(False)


def _attn_b_kernel(q_ref, kl_ref, vl_ref, kc_ref, vc_ref, sink_ref, o_ref, *, nq, seq, win_k):
    t = pl.program_id(1)

    def run(with_latent):
        if with_latent:
            q0 = t * TQ
            start = pl.multiple_of(jnp.clip(q0 - WINDOW, 0, seq - win_k), LANE)
            r = lax.broadcasted_iota(jnp.int32, (2 * TQ, win_k), 0)
            c = lax.broadcasted_iota(jnp.int32, (2 * TQ, win_k), 1)
            qpos = q0 + jnp.where(r >= TQ, r - TQ, r)
            mask = jnp.abs(qpos - (start + c)) <= WINDOW
        for kvh in range(KV_HEADS):
            q2 = jnp.concatenate([_heads(q_ref, 2 * kvh), _heads(q_ref, 2 * kvh + 1)], axis=0)
            r1 = lax.broadcasted_iota(jnp.int32, (2 * TQ, 1), 0)
            sink = jnp.where(r1 >= TQ, sink_ref[0:1, 2 * kvh + 1:2 * kvh + 2],
                             sink_ref[0:1, 2 * kvh:2 * kvh + 1])
            pieces = [(_heads(kc_ref, kvh), _heads(vc_ref, kvh), None)]
            if with_latent:
                kw = kl_ref[pl.ds(start, win_k), kvh * LANE:(kvh + 1) * LANE]
                vw = vl_ref[pl.ds(start, win_k), kvh * LANE:(kvh + 1) * LANE]
                pieces.append((kw, vw, mask))
            o = _attend(q2, pieces, extra=sink)
            o_ref[:, (2 * kvh) * LANE:(2 * kvh + 1) * LANE] = o[:TQ].astype(BF16)
            o_ref[:, (2 * kvh + 1) * LANE:(2 * kvh + 2) * LANE] = o[TQ:].astype(BF16)

    @pl.when(t < nq)
    def _():
        run(True)

    @pl.when(t >= nq)
    def _():
        run(False)


def _attn_c_kernel(q_ref, kl_ref, vl_ref, kc_ref, vc_ref, lam_ref, g_ref, o_ref, *, nq, lam_init):
    t = pl.program_id(1)
    lam = (jnp.exp(jnp.sum(lam_ref[0:1, :] * lam_ref[1:2, :], axis=-1, keepdims=True))
           - jnp.exp(jnp.sum(lam_ref[2:3, :] * lam_ref[3:4, :], axis=-1, keepdims=True)) + lam_init)

    def run(with_latent):
        lane = lax.broadcasted_iota(jnp.int32, (TQ, LANE), 1)
        for hh in range(N_HEADS):
            q = _heads(q_ref, hh)
            zero = jnp.zeros_like(q)
            q2 = jnp.concatenate([jnp.where(lane < C_QK_DIM, q, zero),
                                  jnp.where(lane >= C_QK_DIM, q, zero)], axis=0)
            pieces = [(_heads(kc_ref, hh), _heads(vc_ref, hh), None)]
            if with_latent:
                pieces.append((_heads(kl_ref, hh), _heads(vl_ref, hh), None))
            o = _attend(q2, pieces)
            o = o[:TQ] - lam * o[TQ:]
            o = _rms(o, g_ref[...]) * (1.0 - lam_init)
            o_ref[:, hh * LANE:(hh + 1) * LANE] = o.astype(BF16)

    @pl.when(t < nq)
    def _():
        run(True)

    @pl.when(t >= nq)
    def _():
        run(False)


def _attn_d_kernel(q_ref, kl_ref, vl_ref, kc_ref, vc_ref, o_ref, *, nq):
    t = pl.program_id(1)

    def run(with_latent):
        for hh in range(N_HEADS):
            q = _heads(q_ref, hh, MLA_QK_PAD)
            pieces = [(_heads(kc_ref, hh, MLA_QK_PAD), _heads(vc_ref, hh), None)]
            if with_latent:
                pieces.append((_heads(kl_ref, hh, MLA_QK_PAD), _heads(vl_ref, hh), None))
            o_ref[:, hh * LANE:(hh + 1) * LANE] = _attend(q, pieces).astype(BF16)

    @pl.when(t < nq)
    def _():
        run(True)

    @pl.when(t >= nq)
    def _():
        run(False)


def _attention(kern, name, q_src, k_src, v_src, extras, *, batch, seq, ctx, with_ctx_queries,
               q_w, q_blk, k_w, k_blk, v_w, v_blk):
    assert ctx == TQ
    n = batch * (seq + ctx) if with_ctx_queries else batch * seq
    nq = seq // TQ
    steps = nq + (1 if with_ctx_queries else 0)
    lat_blocks = batch * seq // TQ

    def qrow(b, t):
        return jnp.where(t < nq, b * nq + t, lat_blocks + b)

    in_specs = [pl.BlockSpec((TQ, q_w), lambda b, t: (qrow(b, t), q_blk)),
                pl.BlockSpec((seq, k_w), lambda b, t: (b, k_blk)),
                pl.BlockSpec((seq, v_w), lambda b, t: (b, v_blk)),
                pl.BlockSpec((ctx, k_w), lambda b, t: (batch * seq // ctx + b, k_blk)),
                pl.BlockSpec((ctx, v_w), lambda b, t: (batch * seq // ctx + b, v_blk))]
    args = [q_src, k_src, v_src, k_src, v_src]
    for e in extras:
        in_specs.append(pl.BlockSpec(e.shape, lambda b, t: (0, 0)))
        args.append(e)
    return pl.pallas_call(
        functools.partial(kern, nq=nq),
        grid=(batch, steps),
        in_specs=in_specs,
        out_specs=pl.BlockSpec((TQ, BRANCH_W), lambda b, t: (qrow(b, t), 0)),
        out_shape=jax.ShapeDtypeStruct((n, BRANCH_W), BF16),
        compiler_params=_cp(("arbitrary", "arbitrary"), 48),
        name=name,
    )(*args)


def _merge_kernel(oa_ref, ob_ref, oc_ref, od_ref, ga_ref, gb_ref, gc_ref, gd_ref, w_ref, y_ref):
    acc = None
    for m, (o_ref, g_ref) in enumerate(((oa_ref, ga_ref), (ob_ref, gb_ref), (oc_ref, gc_ref), (od_ref, gd_ref))):
        p = g_ref[...].astype(F32) * _dot(o_ref[...], w_ref[m].astype(BF16))
        acc = p if acc is None else acc + p
    y_ref[...] = acc.astype(BF16)


def _merge(outs, gates, w_branch, *, layer, n_tiles):
    n = n_tiles * TM
    d = w_branch.shape[3]
    tn = 512
    nb = d // tn
    o_spec = pl.BlockSpec((TM, BRANCH_W), lambda i, j: (i, 0))
    g_specs = [pl.BlockSpec((TM, tn), functools.partial(lambda i, j, m: (i, m * nb + j), m=m))
               for m in range(N_BRANCH)]
    return pl.pallas_call(
        _merge_kernel,
        grid=(n_tiles, nb),
        in_specs=[o_spec] * N_BRANCH + g_specs + [pl.BlockSpec((None, N_BRANCH, BRANCH_W, tn), lambda i, j: (layer, 0, 0, j))],
        out_specs=pl.BlockSpec((TM, tn), lambda i, j: (i, j)),
        out_shape=jax.ShapeDtypeStruct((n, d), BF16),
        compiler_params=_cp(("arbitrary", "arbitrary"), 48),
        name="merge",
    )(*outs, gates, gates, gates, gates, w_branch)


def _matmul_kernel(a_ref, w_ref, o_ref):
    o_ref[...] = _dot(a_ref[...], w_ref[...].astype(BF16)).astype(o_ref.dtype)


def _matmul(a, w, *, layer, n_tiles, tn=512, name="matmul"):
    k = a.shape[1]
    n = n_tiles * TM
    d = w.shape[2]
    return pl.pallas_call(
        _matmul_kernel,
        grid=(n_tiles, d // tn),
        in_specs=[pl.BlockSpec((TM, k), lambda i, j: (i, 0)),
                  pl.BlockSpec((None, k, tn), lambda i, j: (layer, 0, j))],
        out_specs=pl.BlockSpec((TM, tn), lambda i, j: (i, j)),
        out_shape=jax.ShapeDtypeStruct((n, d), BF16),
        compiler_params=_cp(("arbitrary", "arbitrary"), 48),
        name=name,
    )(a, w)


def _ffn_kernel(h_ref, wg_ref, wu_ref, wo_ref, o_ref, acc_ref):
    f = pl.program_id(1)

    @pl.when(f == 0)
    def _():
        acc_ref[...] = jnp.zeros_like(acc_ref)

    h = h_ref[...]
    g = _dot(h, wg_ref[...].astype(BF16))
    u = _dot(h, wu_ref[...].astype(BF16))
    a = (g * (0.5 * jnp.tanh(0.5 * g) + 0.5) * u).astype(BF16)

    for n in range(acc_ref.shape[1] // FFN_ACC_TN):
        cols = slice(n * FFN_ACC_TN, (n + 1) * FFN_ACC_TN)
        acc_ref[:, cols] += _dot(a, wo_ref[:, cols].astype(BF16))

    @pl.when(f == pl.num_programs(1) - 1)
    def _():
        o_ref[...] = acc_ref[...].astype(BF16)


def _ffn(h, w_ffn_in, w_ffn_out, *, layer, n_tiles):
    d = h.shape[1]
    n = n_tiles * TM
    hidden = w_ffn_out.shape[1]
    tf = 256
    nf = hidden // tf
    return pl.pallas_call(
        _ffn_kernel,
        grid=(n_tiles, nf),
        in_specs=[pl.BlockSpec((TM, d), lambda i, f: (i, 0)),
                  pl.BlockSpec((None, d, tf), lambda i, f: (layer, 0, f)),
                  pl.BlockSpec((None, d, tf), lambda i, f: (layer, 0, nf + f)),
                  pl.BlockSpec((None, tf, d), lambda i, f: (layer, f, 0))],
        out_specs=pl.BlockSpec((TM, d), lambda i, f: (i, 0)),
        out_shape=jax.ShapeDtypeStruct((n, d), BF16),
        scratch_shapes=[pltpu.VMEM((TM, d), F32)],
        compiler_params=_cp(("arbitrary", "arbitrary"), 56),
        name="ffn",
    )(h, w_ffn_in, w_ffn_in, w_ffn_out)


def _rope_tables(seq, pad_rows):
    t = jnp.arange(seq, dtype=jnp.int32)
    pos_row = (t // GRID_W).astype(F32)[:, None]
    pos_col = (t % GRID_W).astype(F32)[:, None]

    def one(rot_dim, reps):
        axis_dim = rot_dim // 2
        freqs = ROPE_THETA ** (-jnp.arange(0, axis_dim, 2, dtype=F32) / axis_dim)
        ar, ac = pos_row * freqs[None, :], pos_col * freqs[None, :]
        cos = jnp.concatenate([jnp.cos(ar), jnp.cos(ar), jnp.cos(ac), jnp.cos(ac)], axis=-1)
        sin = jnp.concatenate([-jnp.sin(ar), jnp.sin(ar), -jnp.sin(ac), jnp.sin(ac)], axis=-1)
        return jnp.tile(cos, (1, reps)), jnp.tile(sin, (1, reps))

    c128, s128 = one(HEAD_DIM, 1)
    c64, s64 = one(C_QK_DIM, 2)
    ones = jnp.ones((pad_rows, LANE), F32)
    zeros = jnp.zeros((pad_rows, LANE), F32)
    return jnp.stack([jnp.concatenate([c128, ones]), jnp.concatenate([s128, zeros]),
                      jnp.concatenate([c64, ones]), jnp.concatenate([s64, zeros])])


def kernel(x, c, ctx, c_ctx, w_ada, b_ada, w_in, a_qk_norm, b_sink, c_lambda, c_subln, mla_q_norm, mla_w_q_up, mla_kv_norm, mla_w_kv_up, w_branch, w_out, ln_mix_g, ln_mix_b, w_ffn_in, w_ffn_out, ln_ffn_g, ln_ffn_b):
    batch, seq, d = x.shape
    ctx_len = ctx.shape[1]
    depth = w_ada.shape[0]
    assert d == D_MODEL and seq % TM == 0 and (batch * ctx_len) % TM == 0 and batch + 1 <= 8
    n_lat_rows = batch * seq
    n_rows = n_lat_rows + batch * ctx_len
    tiles_per_seq = seq // TM
    n_lat_tiles = n_lat_rows // TM
    n_tiles = n_rows // TM
    alpha = (2 * depth) ** 0.25

    c_all = jnp.concatenate([c, c_ctx[None, :], jnp.zeros((8 - batch - 1, d), F32)], axis=0)
    ada = _ada_all(c_all, w_ada, b_ada)
    tile_row = np.array([b for b in range(batch) for _ in range(tiles_per_seq)]
                        + [batch] * (n_tiles - n_lat_tiles), np.int32)

    def table(l, chunk, plus_one=False):
        v = ada[l, :, chunk * d:(chunk + 1) * d][tile_row]
        return ((1.0 + v) if plus_one else v)[:, None, :]

    tabs = _rope_tables(seq, TM)
    x_all, h = _init(x.reshape(n_lat_rows, d), ctx.reshape(batch * ctx_len, d),
                     table(0, 1, True), table(0, 0))

    win_k = min(TQ + 2 * WINDOW, seq)
    sink_pad = jnp.zeros((depth, 8, LANE), F32).at[:, 0, :N_HEADS].set(b_sink)
    wq_pad = jnp.pad(mla_w_q_up.reshape(depth, MLA_RANK, N_HEADS, MLA_NOPE + MLA_ROPE),
                     ((0, 0), (0, 0), (0, 0), (0, MLA_QK_PAD - MLA_NOPE - MLA_ROPE))
                     ).reshape(depth, MLA_RANK, N_HEADS * MLA_QK_PAD)
    wkv_pad = jnp.pad(mla_w_kv_up.reshape(depth, MLA_RANK, N_HEADS, 2, LANE).transpose(0, 1, 3, 2, 4)
                      .reshape(depth, MLA_RANK, 2 * N_HEADS * LANE),
                      ((0, 0), (0, 0), (0, N_HEADS * MLA_QK_PAD - 2 * N_HEADS * LANE)))

    w_gate = w_in[:, :, QKV_W:].astype(BF16)

    for l in range(depth):
        last = l == depth - 1
        act_tiles = n_lat_tiles if last else n_tiles
        act_rows = act_tiles * TM
        lam_init = 0.8 - 0.6 * math.exp(-0.3 * l)

        up = _proj(h, w_in, tabs, a_qk_norm[l], mla_q_norm[l], mla_kv_norm[l],
                   layer=l, n_lat_tiles=n_lat_tiles, tiles_per_seq=tiles_per_seq)
        dq, dk, dv = _mla_up(up, wq_pad, wkv_pad, tabs,
                             layer=l, n_lat_tiles=n_lat_tiles, tiles_per_seq=tiles_per_seq)
        gates = _gates(h, w_gate, layer=l)

        common = dict(batch=batch, seq=seq, ctx=ctx_len, with_ctx_queries=not last)
        o_a = _attention(_attn_a_kernel, "attn_a", up, up, up, [], **common,
                         q_w=512, q_blk=COL_AQ // 512, k_w=256, k_blk=COL_AK // 256, v_w=256, v_blk=COL_AV // 256)
        o_b = _attention(functools.partial(_attn_b_kernel, seq=seq, win_k=win_k), "attn_b", up, up, up,
                         [sink_pad[l]], **common,
                         q_w=512, q_blk=COL_BQ // 512, k_w=256, k_blk=COL_BK // 256, v_w=256, v_blk=COL_BV // 256)
        o_c = _attention(functools.partial(_attn_c_kernel, lam_init=lam_init), "attn_c", up, up, up,
                         [c_lambda[l], c_subln[l].reshape(1, LANE)], **common,
                         q_w=512, q_blk=COL_CQ // 512, k_w=512, k_blk=COL_CK // 512, v_w=512, v_blk=COL_CV // 512)
        o_d = _attention(_attn_d_kernel, "attn_d", dq, dk, dv, [], **common,
                         q_w=N_HEADS * MLA_QK_PAD, q_blk=0, k_w=N_HEADS * MLA_QK_PAD, k_blk=0,
                         v_w=N_HEADS * LANE, v_blk=0)

        y = _merge([o_a, o_b, o_c, o_d], gates, w_branch, layer=l, n_tiles=act_tiles)
        mix = _matmul(y, w_out, layer=l, n_tiles=act_tiles, name="out_proj")
        x_all, h2 = _post_ln(x_all, mix, table(l, 2), ln_mix_g[l], ln_mix_b[l],
                             table(l, 4, True), table(l, 3),
                             n_rows=act_rows, alpha=alpha, out_rows=act_rows)
        ffn = _ffn(h2, w_ffn_in, w_ffn_out, layer=l, n_tiles=act_tiles)
        if last:
            x_all, _ = _post_ln(x_all, ffn, table(l, 5), ln_ffn_g[l], ln_ffn_b[l], None, None,
                                n_rows=act_rows, alpha=alpha, out_rows=n_lat_rows)
        else:
            x_all, h = _post_ln(x_all, ffn, table(l, 5), ln_ffn_g[l], ln_ffn_b[l],
                                table(l + 1, 1, True), table(l + 1, 0),
                                n_rows=act_rows, alpha=alpha, out_rows=n_rows)
    return x_all.reshape(batch, seq, d)
```

```python
import functools
import math

import numpy as np
import jax
import jax.numpy as jnp
from jax import lax
from jax.experimental import pallas as pl
from jax.experimental.pallas import tpu as pltpu

F32 = jnp.float32
BF16 = jnp.bfloat16

D_MODEL = 2048
GRID_W = 64
HEAD_DIM = 128
WINDOW = 128
ROPE_THETA = 10000.0
NORM_EPS = 1e-6
NEG_INF = -1e30
N_HEADS = 4
KV_HEADS = 2
C_QK_DIM = 64
MLA_RANK = 512
MLA_NOPE = 128
MLA_ROPE = 64
MLA_QK_PAD = 256
BRANCH_W = 512
N_BRANCH = 4
FFN_HIDDEN = 5632
QKV_W = 4672
LANE = 128
GATE_LANE_OFF = QKV_W % LANE

UP_W = 5120
PROJ_TN = 512
COL_AQ, COL_AK, COL_AV = 0, 512, 768
COL_BQ, COL_BK, COL_BV = 1024, 1536, 1792
COL_CQ, COL_CK, COL_CV = 2048, 2560, 3072
COL_DQA, COL_DKVA, COL_DKR = 3584, 4096, 4608

TM = 1024
TE = 256
TQ = 256
FFN_ACC_TN = 256
MIB = 1024 * 1024


def _cp(sem, vmem_mib):
    return pltpu.CompilerParams(dimension_semantics=sem, vmem_limit_bytes=vmem_mib * MIB)


def _dot(a, b):
    return jnp.dot(a, b, preferred_element_type=F32)


def _dot_nt(a, b):
    return lax.dot_general(a, b, (((1,), (1,)), ((), ())), preferred_element_type=F32)


def _ada_kernel(c_ref, w_ref, b_ref, o_ref):
    k = pl.program_id(1)
    c = c_ref[...]
    a = (c * (0.5 * jnp.tanh(0.5 * c) + 0.5)).astype(BF16)
    part = _dot(a, w_ref[...].astype(BF16))

    @pl.when(k == 0)
    def _():
        o_ref[...] = part + b_ref[...]

    @pl.when(k > 0)
    def _():
        o_ref[...] += part


def _ada_all(c_all, w_ada, b_ada):
    depth, d, n6 = w_ada.shape
    tk = 128
    return pl.pallas_call(
        _ada_kernel,
        grid=(depth, d // tk),
        in_specs=[pl.BlockSpec((8, tk), lambda l, k: (0, k)),
                  pl.BlockSpec((None, tk, n6), lambda l, k: (l, k, 0)),
                  pl.BlockSpec((None, 1, n6), lambda l, k: (l, 0, 0))],
        out_specs=pl.BlockSpec((None, 8, n6), lambda l, k: (l, 0, 0)),
        out_shape=jax.ShapeDtypeStruct((depth, 8, n6), F32),
        compiler_params=_cp(("arbitrary", "arbitrary"), 40),
        name="ada",
    )(c_all, w_ada, b_ada.reshape(depth, 1, n6))


def _standardise(z):
    mu = jnp.mean(z, axis=-1, keepdims=True)
    zc = z - mu
    var = jnp.mean(zc * zc, axis=-1, keepdims=True)
    return zc * lax.rsqrt(var + NORM_EPS)


def _init_kernel(x_ref, ctx_ref, sc_ref, sh_ref, xo_ref, ho_ref, *, n_lat):
    i = pl.program_id(0)

    def body(src):
        xs = _standardise(src[...])
        xo_ref[...] = xs
        ho_ref[...] = (xs * sc_ref[0] + sh_ref[0]).astype(BF16)

    @pl.when(i < n_lat)
    def _():
        body(x_ref)

    @pl.when(i >= n_lat)
    def _():
        body(ctx_ref)


def _init(x2, ctx2, sc_tbl, sh_tbl):
    n_lat_rows, d = x2.shape
    n_ctx_rows = ctx2.shape[0]
    n_lat = n_lat_rows // TE
    n_all = (n_lat_rows + n_ctx_rows) // TE
    tbl = pl.BlockSpec((1, 1, d), lambda i: ((i * TE) // TM, 0, 0))
    row = pl.BlockSpec((TE, d), lambda i: (i, 0))
    return pl.pallas_call(
        functools.partial(_init_kernel, n_lat=n_lat),
        grid=(n_all,),
        in_specs=[pl.BlockSpec((TE, d), lambda i: (jnp.minimum(i, n_lat - 1), 0)),
                  pl.BlockSpec((TE, d), lambda i: (jnp.maximum(i - n_lat, 0), 0)),
                  tbl, tbl],
        out_specs=[row, row],
        out_shape=[jax.ShapeDtypeStruct((n_lat_rows + n_ctx_rows, d), F32),
                   jax.ShapeDtypeStruct((n_lat_rows + n_ctx_rows, d), BF16)],
        compiler_params=_cp(("arbitrary",), 32),
        name="init_norm",
    )(x2, ctx2, sc_tbl, sh_tbl)


def _ln_kernel(x_ref, d_ref, gate_ref, lng_ref, lnb_ref, *rest, alpha, with_h):
    z = alpha * x_ref[...] + gate_ref[0] * d_ref[...].astype(F32)
    xn = _standardise(z) * lng_ref[...] + lnb_ref[...]
    if with_h:
        sc_ref, sh_ref, xo_ref, ho_ref = rest
        xo_ref[...] = xn
        ho_ref[...] = (xn * sc_ref[0] + sh_ref[0]).astype(BF16)
    else:
        (xo_ref,) = rest
        xo_ref[...] = xn


def _post_ln(x, delta, gate_tbl, ln_g, ln_b, sc_tbl, sh_tbl, *, n_rows, alpha, out_rows):
    d = x.shape[1]
    with_h = sc_tbl is not None
    tbl = pl.BlockSpec((1, 1, d), lambda i: ((i * TE) // TM, 0, 0))
    row = pl.BlockSpec((TE, d), lambda i: (i, 0))
    vec = pl.BlockSpec((1, d), lambda i: (0, 0))
    in_specs = [row, row, tbl, vec, vec]
    args = [x, delta, gate_tbl, ln_g.reshape(1, d), ln_b.reshape(1, d)]
    out_specs = [row]
    out_shape = [jax.ShapeDtypeStruct((out_rows, d), F32)]
    if with_h:
        in_specs += [tbl, tbl]
        args += [sc_tbl, sh_tbl]
        out_specs.append(row)
        out_shape.append(jax.ShapeDtypeStruct((out_rows, d), BF16))
    res = pl.pallas_call(
        functools.partial(_ln_kernel, alpha=alpha, with_h=with_h),
        grid=(n_rows // TE,),
        in_specs=in_specs, out_specs=out_specs, out_shape=out_shape,
        compiler_params=_cp(("arbitrary",), 32),
        name="post_ln",
    )(*args)
    return res if with_h else (res[0], None)


def _swap_select(x, near, far, bit):
    lane = lax.broadcasted_iota(jnp.int32, x.shape, 1)
    return jnp.where((lane & bit) == 0, pltpu.roll(x, far, 1), pltpu.roll(x, near, 1))


def _rope128(x, cos, sin):
    return x * cos + _swap_select(x, 32, LANE - 32, 32) * sin


def _rope64(x, cos, sin):
    return x * cos + _swap_select(x, 16, LANE - 16, 16) * sin


def _rms(x, g):
    return x * lax.rsqrt(jnp.mean(x * x, axis=-1, keepdims=True) + NORM_EPS) * g


def _proj_kernel(h_ref, w_ref, tab_ref, aqk_ref, mq_ref, mkv_ref, o_ref, u_ref):
    j = pl.program_id(1)
    narrow = COL_DKR // PROJ_TN

    @pl.when(j != narrow)
    def _():
        u_ref[...] = _dot(h_ref[...], w_ref[...].astype(BF16))

    @pl.when(j == narrow)
    def _():
        u_ref[:, :LANE] = _dot(h_ref[...], w_ref[:, :LANE].astype(BF16))

    def head(k):
        return u_ref[:, k * LANE:(k + 1) * LANE]

    def put(k, val):
        o_ref[:, k * LANE:(k + 1) * LANE] = val.astype(BF16)

    def tabs128():
        return tab_ref[0], tab_ref[1]

    def tabs64():
        return tab_ref[2], tab_ref[3]

    @pl.when(j == COL_AQ // PROJ_TN)
    def _():
        cos, sin = tabs128()
        for k in range(4):
            put(k, _rope128(_rms(head(k), aqk_ref[0:1, :]), cos, sin) * (HEAD_DIM ** -0.5))

    @pl.when(j == COL_AK // PROJ_TN)
    def _():
        cos, sin = tabs128()
        for k in range(2):
            put(k, _rope128(_rms(head(k), aqk_ref[1:2, :]), cos, sin))
        for k in range(2, 4):
            put(k, head(k))

    @pl.when(j == COL_BQ // PROJ_TN)
    def _():
        cos, sin = tabs128()
        for k in range(4):
            put(k, _rope128(head(k), cos, sin) * (HEAD_DIM ** -0.5))

    @pl.when(j == COL_BK // PROJ_TN)
    def _():
        cos, sin = tabs128()
        for k in range(2):
            put(k, _rope128(head(k), cos, sin))
        for k in range(2, 4):
            put(k, head(k))

    @pl.when(j == COL_CQ // PROJ_TN)
    def _():
        cos, sin = tabs64()
        for k in range(4):
            put(k, _rope64(head(k), cos, sin) * (C_QK_DIM ** -0.5))

    @pl.when(j == COL_CK // PROJ_TN)
    def _():
        cos, sin = tabs64()
        for k in range(4):
            put(k, _rope64(head(k), cos, sin))

    @pl.when(j == COL_CV // PROJ_TN)
    def _():
        o_ref[...] = u_ref[...].astype(BF16)

    @pl.when(j == COL_DQA // PROJ_TN)
    def _():
        o_ref[...] = _rms(u_ref[...], mq_ref[...]).astype(BF16)

    @pl.when(j == COL_DKVA // PROJ_TN)
    def _():
        o_ref[...] = _rms(u_ref[...], mkv_ref[...]).astype(BF16)

    @pl.when(j == narrow)
    def _():
        cos, sin = tabs64()
        x = head(0)
        lane = lax.broadcasted_iota(jnp.int32, x.shape, 1)
        put(0, jnp.where(lane < MLA_ROPE, _rope64(x, cos, sin), 0.0))
        o_ref[:, LANE:] = jnp.zeros((o_ref.shape[0], PROJ_TN - LANE), BF16)


def _tab_index(i, n_lat_tiles, tiles_per_seq):
    return jnp.where(i < n_lat_tiles, i % tiles_per_seq, tiles_per_seq)


def _proj(h, w_in, tabs, aqk, mq, mkv, *, layer, n_lat_tiles, tiles_per_seq):
    n, d = h.shape
    nt = n // TM
    return pl.pallas_call(
        _proj_kernel,
        grid=(nt, UP_W // PROJ_TN),
        in_specs=[pl.BlockSpec((TM, d), lambda i, j: (i, 0)),
                  pl.BlockSpec((None, d, PROJ_TN), lambda i, j: (layer, 0, j)),
                  pl.BlockSpec((4, TM, LANE), lambda i, j: (0, _tab_index(i, n_lat_tiles, tiles_per_seq), 0)),
                  pl.BlockSpec((2, LANE), lambda i, j: (0, 0)),
                  pl.BlockSpec((1, MLA_RANK), lambda i, j: (0, 0)),
                  pl.BlockSpec((1, MLA_RANK), lambda i, j: (0, 0))],
        out_specs=pl.BlockSpec((TM, PROJ_TN), lambda i, j: (i, j)),
        out_shape=jax.ShapeDtypeStruct((n, UP_W), BF16),
        scratch_shapes=[pltpu.VMEM((TM, PROJ_TN), F32)],
        compiler_params=_cp(("arbitrary", "arbitrary"), 48),
        name="proj",
    )(h, w_in, tabs, aqk, mq.reshape(1, MLA_RANK), mkv.reshape(1, MLA_RANK))


def _mla_up_kernel(qa_ref, kva_ref, kr_ref, wq_ref, wkv_ref, tab_ref, dq_ref, dk_ref, dv_ref):
    cos, sin = tab_ref[2], tab_ref[3]
    q = _dot(qa_ref[...], wq_ref[...].astype(BF16))
    kv = _dot(kva_ref[...], wkv_ref[...].astype(BF16))
    scale = (MLA_NOPE + MLA_ROPE) ** -0.5
    kr = kr_ref[...]
    for hh in range(N_HEADS):
        base = hh * MLA_QK_PAD
        dq_ref[:, base:base + LANE] = (q[:, base:base + LANE] * scale).astype(BF16)
        dq_ref[:, base + LANE:base + 2 * LANE] = (
            _rope64(q[:, base + LANE:base + 2 * LANE], cos, sin) * scale).astype(BF16)
        dk_ref[:, base:base + LANE] = kv[:, hh * LANE:(hh + 1) * LANE].astype(BF16)
        dk_ref[:, base + LANE:base + 2 * LANE] = kr
    dv_ref[...] = kv[:, N_HEADS * LANE:].astype(BF16)


def _mla_up(up, wq_pad, wkv_perm, tabs, *, layer, n_lat_tiles, tiles_per_seq):
    n = up.shape[0]
    nt = n // TM
    wide = N_HEADS * MLA_QK_PAD
    return pl.pallas_call(
        _mla_up_kernel,
        grid=(nt,),
        in_specs=[pl.BlockSpec((TM, MLA_RANK), lambda i: (i, COL_DQA // MLA_RANK)),
                  pl.BlockSpec((TM, MLA_RANK), lambda i: (i, COL_DKVA // MLA_RANK)),
                  pl.BlockSpec((TM, LANE), lambda i: (i, COL_DKR // LANE)),
                  pl.BlockSpec((None, MLA_RANK, wide), lambda i: (layer, 0, 0)),
                  pl.BlockSpec((None, MLA_RANK, wide), lambda i: (layer, 0, 0)),
                  pl.BlockSpec((4, TM, LANE), lambda i: (0, _tab_index(i, n_lat_tiles, tiles_per_seq), 0))],
        out_specs=[pl.BlockSpec((TM, wide), lambda i: (i, 0)),
                   pl.BlockSpec((TM, wide), lambda i: (i, 0)),
                   pl.BlockSpec((TM, N_HEADS * LANE), lambda i: (i, 0))],
        out_shape=[jax.ShapeDtypeStruct((n, wide), BF16),
                   jax.ShapeDtypeStruct((n, wide), BF16),
                   jax.ShapeDtypeStruct((n, N_HEADS * LANE), BF16)],
        compiler_params=_cp(("arbitrary",), 48),
        name="mla_up",
    )(up, up, up, wq_pad, wkv_perm, tabs)


def _gate_kernel(h_ref, wa_ref, wb_ref, o_ref, w_scr):
    i = pl.program_id(1)

    @pl.when(i == 0)
    def _():
        groups = wa_ref.shape[1] // LANE
        lane = lax.broadcasted_iota(jnp.int32, (wa_ref.shape[0], LANE), 1)
        low = lane < LANE - GATE_LANE_OFF
        prev = pltpu.roll(wa_ref[:, :LANE], LANE - GATE_LANE_OFF, 1)
        for g in range(groups):
            nxt_src = wa_ref[:, (g + 1) * LANE:(g + 2) * LANE] if g + 1 < groups else wb_ref[...]
            nxt = pltpu.roll(nxt_src, LANE - GATE_LANE_OFF, 1)
            w_scr[:, g * LANE:(g + 1) * LANE] = jnp.where(low, prev, nxt).astype(BF16)
            prev = nxt

    g = _dot(h_ref[...], w_scr[...])
    o_ref[...] = (0.5 * jnp.tanh(0.5 * g) + 0.5).astype(BF16)


def _gates(h, w_in, *, layer):
    n, d = h.shape
    wn = w_in.shape[2] - QKV_W
    tn = 512
    tg = next(t for t in (1536, 1024, 512) if n % t == 0)
    base = QKV_W - GATE_LANE_OFF
    assert base % tn == 0 and wn % tn == 0
    return pl.pallas_call(
        _gate_kernel,
        grid=(wn // tn, n // tg),
        in_specs=[pl.BlockSpec((tg, d), lambda j, i: (i, 0)),
                  pl.BlockSpec((None, d, tn), lambda j, i: (layer, 0, base // tn + j)),
                  pl.BlockSpec((None, d, LANE), lambda j, i: (layer, 0, (base + (j + 1) * tn) // LANE))],
        out_specs=pl.BlockSpec((tg, tn), lambda j, i: (i, j)),
        out_shape=jax.ShapeDtypeStruct((n, wn), BF16),
        scratch_shapes=[pltpu.VMEM((d, tn), BF16)],
        compiler_params=_cp(("arbitrary", "arbitrary"), 48),
        name="gates",
    )(h, w_in, w_in)


def _attend(q, pieces, extra=None):
    scores = []
    for k, _, mask in pieces:
        s = _dot_nt(q, k)
        if mask is not None:
            s = jnp.where(mask, s, NEG_INF)
        scores.append(s)
    m = functools.reduce(jnp.maximum, [jnp.max(s, axis=-1, keepdims=True) for s in scores])
    if extra is not None:
        m = jnp.maximum(m, extra)
    den = None
    out = None
    for s, (_, v, _) in zip(scores, pieces):
        e = jnp.exp(s - m)
        part = jnp.sum(e, axis=-1, keepdims=True)
        pv = _dot(e.astype(BF16), v)
        den = part if den is None else den + part
        out = pv if out is None else out + pv
    if extra is not None:
        den = den + jnp.exp(extra - m)
    return out / den


def _heads(ref, k, width=LANE):
    return ref[:, k * width:(k + 1) * width]


def _attn_a_kernel(q_ref, kl_ref, vl_ref, kc_ref, vc_ref, o_ref, *, nq):
    t = pl.program_id(1)

    def run(with_latent):
        for kvh in range(KV_HEADS):
            q2 = jnp.concatenate([_heads(q_ref, 2 * kvh), _heads(q_ref, 2 * kvh + 1)], axis=0)
            pieces = [(_heads(kc_ref, kvh), _heads(vc_ref, kvh), None)]
            if with_latent:
                pieces.append((_heads(kl_ref, kvh), _heads(vl_ref, kvh), None))
            o = _attend(q2, pieces)
            o_ref[:, (2 * kvh) * LANE:(2 * kvh + 1) * LANE] = o[:TQ].astype(BF16)
            o_ref[:, (2 * kvh + 1) * LANE:(2 * kvh + 2) * LANE] = o[TQ:].astype(BF16)

    @pl.when(t < nq)
    def _():
        run(True)

    @pl.when(t >= nq)
    def _():
        run(False)


def _attn_b_kernel(q_ref, kl_ref, vl_ref, kc_ref, vc_ref, sink_ref, o_ref, *, nq, seq, win_k):
    t = pl.program_id(1)

    def run(with_latent):
        if with_latent:
            q0 = t * TQ
            start = pl.multiple_of(jnp.clip(q0 - WINDOW, 0, seq - win_k), LANE)
            r = lax.broadcasted_iota(jnp.int32, (2 * TQ, win_k), 0)
            c = lax.broadcasted_iota(jnp.int32, (2 * TQ, win_k), 1)
            qpos = q0 + jnp.where(r >= TQ, r - TQ, r)
            mask = jnp.abs(qpos - (start + c)) <= WINDOW
        for kvh in range(KV_HEADS):
            q2 = jnp.concatenate([_heads(q_ref, 2 * kvh), _heads(q_ref, 2 * kvh + 1)], axis=0)
            r1 = lax.broadcasted_iota(jnp.int32, (2 * TQ, 1), 0)
            sink = jnp.where(r1 >= TQ, sink_ref[0:1, 2 * kvh + 1:2 * kvh + 2],
                             sink_ref[0:1, 2 * kvh:2 * kvh + 1])
            pieces = [(_heads(kc_ref, kvh), _heads(vc_ref, kvh), None)]
            if with_latent:
                kw = kl_ref[pl.ds(start, win_k), kvh * LANE:(kvh + 1) * LANE]
                vw = vl_ref[pl.ds(start, win_k), kvh * LANE:(kvh + 1) * LANE]
                pieces.append((kw, vw, mask))
            o = _attend(q2, pieces, extra=sink)
            o_ref[:, (2 * kvh) * LANE:(2 * kvh + 1) * LANE] = o[:TQ].astype(BF16)
            o_ref[:, (2 * kvh + 1) * LANE:(2 * kvh + 2) * LANE] = o[TQ:].astype(BF16)

    @pl.when(t < nq)
    def _():
        run(True)

    @pl.when(t >= nq)
    def _():
        run(False)


def _attn_c_kernel(q_ref, kl_ref, vl_ref, kc_ref, vc_ref, lam_ref, g_ref, o_ref, *, nq, lam_init):
    t = pl.program_id(1)
    lam = (jnp.exp(jnp.sum(lam_ref[0:1, :] * lam_ref[1:2, :], axis=-1, keepdims=True))
           - jnp.exp(jnp.sum(lam_ref[2:3, :] * lam_ref[3:4, :], axis=-1, keepdims=True)) + lam_init)

    def run(with_latent):
        lane = lax.broadcasted_iota(jnp.int32, (TQ, LANE), 1)
        for hh in range(N_HEADS):
            q = _heads(q_ref, hh)
            zero = jnp.zeros_like(q)
            q2 = jnp.concatenate([jnp.where(lane < C_QK_DIM, q, zero),
                                  jnp.where(lane >= C_QK_DIM, q, zero)], axis=0)
            pieces = [(_heads(kc_ref, hh), _heads(vc_ref, hh), None)]
            if with_latent:
                pieces.append((_heads(kl_ref, hh), _heads(vl_ref, hh), None))
            o = _attend(q2, pieces)
            o = o[:TQ] - lam * o[TQ:]
            o = _rms(o, g_ref[...]) * (1.0 - lam_init)
            o_ref[:, hh * LANE:(hh + 1) * LANE] = o.astype(BF16)

    @pl.when(t < nq)
    def _():
        run(True)

    @pl.when(t >= nq)
    def _():
        run(False)


def _attn_d_kernel(q_ref, kl_ref, vl_ref, kc_ref, vc_ref, o_ref, *, nq):
    t = pl.program_id(1)

    def run(with_latent):
        for hh in range(N_HEADS):
            q = _heads(q_ref, hh, MLA_QK_PAD)
            pieces = [(_heads(kc_ref, hh, MLA_QK_PAD), _heads(vc_ref, hh), None)]
            if with_latent:
                pieces.append((_heads(kl_ref, hh, MLA_QK_PAD), _heads(vl_ref, hh), None))
            o_ref[:, hh * LANE:(hh + 1) * LANE] = _attend(q, pieces).astype(BF16)

    @pl.when(t < nq)
    def _():
        run(True)

    @pl.when(t >= nq)
    def _():
        run(False)


def _attention(kern, name, q_src, k_src, v_src, extras, *, batch, seq, ctx, with_ctx_queries,
               q_w, q_blk, k_w, k_blk, v_w, v_blk):
    assert ctx == TQ
    n = batch * (seq + ctx) if with_ctx_queries else batch * seq
    nq = seq // TQ
    steps = nq + (1 if with_ctx_queries else 0)
    lat_blocks = batch * seq // TQ

    def qrow(b, t):
        return jnp.where(t < nq, b * nq + t, lat_blocks + b)

    in_specs = [pl.BlockSpec((TQ, q_w), lambda b, t: (qrow(b, t), q_blk)),
                pl.BlockSpec((seq, k_w), lambda b, t: (b, k_blk)),
                pl.BlockSpec((seq, v_w), lambda b, t: (b, v_blk)),
                pl.BlockSpec((ctx, k_w), lambda b, t: (batch * seq // ctx + b, k_blk)),
                pl.BlockSpec((ctx, v_w), lambda b, t: (batch * seq // ctx + b, v_blk))]
    args = [q_src, k_src, v_src, k_src, v_src]
    for e in extras:
        in_specs.append(pl.BlockSpec(e.shape, lambda b, t: (0, 0)))
        args.append(e)
    return pl.pallas_call(
        functools.partial(kern, nq=nq),
        grid=(batch, steps),
        in_specs=in_specs,
        out_specs=pl.BlockSpec((TQ, BRANCH_W), lambda b, t: (qrow(b, t), 0)),
        out_shape=jax.ShapeDtypeStruct((n, BRANCH_W), BF16),
        compiler_params=_cp(("arbitrary", "arbitrary"), 48),
        name=name,
    )(*args)


def _merge_kernel(oa_ref, ob_ref, oc_ref, od_ref, ga_ref, gb_ref, gc_ref, gd_ref, w_ref, y_ref):
    acc = None
    for m, (o_ref, g_ref) in enumerate(((oa_ref, ga_ref), (ob_ref, gb_ref), (oc_ref, gc_ref), (od_ref, gd_ref))):
        p = g_ref[...].astype(F32) * _dot(o_ref[...], w_ref[m].astype(BF16))
        acc = p if acc is None else acc + p
    y_ref[...] = acc.astype(BF16)


def _merge(outs, gates, w_branch, *, layer, n_tiles):
    n = n_tiles * TM
    d = w_branch.shape[3]
    tn = 512
    nb = d // tn
    o_spec = pl.BlockSpec((TM, BRANCH_W), lambda i, j: (i, 0))
    g_specs = [pl.BlockSpec((TM, tn), functools.partial(lambda i, j, m: (i, m * nb + j), m=m))
               for m in range(N_BRANCH)]
    return pl.pallas_call(
        _merge_kernel,
        grid=(n_tiles, nb),
        in_specs=[o_spec] * N_BRANCH + g_specs + [pl.BlockSpec((None, N_BRANCH, BRANCH_W, tn), lambda i, j: (layer, 0, 0, j))],
        out_specs=pl.BlockSpec((TM, tn), lambda i, j: (i, j)),
        out_shape=jax.ShapeDtypeStruct((n, d), BF16),
        compiler_params=_cp(("arbitrary", "arbitrary"), 48),
        name="merge",
    )(*outs, gates, gates, gates, gates, w_branch)


def _matmul_kernel(a_ref, w_ref, o_ref):
    o_ref[...] = _dot(a_ref[...], w_ref[...].astype(BF16)).astype(o_ref.dtype)


def _matmul(a, w, *, layer, n_tiles, tn=512, name="matmul"):
    k = a.shape[1]
    n = n_tiles * TM
    d = w.shape[2]
    return pl.pallas_call(
        _matmul_kernel,
        grid=(n_tiles, d // tn),
        in_specs=[pl.BlockSpec((TM, k), lambda i, j: (i, 0)),
                  pl.BlockSpec((None, k, tn), lambda i, j: (layer, 0, j))],
        out_specs=pl.BlockSpec((TM, tn), lambda i, j: (i, j)),
        out_shape=jax.ShapeDtypeStruct((n, d), BF16),
        compiler_params=_cp(("arbitrary", "arbitrary"), 48),
        name=name,
    )(a, w)


def _ffn_kernel(h_ref, wg_ref, wu_ref, wo_ref, o_ref, acc_ref):
    f = pl.program_id(1)

    @pl.when(f == 0)
    def _():
        acc_ref[...] = jnp.zeros_like(acc_ref)

    h = h_ref[...]
    g = _dot(h, wg_ref[...].astype(BF16))
    u = _dot(h, wu_ref[...].astype(BF16))
    a = (g * (0.5 * jnp.tanh(0.5 * g) + 0.5) * u).astype(BF16)

    for n in range(acc_ref.shape[1] // FFN_ACC_TN):
        cols = slice(n * FFN_ACC_TN, (n + 1) * FFN_ACC_TN)
        acc_ref[:, cols] += _dot(a, wo_ref[:, cols].astype(BF16))

    @pl.when(f == pl.num_programs(1) - 1)
    def _():
        o_ref[...] = acc_ref[...].astype(BF16)


def _ffn(h, w_ffn_in, w_ffn_out, *, layer, n_tiles):
    d = h.shape[1]
    n = n_tiles * TM
    hidden = w_ffn_out.shape[1]
    tf = 512
    nf = hidden // tf
    return pl.pallas_call(
        _ffn_kernel,
        grid=(n_tiles, nf),
        in_specs=[pl.BlockSpec((TM, d), lambda i, f: (i, 0)),
                  pl.BlockSpec((None, d, tf), lambda i, f: (layer, 0, f)),
                  pl.BlockSpec((None, d, tf), lambda i, f: (layer, 0, nf + f)),
                  pl.BlockSpec((None, tf, d), lambda i, f: (layer, f, 0))],
        out_specs=pl.BlockSpec((TM, d), lambda i, f: (i, 0)),
        out_shape=jax.ShapeDtypeStruct((n, d), BF16),
        scratch_shapes=[pltpu.VMEM((TM, d), F32)],
        compiler_params=_cp(("arbitrary", "arbitrary"), 60),
        name="ffn",
    )(h, w_ffn_in, w_ffn_in, w_ffn_out)


def _rope_tables(seq, pad_rows):
    t = jnp.arange(seq, dtype=jnp.int32)
    pos_row = (t // GRID_W).astype(F32)[:, None]
    pos_col = (t % GRID_W).astype(F32)[:, None]

    def one(rot_dim, reps):
        axis_dim = rot_dim // 2
        freqs = ROPE_THETA ** (-jnp.arange(0, axis_dim, 2, dtype=F32) / axis_dim)
        ar, ac = pos_row * freqs[None, :], pos_col * freqs[None, :]
        cos = jnp.concatenate([jnp.cos(ar), jnp.cos(ar), jnp.cos(ac), jnp.cos(ac)], axis=-1)
        sin = jnp.concatenate([-jnp.sin(ar), jnp.sin(ar), -jnp.sin(ac), jnp.sin(ac)], axis=-1)
        return jnp.tile(cos, (1, reps)), jnp.tile(sin, (1, reps))

    c128, s128 = one(HEAD_DIM, 1)
    c64, s64 = one(C_QK_DIM, 2)
    ones = jnp.ones((pad_rows, LANE), F32)
    zeros = jnp.zeros((pad_rows, LANE), F32)
    return jnp.stack([jnp.concatenate([c128, ones]), jnp.concatenate([s128, zeros]),
                      jnp.concatenate([c64, ones]), jnp.concatenate([s64, zeros])])


def kernel(x, c, ctx, c_ctx, w_ada, b_ada, w_in, a_qk_norm, b_sink, c_lambda, c_subln, mla_q_norm, mla_w_q_up, mla_kv_norm, mla_w_kv_up, w_branch, w_out, ln_mix_g, ln_mix_b, w_ffn_in, w_ffn_out, ln_ffn_g, ln_ffn_b):
    batch, seq, d = x.shape
    ctx_len = ctx.shape[1]
    depth = w_ada.shape[0]
    assert d == D_MODEL and seq % TM == 0 and (batch * ctx_len) % TM == 0 and batch + 1 <= 8
    n_lat_rows = batch * seq
    n_rows = n_lat_rows + batch * ctx_len
    tiles_per_seq = seq // TM
    n_lat_tiles = n_lat_rows // TM
    n_tiles = n_rows // TM
    alpha = (2 * depth) ** 0.25

    c_all = jnp.concatenate([c, c_ctx[None, :], jnp.zeros((8 - batch - 1, d), F32)], axis=0)
    ada = _ada_all(c_all, w_ada, b_ada)
    tile_row = np.array([b for b in range(batch) for _ in range(tiles_per_seq)]
                        + [batch] * (n_tiles - n_lat_tiles), np.int32)

    def table(l, chunk, plus_one=False):
        v = ada[l, :, chunk * d:(chunk + 1) * d][tile_row]
        return ((1.0 + v) if plus_one else v)[:, None, :]

    tabs = _rope_tables(seq, TM)
    x_all, h = _init(x.reshape(n_lat_rows, d), ctx.reshape(batch * ctx_len, d),
                     table(0, 1, True), table(0, 0))

    win_k = min(TQ + 2 * WINDOW, seq)
    sink_pad = jnp.zeros((depth, 8, LANE), F32).at[:, 0, :N_HEADS].set(b_sink)
    wq_pad = jnp.pad(mla_w_q_up.reshape(depth, MLA_RANK, N_HEADS, MLA_NOPE + MLA_ROPE),
                     ((0, 0), (0, 0), (0, 0), (0, MLA_QK_PAD - MLA_NOPE - MLA_ROPE))
                     ).reshape(depth, MLA_RANK, N_HEADS * MLA_QK_PAD)
    wkv_pad = jnp.pad(mla_w_kv_up.reshape(depth, MLA_RANK, N_HEADS, 2, LANE).transpose(0, 1, 3, 2, 4)
                      .reshape(depth, MLA_RANK, 2 * N_HEADS * LANE),
                      ((0, 0), (0, 0), (0, N_HEADS * MLA_QK_PAD - 2 * N_HEADS * LANE)))

    for l in range(depth):
        last = l == depth - 1
        act_tiles = n_lat_tiles if last else n_tiles
        act_rows = act_tiles * TM
        lam_init = 0.8 - 0.6 * math.exp(-0.3 * l)

        up = _proj(h, w_in, tabs, a_qk_norm[l], mla_q_norm[l], mla_kv_norm[l],
                   layer=l, n_lat_tiles=n_lat_tiles, tiles_per_seq=tiles_per_seq)
        dq, dk, dv = _mla_up(up, wq_pad, wkv_pad, tabs,
                             layer=l, n_lat_tiles=n_lat_tiles, tiles_per_seq=tiles_per_seq)
        gates = _gates(h, w_in, layer=l)

        common = dict(batch=batch, seq=seq, ctx=ctx_len, with_ctx_queries=not last)
        o_a = _attention(_attn_a_kernel, "attn_a", up, up, up, [], **common,
                         q_w=512, q_blk=COL_AQ // 512, k_w=256, k_blk=COL_AK // 256, v_w=256, v_blk=COL_AV // 256)
        o_b = _attention(functools.partial(_attn_b_kernel, seq=seq, win_k=win_k), "attn_b", up, up, up,
                         [sink_pad[l]], **common,
                         q_w=512, q_blk=COL_BQ // 512, k_w=256, k_blk=COL_BK // 256, v_w=256, v_blk=COL_BV // 256)
        o_c = _attention(functools.partial(_attn_c_kernel, lam_init=lam_init), "attn_c", up, up, up,
                         [c_lambda[l], c_subln[l].reshape(1, LANE)], **common,
                         q_w=512, q_blk=COL_CQ // 512, k_w=512, k_blk=COL_CK // 512, v_w=512, v_blk=COL_CV // 512)
        o_d = _attention(_attn_d_kernel, "attn_d", dq, dk, dv, [], **common,
                         q_w=N_HEADS * MLA_QK_PAD, q_blk=0, k_w=N_HEADS * MLA_QK_PAD, k_blk=0,
                         v_w=N_HEADS * LANE, v_blk=0)

        y = _merge([o_a, o_b, o_c, o_d], gates, w_branch, layer=l, n_tiles=act_tiles)
        mix = _matmul(y, w_out, layer=l, n_tiles=act_tiles, name="out_proj")
        x_all, h2 = _post_ln(x_all, mix, table(l, 2), ln_mix_g[l], ln_mix_b[l],
                             table(l, 4, True), table(l, 3),
                             n_rows=act_rows, alpha=alpha, out_rows=act_rows)
        ffn = _ffn(h2, w_ffn_in, w_ffn_out, layer=l, n_tiles=act_tiles)
        if last:
            x_all, _ = _post_ln(x_all, ffn, table(l, 5), ln_ffn_g[l], ln_ffn_b[l], None, None,
                                n_rows=act_rows, alpha=alpha, out_rows=n_lat_rows)
        else:
            x_all, h = _post_ln(x_all, ffn, table(l, 5), ln_ffn_g[l], ln_ffn_b[l],
                                table(l + 1, 1, True), table(l + 1, 0),
                                n_rows=act_rows, alpha=alpha, out_rows=n_rows)
    return x_all.reshape(batch, seq, d)
```

```python
import functools
import math

import numpy as np
import jax
import jax.numpy as jnp
from jax import lax
from jax.experimental import pallas as pl
from jax.experimental.pallas import tpu as pltpu

F32 = jnp.float32
BF16 = jnp.bfloat16

D_MODEL = 2048
GRID_W = 64
HEAD_DIM = 128
WINDOW = 128
ROPE_THETA = 10000.0
NORM_EPS = 1e-6
NEG_INF = -1e30
N_HEADS = 4
KV_HEADS = 2
C_QK_DIM = 64
MLA_RANK = 512
MLA_NOPE = 128
MLA_ROPE = 64
MLA_QK_PAD = 256
BRANCH_W = 512
N_BRANCH = 4
FFN_HIDDEN = 5632
QKV_W = 4672
LANE = 128
GATE_ROW_OFF = QKV_W % LANE

UP_W = 5120
PROJ_TN = 512
N_UP_BLOCKS = UP_W // PROJ_TN
PROJ_ROW_SPLIT = 4
COL_AQ, COL_AK, COL_AV = 0, 512, 768
COL_BQ, COL_BK, COL_BV = 1024, 1536, 1792
COL_CQ, COL_CK, COL_CV = 2048, 2560, 3072
COL_DQA, COL_DKVA, COL_DKR = 3584, 4096, 4608

TM = 1024
TE = 256
TQ = 256
FFN_ACC_TN = 256
MIB = 1024 * 1024


def _cp(sem, vmem_mib):
    return pltpu.CompilerParams(dimension_semantics=sem, vmem_limit_bytes=vmem_mib * MIB)


def _dot(a, b):
    return jnp.dot(a, b, preferred_element_type=F32)


def _dot_nt(a, b):
    return lax.dot_general(a, b, (((1,), (1,)), ((), ())), preferred_element_type=F32)


def _ada_kernel(c_ref, w_ref, b_ref, o_ref):
    k = pl.program_id(1)
    c = c_ref[...]
    a = (c * (0.5 * jnp.tanh(0.5 * c) + 0.5)).astype(BF16)
    part = _dot(a, w_ref[...].astype(BF16))

    @pl.when(k == 0)
    def _():
        o_ref[...] = part + b_ref[...]

    @pl.when(k > 0)
    def _():
        o_ref[...] += part


def _ada_all(c_all, w_ada, b_ada):
    depth, d, n6 = w_ada.shape
    tk = 128
    return pl.pallas_call(
        _ada_kernel,
        grid=(depth, d // tk),
        in_specs=[pl.BlockSpec((8, tk), lambda l, k: (0, k)),
                  pl.BlockSpec((None, tk, n6), lambda l, k: (l, k, 0)),
                  pl.BlockSpec((None, 1, n6), lambda l, k: (l, 0, 0))],
        out_specs=pl.BlockSpec((None, 8, n6), lambda l, k: (l, 0, 0)),
        out_shape=jax.ShapeDtypeStruct((depth, 8, n6), F32),
        compiler_params=_cp(("arbitrary", "arbitrary"), 40),
        name="ada",
    )(c_all, w_ada, b_ada.reshape(depth, 1, n6))


def _standardise(z):
    mu = jnp.mean(z, axis=-1, keepdims=True)
    zc = z - mu
    var = jnp.mean(zc * zc, axis=-1, keepdims=True)
    return zc * lax.rsqrt(var + NORM_EPS)


def _init_kernel(x_ref, ctx_ref, sc_ref, sh_ref, xo_ref, ho_ref, *, n_lat):
    i = pl.program_id(0)

    def body(src):
        xs = _standardise(src[...])
        xo_ref[...] = xs
        ho_ref[...] = (xs * sc_ref[0] + sh_ref[0]).astype(BF16)

    @pl.when(i < n_lat)
    def _():
        body(x_ref)

    @pl.when(i >= n_lat)
    def _():
        body(ctx_ref)


def _init(x2, ctx2, sc_tbl, sh_tbl):
    n_lat_rows, d = x2.shape
    n_ctx_rows = ctx2.shape[0]
    n_lat = n_lat_rows // TE
    n_all = (n_lat_rows + n_ctx_rows) // TE
    tbl = pl.BlockSpec((1, 1, d), lambda i: ((i * TE) // TM, 0, 0))
    row = pl.BlockSpec((TE, d), lambda i: (i, 0))
    return pl.pallas_call(
        functools.partial(_init_kernel, n_lat=n_lat),
        grid=(n_all,),
        in_specs=[pl.BlockSpec((TE, d), lambda i: (jnp.minimum(i, n_lat - 1), 0)),
                  pl.BlockSpec((TE, d), lambda i: (jnp.maximum(i - n_lat, 0), 0)),
                  tbl, tbl],
        out_specs=[row, row],
        out_shape=[jax.ShapeDtypeStruct((n_lat_rows + n_ctx_rows, d), F32),
                   jax.ShapeDtypeStruct((n_lat_rows + n_ctx_rows, d), BF16)],
        compiler_params=_cp(("arbitrary",), 32),
        name="init_norm",
    )(x2, ctx2, sc_tbl, sh_tbl)


def _ln_kernel(x_ref, d_ref, gate_ref, lng_ref, lnb_ref, *rest, alpha, with_h):
    z = alpha * x_ref[...] + gate_ref[0] * d_ref[...].astype(F32)
    xn = _standardise(z) * lng_ref[...] + lnb_ref[...]
    if with_h:
        sc_ref, sh_ref, xo_ref, ho_ref = rest
        xo_ref[...] = xn
        ho_ref[...] = (xn * sc_ref[0] + sh_ref[0]).astype(BF16)
    else:
        (xo_ref,) = rest
        xo_ref[...] = xn


def _post_ln(x, delta, gate_tbl, ln_g, ln_b, sc_tbl, sh_tbl, *, n_rows, alpha, out_rows):
    d = x.shape[1]
    with_h = sc_tbl is not None
    tbl = pl.BlockSpec((1, 1, d), lambda i: ((i * TE) // TM, 0, 0))
    row = pl.BlockSpec((TE, d), lambda i: (i, 0))
    vec = pl.BlockSpec((1, d), lambda i: (0, 0))
    in_specs = [row, row, tbl, vec, vec]
    args = [x, delta, gate_tbl, ln_g.reshape(1, d), ln_b.reshape(1, d)]
    out_specs = [row]
    out_shape = [jax.ShapeDtypeStruct((out_rows, d), F32)]
    if with_h:
        in_specs += [tbl, tbl]
        args += [sc_tbl, sh_tbl]
        out_specs.append(row)
        out_shape.append(jax.ShapeDtypeStruct((out_rows, d), BF16))
    res = pl.pallas_call(
        functools.partial(_ln_kernel, alpha=alpha, with_h=with_h),
        grid=(n_rows // TE,),
        in_specs=in_specs, out_specs=out_specs, out_shape=out_shape,
        compiler_params=_cp(("arbitrary",), 32),
        name="post_ln",
    )(*args)
    return res if with_h else (res[0], None)


def _swap_select(x, near, far, bit):
    lane = lax.broadcasted_iota(jnp.int32, x.shape, 1)
    return jnp.where((lane & bit) == 0, pltpu.roll(x, far, 1), pltpu.roll(x, near, 1))


def _rope128(x, cos, sin):
    return x * cos + _swap_select(x, 32, LANE - 32, 32) * sin


def _rope64(x, cos, sin):
    return x * cos + _swap_select(x, 16, LANE - 16, 16) * sin


def _rms(x, g):
    return x * lax.rsqrt(jnp.mean(x * x, axis=-1, keepdims=True) + NORM_EPS) * g


def _proj_kernel(h_ref, wa_ref, wb_ref, tab_ref, aqk_ref, mq_ref, mkv_ref, up_ref, g_ref):
    j = pl.program_id(1)
    piece = h_ref.shape[0] // PROJ_ROW_SPLIT

    def block(k, epilogue, width=PROJ_TN):
        @pl.when(j == k)
        def _():
            w = wa_ref[:width, :].astype(BF16)
            for r in range(PROJ_ROW_SPLIT):
                rows = slice(r * piece, (r + 1) * piece)
                u = _dot_nt(h_ref[rows, :], w)
                cols = [u[:, c * LANE:(c + 1) * LANE] for c in range(width // LANE)]
                outs = epilogue(cols, rows)
                for c, val in enumerate(outs):
                    up_ref[rows, c * LANE:(c + 1) * LANE] = val.astype(BF16)

    def rope128(x, rows):
        return _rope128(x, tab_ref[0, rows, :], tab_ref[1, rows, :])

    def rope64(x, rows):
        return _rope64(x, tab_ref[2, rows, :], tab_ref[3, rows, :])

    block(COL_AQ // PROJ_TN, lambda cols, rows: [
        rope128(_rms(x, aqk_ref[0:1, :]), rows) * (HEAD_DIM ** -0.5) for x in cols])
    block(COL_AK // PROJ_TN, lambda cols, rows: [
        rope128(_rms(x, aqk_ref[1:2, :]), rows) for x in cols[:2]] + cols[2:])
    block(COL_BQ // PROJ_TN, lambda cols, rows: [rope128(x, rows) * (HEAD_DIM ** -0.5) for x in cols])
    block(COL_BK // PROJ_TN, lambda cols, rows: [rope128(x, rows) for x in cols[:2]] + cols[2:])
    block(COL_CQ // PROJ_TN, lambda cols, rows: [rope64(x, rows) * (C_QK_DIM ** -0.5) for x in cols])
    block(COL_CK // PROJ_TN, lambda cols, rows: [rope64(x, rows) for x in cols])
    block(COL_CV // PROJ_TN, lambda cols, rows: cols)

    def rms_wide(g_ref_):
        def epilogue(cols, rows):
            ms = sum(jnp.sum(x * x, axis=-1, keepdims=True) for x in cols) * (1.0 / (len(cols) * LANE))
            inv = lax.rsqrt(ms + NORM_EPS)
            return [x * inv * g_ref_[:, c * LANE:(c + 1) * LANE] for c, x in enumerate(cols)]
        return epilogue

    block(COL_DQA // PROJ_TN, rms_wide(mq_ref))
    block(COL_DKVA // PROJ_TN, rms_wide(mkv_ref))

    def rope_key(cols, rows):
        x = cols[0]
        lane = lax.broadcasted_iota(jnp.int32, x.shape, 1)
        return [jnp.where(lane < MLA_ROPE, rope64(x, rows), 0.0)]

    block(COL_DKR // PROJ_TN, rope_key, width=LANE)

    @pl.when(j == COL_DKR // PROJ_TN)
    def _():
        up_ref[:, LANE:] = jnp.zeros((up_ref.shape[0], PROJ_TN - LANE), BF16)

    @pl.when(j >= N_UP_BLOCKS)
    def _():
        w = jnp.concatenate([wa_ref[GATE_ROW_OFF:, :], wb_ref[...]], axis=0).astype(BF16)
        for r in range(PROJ_ROW_SPLIT):
            rows = slice(r * piece, (r + 1) * piece)
            g = _dot_nt(h_ref[rows, :], w)
            g_ref[rows, :] = (0.5 * jnp.tanh(0.5 * g) + 0.5).astype(BF16)


def _tab_index(i, n_lat_tiles, tiles_per_seq):
    return jnp.where(i < n_lat_tiles, i % tiles_per_seq, tiles_per_seq)


def _proj(h, w_in_t, tabs, aqk, mq, mkv, *, layer, n_lat_tiles, tiles_per_seq):
    n, d = h.shape
    nt = n // TM
    n_gate = w_in_t.shape[1] - QKV_W
    gate_blocks = n_gate // PROJ_TN
    first_gate_blk = (QKV_W - GATE_ROW_OFF) // PROJ_TN
    assert (QKV_W - GATE_ROW_OFF) % PROJ_TN == 0 and n_gate % PROJ_TN == 0 and first_gate_blk == N_UP_BLOCKS - 1

    def wa_idx(i, j):
        return (layer, jnp.where(j < N_UP_BLOCKS, j, j - 1), 0)

    def wb_idx(i, j):
        g = jnp.maximum(j - N_UP_BLOCKS, 0)
        return (layer, (QKV_W - GATE_ROW_OFF + (g + 1) * PROJ_TN) // GATE_ROW_OFF, 0)

    return pl.pallas_call(
        _proj_kernel,
        grid=(nt, N_UP_BLOCKS + gate_blocks),
        in_specs=[pl.BlockSpec((TM, d), lambda i, j: (i, 0)),
                  pl.BlockSpec((None, PROJ_TN, d), wa_idx),
                  pl.BlockSpec((None, GATE_ROW_OFF, d), wb_idx),
                  pl.BlockSpec((4, TM, LANE), lambda i, j: (0, _tab_index(i, n_lat_tiles, tiles_per_seq), 0)),
                  pl.BlockSpec((2, LANE), lambda i, j: (0, 0)),
                  pl.BlockSpec((1, MLA_RANK), lambda i, j: (0, 0)),
                  pl.BlockSpec((1, MLA_RANK), lambda i, j: (0, 0))],
        out_specs=[pl.BlockSpec((TM, PROJ_TN), lambda i, j: (i, jnp.minimum(j, N_UP_BLOCKS - 1))),
                   pl.BlockSpec((TM, PROJ_TN), lambda i, j: (i, jnp.maximum(j - N_UP_BLOCKS, 0)))],
        out_shape=[jax.ShapeDtypeStruct((n, UP_W), BF16),
                   jax.ShapeDtypeStruct((n, n_gate), BF16)],
        compiler_params=_cp(("arbitrary", "arbitrary"), 48),
        name="proj",
    )(h, w_in_t, w_in_t, tabs, aqk, mq.reshape(1, MLA_RANK), mkv.reshape(1, MLA_RANK))


def _mla_up_kernel(qa_ref, kva_ref, kr_ref, wq_ref, wkv_ref, tab_ref, dq_ref, dk_ref, dv_ref):
    cos, sin = tab_ref[2], tab_ref[3]
    q = _dot(qa_ref[...], wq_ref[...].astype(BF16))
    kv = _dot(kva_ref[...], wkv_ref[...].astype(BF16))
    scale = (MLA_NOPE + MLA_ROPE) ** -0.5
    kr = kr_ref[...]
    for hh in range(N_HEADS):
        base = hh * MLA_QK_PAD
        dq_ref[:, base:base + LANE] = (q[:, base:base + LANE] * scale).astype(BF16)
        dq_ref[:, base + LANE:base + 2 * LANE] = (
            _rope64(q[:, base + LANE:base + 2 * LANE], cos, sin) * scale).astype(BF16)
        dk_ref[:, base:base + LANE] = kv[:, hh * LANE:(hh + 1) * LANE].astype(BF16)
        dk_ref[:, base + LANE:base + 2 * LANE] = kr
    dv_ref[...] = kv[:, N_HEADS * LANE:].astype(BF16)


def _mla_up(up, wq_pad, wkv_perm, tabs, *, layer, n_lat_tiles, tiles_per_seq):
    n = up.shape[0]
    nt = n // TM
    wide = N_HEADS * MLA_QK_PAD
    return pl.pallas_call(
        _mla_up_kernel,
        grid=(nt,),
        in_specs=[pl.BlockSpec((TM, MLA_RANK), lambda i: (i, COL_DQA // MLA_RANK)),
                  pl.BlockSpec((TM, MLA_RANK), lambda i: (i, COL_DKVA // MLA_RANK)),
                  pl.BlockSpec((TM, LANE), lambda i: (i, COL_DKR // LANE)),
                  pl.BlockSpec((None, MLA_RANK, wide), lambda i: (layer, 0, 0)),
                  pl.BlockSpec((None, MLA_RANK, wide), lambda i: (layer, 0, 0)),
                  pl.BlockSpec((4, TM, LANE), lambda i: (0, _tab_index(i, n_lat_tiles, tiles_per_seq), 0))],
        out_specs=[pl.BlockSpec((TM, wide), lambda i: (i, 0)),
                   pl.BlockSpec((TM, wide), lambda i: (i, 0)),
                   pl.BlockSpec((TM, N_HEADS * LANE), lambda i: (i, 0))],
        out_shape=[jax.ShapeDtypeStruct((n, wide), BF16),
                   jax.ShapeDtypeStruct((n, wide), BF16),
                   jax.ShapeDtypeStruct((n, N_HEADS * LANE), BF16)],
        compiler_params=_cp(("arbitrary",), 48),
        name="mla_up",
    )(up, up, up, wq_pad, wkv_perm, tabs)


def _attend(q, pieces, extra=None):
    scores = []
    for k, _, mask in pieces:
        s = _dot_nt(q, k)
        if mask is not None:
            s = jnp.where(mask, s, NEG_INF)
        scores.append(s)
    m = functools.reduce(jnp.maximum, [jnp.max(s, axis=-1, keepdims=True) for s in scores])
    if extra is not None:
        m = jnp.maximum(m, extra)
    den = None
    out = None
    for s, (_, v, _) in zip(scores, pieces):
        e = jnp.exp(s - m)
        part = jnp.sum(e, axis=-1, keepdims=True)
        pv = _dot(e.astype(BF16), v)
        den = part if den is None else den + part
        out = pv if out is None else out + pv
    if extra is not None:
        den = den + jnp.exp(extra - m)
    return out / den


def _heads(ref, k, width=LANE):
    return ref[:, k * width:(k + 1) * width]


def _attn_a_kernel(q_ref, kl_ref, vl_ref, kc_ref, vc_ref, o_ref, *, nq):
    t = pl.program_id(1)

    def run(with_latent):
        for kvh in range(KV_HEADS):
            q2 = jnp.concatenate([_heads(q_ref, 2 * kvh), _heads(q_ref, 2 * kvh + 1)], axis=0)
            pieces = [(_heads(kc_ref, kvh), _heads(vc_ref, kvh), None)]
            if with_latent:
                pieces.append((_heads(kl_ref, kvh), _heads(vl_ref, kvh), None))
            o = _attend(q2, pieces)
            o_ref[:, (2 * kvh) * LANE:(2 * kvh + 1) * LANE] = o[:TQ].astype(BF16)
            o_ref[:, (2 * kvh + 1) * LANE:(2 * kvh + 2) * LANE] = o[TQ:].astype(BF16)

    @pl.when(t < nq)
    def _():
        run(True)

    @pl.when(t >= nq)
    def _():
        run(False)


def _attn_b_kernel(q_ref, kl_ref, vl_ref, kc_ref, vc_ref, sink_ref, o_ref, *, nq, seq, win_k):
    t = pl.program_id(1)

    def run(with_latent):
        if with_latent:
            q0 = t * TQ
            start = pl.multiple_of(jnp.clip(q0 - WINDOW, 0, seq - win_k), LANE)
            r = lax.broadcasted_iota(jnp.int32, (2 * TQ, win_k), 0)
            c = lax.broadcasted_iota(jnp.int32, (2 * TQ, win_k), 1)
            qpos = q0 + jnp.where(r >= TQ, r - TQ, r)
            mask = jnp.abs(qpos - (start + c)) <= WINDOW
        for kvh in range(KV_HEADS):
            q2 = jnp.concatenate([_heads(q_ref, 2 * kvh), _heads(q_ref, 2 * kvh + 1)], axis=0)
            r1 = lax.broadcasted_iota(jnp.int32, (2 * TQ, 1), 0)
            sink = jnp.where(r1 >= TQ, sink_ref[0:1, 2 * kvh + 1:2 * kvh + 2],
                             sink_ref[0:1, 2 * kvh:2 * kvh + 1])
            pieces = [(_heads(kc_ref, kvh), _heads(vc_ref, kvh), None)]
            if with_latent:
                kw = kl_ref[pl.ds(start, win_k), kvh * LANE:(kvh + 1) * LANE]
                vw = vl_ref[pl.ds(start, win_k), kvh * LANE:(kvh + 1) * LANE]
                pieces.append((kw, vw, mask))
            o = _attend(q2, pieces, extra=sink)
            o_ref[:, (2 * kvh) * LANE:(2 * kvh + 1) * LANE] = o[:TQ].astype(BF16)
            o_ref[:, (2 * kvh + 1) * LANE:(2 * kvh + 2) * LANE] = o[TQ:].astype(BF16)

    @pl.when(t < nq)
    def _():
        run(True)

    @pl.when(t >= nq)
    def _():
        run(False)


def _attn_c_kernel(q_ref, kl_ref, vl_ref, kc_ref, vc_ref, lam_ref, g_ref, o_ref, *, nq, lam_init):
    t = pl.program_id(1)
    lam = (jnp.exp(jnp.sum(lam_ref[0:1, :] * lam_ref[1:2, :], axis=-1, keepdims=True))
           - jnp.exp(jnp.sum(lam_ref[2:3, :] * lam_ref[3:4, :], axis=-1, keepdims=True)) + lam_init)

    def run(with_latent):
        lane = lax.broadcasted_iota(jnp.int32, (TQ, LANE), 1)
        for hh in range(N_HEADS):
            q = _heads(q_ref, hh)
            zero = jnp.zeros_like(q)
            q2 = jnp.concatenate([jnp.where(lane < C_QK_DIM, q, zero),
                                  jnp.where(lane >= C_QK_DIM, q, zero)], axis=0)
            pieces = [(_heads(kc_ref, hh), _heads(vc_ref, hh), None)]
            if with_latent:
                pieces.append((_heads(kl_ref, hh), _heads(vl_ref, hh), None))
            o = _attend(q2, pieces)
            o = o[:TQ] - lam * o[TQ:]
            o = _rms(o, g_ref[...]) * (1.0 - lam_init)
            o_ref[:, hh * LANE:(hh + 1) * LANE] = o.astype(BF16)

    @pl.when(t < nq)
    def _():
        run(True)

    @pl.when(t >= nq)
    def _():
        run(False)


def _attn_d_kernel(q_ref, kl_ref, vl_ref, kc_ref, vc_ref, o_ref, *, nq):
    t = pl.program_id(1)

    def run(with_latent):
        for hh in range(N_HEADS):
            q = _heads(q_ref, hh, MLA_QK_PAD)
            pieces = [(_heads(kc_ref, hh, MLA_QK_PAD), _heads(vc_ref, hh), None)]
            if with_latent:
                pieces.append((_heads(kl_ref, hh, MLA_QK_PAD), _heads(vl_ref, hh), None))
            o_ref[:, hh * LANE:(hh + 1) * LANE] = _attend(q, pieces).astype(BF16)

    @pl.when(t < nq)
    def _():
        run(True)

    @pl.when(t >= nq)
    def _():
        run(False)


def _attention(kern, name, q_src, k_src, v_src, extras, *, batch, seq, ctx, with_ctx_queries,
               q_w, q_blk, k_w, k_blk, v_w, v_blk):
    assert ctx == TQ
    n = batch * (seq + ctx) if with_ctx_queries else batch * seq
    nq = seq // TQ
    steps = nq + (1 if with_ctx_queries else 0)
    lat_blocks = batch * seq // TQ

    def qrow(b, t):
        return jnp.where(t < nq, b * nq + t, lat_blocks + b)

    in_specs = [pl.BlockSpec((TQ, q_w), lambda b, t: (qrow(b, t), q_blk)),
                pl.BlockSpec((seq, k_w), lambda b, t: (b, k_blk)),
                pl.BlockSpec((seq, v_w), lambda b, t: (b, v_blk)),
                pl.BlockSpec((ctx, k_w), lambda b, t: (batch * seq // ctx + b, k_blk)),
                pl.BlockSpec((ctx, v_w), lambda b, t: (batch * seq // ctx + b, v_blk))]
    args = [q_src, k_src, v_src, k_src, v_src]
    for e in extras:
        in_specs.append(pl.BlockSpec(e.shape, lambda b, t: (0, 0)))
        args.append(e)
    return pl.pallas_call(
        functools.partial(kern, nq=nq),
        grid=(batch, steps),
        in_specs=in_specs,
        out_specs=pl.BlockSpec((TQ, BRANCH_W), lambda b, t: (qrow(b, t), 0)),
        out_shape=jax.ShapeDtypeStruct((n, BRANCH_W), BF16),
        compiler_params=_cp(("arbitrary", "arbitrary"), 48),
        name=name,
    )(*args)


def _merge_kernel(oa_ref, ob_ref, oc_ref, od_ref, ga_ref, gb_ref, gc_ref, gd_ref, w_ref, y_ref):
    acc = None
    for m, (o_ref, g_ref) in enumerate(((oa_ref, ga_ref), (ob_ref, gb_ref), (oc_ref, gc_ref), (od_ref, gd_ref))):
        p = g_ref[...].astype(F32) * _dot(o_ref[...], w_ref[m].astype(BF16))
        acc = p if acc is None else acc + p
    y_ref[...] = acc.astype(BF16)


def _merge(outs, gates, w_branch, *, layer, n_tiles):
    n = n_tiles * TM
    d = w_branch.shape[3]
    tn = 512
    nb = d // tn
    o_spec = pl.BlockSpec((TM, BRANCH_W), lambda i, j: (i, 0))
    g_specs = [pl.BlockSpec((TM, tn), functools.partial(lambda i, j, m: (i, m * nb + j), m=m))
               for m in range(N_BRANCH)]
    return pl.pallas_call(
        _merge_kernel,
        grid=(n_tiles, nb),
        in_specs=[o_spec] * N_BRANCH + g_specs + [pl.BlockSpec((None, N_BRANCH, BRANCH_W, tn), lambda i, j: (layer, 0, 0, j))],
        out_specs=pl.BlockSpec((TM, tn), lambda i, j: (i, j)),
        out_shape=jax.ShapeDtypeStruct((n, d), BF16),
        compiler_params=_cp(("arbitrary", "arbitrary"), 48),
        name="merge",
    )(*outs, gates, gates, gates, gates, w_branch)


def _matmul_kernel(a_ref, w_ref, o_ref):
    o_ref[...] = _dot(a_ref[...], w_ref[...].astype(BF16)).astype(o_ref.dtype)


def _matmul(a, w, *, layer, n_tiles, tn=512, name="matmul"):
    k = a.shape[1]
    n = n_tiles * TM
    d = w.shape[2]
    return pl.pallas_call(
        _matmul_kernel,
        grid=(n_tiles, d // tn),
        in_specs=[pl.BlockSpec((TM, k), lambda i, j: (i, 0)),
                  pl.BlockSpec((None, k, tn), lambda i, j: (layer, 0, j))],
        out_specs=pl.BlockSpec((TM, tn), lambda i, j: (i, j)),
        out_shape=jax.ShapeDtypeStruct((n, d), BF16),
        compiler_params=_cp(("arbitrary", "arbitrary"), 48),
        name=name,
    )(a, w)


def _ffn_kernel(h_ref, wg_ref, wu_ref, wo_ref, o_ref, acc_ref):
    f = pl.program_id(1)

    @pl.when(f == 0)
    def _():
        acc_ref[...] = jnp.zeros_like(acc_ref)

    h = h_ref[...]
    g = _dot(h, wg_ref[...].astype(BF16))
    u = _dot(h, wu_ref[...].astype(BF16))
    a = (g * (0.5 * jnp.tanh(0.5 * g) + 0.5) * u).astype(BF16)

    for n in range(acc_ref.shape[1] // FFN_ACC_TN):
        cols = slice(n * FFN_ACC_TN, (n + 1) * FFN_ACC_TN)
        acc_ref[:, cols] += _dot(a, wo_ref[:, cols].astype(BF16))

    @pl.when(f == pl.num_programs(1) - 1)
    def _():
        o_ref[...] = acc_ref[...].astype(BF16)


def _ffn(h, w_ffn_in, w_ffn_out, *, layer, n_tiles):
    d = h.shape[1]
    n = n_tiles * TM
    hidden = w_ffn_out.shape[1]
    tf = 512
    nf = hidden // tf
    return pl.pallas_call(
        _ffn_kernel,
        grid=(n_tiles, nf),
        in_specs=[pl.BlockSpec((TM, d), lambda i, f: (i, 0)),
                  pl.BlockSpec((None, d, tf), lambda i, f: (layer, 0, f)),
                  pl.BlockSpec((None, d, tf), lambda i, f: (layer, 0, nf + f)),
                  pl.BlockSpec((None, tf, d), lambda i, f: (layer, f, 0))],
        out_specs=pl.BlockSpec((TM, d), lambda i, f: (i, 0)),
        out_shape=jax.ShapeDtypeStruct((n, d), BF16),
        scratch_shapes=[pltpu.VMEM((TM, d), F32)],
        compiler_params=_cp(("arbitrary", "arbitrary"), 60),
        name="ffn",
    )(h, w_ffn_in, w_ffn_in, w_ffn_out)


def _rope_tables(seq, pad_rows):
    t = jnp.arange(seq, dtype=jnp.int32)
    pos_row = (t // GRID_W).astype(F32)[:, None]
    pos_col = (t % GRID_W).astype(F32)[:, None]

    def one(rot_dim, reps):
        axis_dim = rot_dim // 2
        freqs = ROPE_THETA ** (-jnp.arange(0, axis_dim, 2, dtype=F32) / axis_dim)
        ar, ac = pos_row * freqs[None, :], pos_col * freqs[None, :]
        cos = jnp.concatenate([jnp.cos(ar), jnp.cos(ar), jnp.cos(ac), jnp.cos(ac)], axis=-1)
        sin = jnp.concatenate([-jnp.sin(ar), jnp.sin(ar), -jnp.sin(ac), jnp.sin(ac)], axis=-1)
        return jnp.tile(cos, (1, reps)), jnp.tile(sin, (1, reps))

    c128, s128 = one(HEAD_DIM, 1)
    c64, s64 = one(C_QK_DIM, 2)
    ones = jnp.ones((pad_rows, LANE), F32)
    zeros = jnp.zeros((pad_rows, LANE), F32)
    return jnp.stack([jnp.concatenate([c128, ones]), jnp.concatenate([s128, zeros]),
                      jnp.concatenate([c64, ones]), jnp.concatenate([s64, zeros])])


def kernel(x, c, ctx, c_ctx, w_ada, b_ada, w_in, a_qk_norm, b_sink, c_lambda, c_subln, mla_q_norm, mla_w_q_up, mla_kv_norm, mla_w_kv_up, w_branch, w_out, ln_mix_g, ln_mix_b, w_ffn_in, w_ffn_out, ln_ffn_g, ln_ffn_b):
    batch, seq, d = x.shape
    ctx_len = ctx.shape[1]
    depth = w_ada.shape[0]
    assert d == D_MODEL and seq % TM == 0 and (batch * ctx_len) % TM == 0 and batch + 1 <= 8
    n_lat_rows = batch * seq
    n_rows = n_lat_rows + batch * ctx_len
    tiles_per_seq = seq // TM
    n_lat_tiles = n_lat_rows // TM
    n_tiles = n_rows // TM
    alpha = (2 * depth) ** 0.25

    c_all = jnp.concatenate([c, c_ctx[None, :], jnp.zeros((8 - batch - 1, d), F32)], axis=0)
    ada = _ada_all(c_all, w_ada, b_ada)
    tile_row = np.array([b for b in range(batch) for _ in range(tiles_per_seq)]
                        + [batch] * (n_tiles - n_lat_tiles), np.int32)

    def table(l, chunk, plus_one=False):
        v = ada[l, :, chunk * d:(chunk + 1) * d][tile_row]
        return ((1.0 + v) if plus_one else v)[:, None, :]

    tabs = _rope_tables(seq, TM)
    x_all, h = _init(x.reshape(n_lat_rows, d), ctx.reshape(batch * ctx_len, d),
                     table(0, 1, True), table(0, 0))

    win_k = min(TQ + 2 * WINDOW, seq)
    sink_pad = jnp.zeros((depth, 8, LANE), F32).at[:, 0, :N_HEADS].set(b_sink)
    wq_pad = jnp.pad(mla_w_q_up.reshape(depth, MLA_RANK, N_HEADS, MLA_NOPE + MLA_ROPE),
                     ((0, 0), (0, 0), (0, 0), (0, MLA_QK_PAD - MLA_NOPE - MLA_ROPE))
                     ).reshape(depth, MLA_RANK, N_HEADS * MLA_QK_PAD)
    wkv_pad = jnp.pad(mla_w_kv_up.reshape(depth, MLA_RANK, N_HEADS, 2, LANE).transpose(0, 1, 3, 2, 4)
                      .reshape(depth, MLA_RANK, 2 * N_HEADS * LANE),
                      ((0, 0), (0, 0), (0, N_HEADS * MLA_QK_PAD - 2 * N_HEADS * LANE)))

    w_in_t = jnp.swapaxes(w_in, 1, 2)

    for l in range(depth):
        last = l == depth - 1
        act_tiles = n_lat_tiles if last else n_tiles
        act_rows = act_tiles * TM
        lam_init = 0.8 - 0.6 * math.exp(-0.3 * l)

        up, gates = _proj(h, w_in_t, tabs, a_qk_norm[l], mla_q_norm[l], mla_kv_norm[l],
                          layer=l, n_lat_tiles=n_lat_tiles, tiles_per_seq=tiles_per_seq)
        dq, dk, dv = _mla_up(up, wq_pad, wkv_pad, tabs,
                             layer=l, n_lat_tiles=n_lat_tiles, tiles_per_seq=tiles_per_seq)

        common = dict(batch=batch, seq=seq, ctx=ctx_len, with_ctx_queries=not last)
        o_a = _attention(_attn_a_kernel, "attn_a", up, up, up, [], **common,
                         q_w=512, q_blk=COL_AQ // 512, k_w=256, k_blk=COL_AK // 256, v_w=256, v_blk=COL_AV // 256)
        o_b = _attention(functools.partial(_attn_b_kernel, seq=seq, win_k=win_k), "attn_b", up, up, up,
                         [sink_pad[l]], **common,
                         q_w=512, q_blk=COL_BQ // 512, k_w=256, k_blk=COL_BK // 256, v_w=256, v_blk=COL_BV // 256)
        o_c = _attention(functools.partial(_attn_c_kernel, lam_init=lam_init), "attn_c", up, up, up,
                         [c_lambda[l], c_subln[l].reshape(1, LANE)], **common,
                         q_w=512, q_blk=COL_CQ // 512, k_w=512, k_blk=COL_CK // 512, v_w=512, v_blk=COL_CV // 512)
        o_d = _attention(_attn_d_kernel, "attn_d", dq, dk, dv, [], **common,
                         q_w=N_HEADS * MLA_QK_PAD, q_blk=0, k_w=N_HEADS * MLA_QK_PAD, k_blk=0,
                         v_w=N_HEADS * LANE, v_blk=0)

        y = _merge([o_a, o_b, o_c, o_d], gates, w_branch, layer=l, n_tiles=act_tiles)
        mix = _matmul(y, w_out, layer=l, n_tiles=act_tiles, name="out_proj")
        x_all, h2 = _post_ln(x_all, mix, table(l, 2), ln_mix_g[l], ln_mix_b[l],
                             table(l, 4, True), table(l, 3),
                             n_rows=act_rows, alpha=alpha, out_rows=act_rows)
        ffn = _ffn(h2, w_ffn_in, w_ffn_out, layer=l, n_tiles=act_tiles)
        if last:
            x_all, _ = _post_ln(x_all, ffn, table(l, 5), ln_ffn_g[l], ln_ffn_b[l], None, None,
                                n_rows=act_rows, alpha=alpha, out_rows=n_lat_rows)
        else:
            x_all, h = _post_ln(x_all, ffn, table(l, 5), ln_ffn_g[l], ln_ffn_b[l],
                                table(l + 1, 1, True), table(l + 1, 0),
                                n_rows=act_rows, alpha=alpha, out_rows=n_rows)
    return x_all.reshape(batch, seq, d)
```

```python
import functools
import math

import numpy as np
import jax
import jax.numpy as jnp
from jax import lax
from jax.experimental import pallas as pl
from jax.experimental.pallas import tpu as pltpu

F32 = jnp.float32
BF16 = jnp.bfloat16

D_MODEL = 2048
GRID_W = 64
HEAD_DIM = 128
WINDOW = 128
ROPE_THETA = 10000.0
NORM_EPS = 1e-6
NEG_INF = -1e30
N_HEADS = 4
KV_HEADS = 2
C_QK_DIM = 64
MLA_RANK = 512
MLA_NOPE = 128
MLA_ROPE = 64
MLA_QK_PAD = 256
BRANCH_W = 512
N_BRANCH = 4
FFN_HIDDEN = 5632
QKV_W = 4672
LANE = 128
GATE_ROW_OFF = QKV_W % LANE

UP_W = 5120
PROJ_TN = 512
N_UP_BLOCKS = UP_W // PROJ_TN
PROJ_ROW_SPLIT = 4
COL_AQ, COL_AK, COL_AV = 0, 512, 768
COL_BQ, COL_BK, COL_BV = 1024, 1536, 1792
COL_CQ, COL_CK, COL_CV = 2048, 2560, 3072
COL_DQA, COL_DKVA, COL_DKR = 3584, 4096, 4608

TM = 1024
PROJ_ROW_TILES = (2304, 1536, 1024, 512)
TO = 512
TE = 256
TQ = 256
FFN_ACC_TN = 256
MIB = 1024 * 1024


def _cp(sem, vmem_mib):
    return pltpu.CompilerParams(dimension_semantics=sem, vmem_limit_bytes=vmem_mib * MIB)


def _dot(a, b):
    return jnp.dot(a, b, preferred_element_type=F32)


def _dot_nt(a, b):
    return lax.dot_general(a, b, (((1,), (1,)), ((), ())), preferred_element_type=F32)


def _ada_kernel(c_ref, w_ref, b_ref, o_ref):
    k = pl.program_id(1)
    c = c_ref[...]
    a = (c * (0.5 * jnp.tanh(0.5 * c) + 0.5)).astype(BF16)
    part = _dot(a, w_ref[...].astype(BF16))

    @pl.when(k == 0)
    def _():
        o_ref[...] = part + b_ref[...]

    @pl.when(k > 0)
    def _():
        o_ref[...] += part


def _ada_all(c_all, w_ada, b_ada):
    depth, d, n6 = w_ada.shape
    tk = 128
    return pl.pallas_call(
        _ada_kernel,
        grid=(depth, d // tk),
        in_specs=[pl.BlockSpec((8, tk), lambda l, k: (0, k)),
                  pl.BlockSpec((None, tk, n6), lambda l, k: (l, k, 0)),
                  pl.BlockSpec((None, 1, n6), lambda l, k: (l, 0, 0))],
        out_specs=pl.BlockSpec((None, 8, n6), lambda l, k: (l, 0, 0)),
        out_shape=jax.ShapeDtypeStruct((depth, 8, n6), F32),
        compiler_params=_cp(("arbitrary", "arbitrary"), 40),
        name="ada",
    )(c_all, w_ada, b_ada.reshape(depth, 1, n6))


def _standardise(z):
    mu = jnp.mean(z, axis=-1, keepdims=True)
    zc = z - mu
    var = jnp.mean(zc * zc, axis=-1, keepdims=True)
    return zc * lax.rsqrt(var + NORM_EPS)


def _init_kernel(x_ref, ctx_ref, sc_ref, sh_ref, xo_ref, ho_ref, *, n_lat):
    i = pl.program_id(0)

    def body(src):
        xs = _standardise(src[...])
        xo_ref[...] = xs
        ho_ref[...] = (xs * sc_ref[0] + sh_ref[0]).astype(BF16)

    @pl.when(i < n_lat)
    def _():
        body(x_ref)

    @pl.when(i >= n_lat)
    def _():
        body(ctx_ref)


def _init(x2, ctx2, sc_tbl, sh_tbl):
    n_lat_rows, d = x2.shape
    n_ctx_rows = ctx2.shape[0]
    n_lat = n_lat_rows // TE
    n_all = (n_lat_rows + n_ctx_rows) // TE
    tbl = pl.BlockSpec((1, 1, d), lambda i: ((i * TE) // TM, 0, 0))
    row = pl.BlockSpec((TE, d), lambda i: (i, 0))
    return pl.pallas_call(
        functools.partial(_init_kernel, n_lat=n_lat),
        grid=(n_all,),
        in_specs=[pl.BlockSpec((TE, d), lambda i: (jnp.minimum(i, n_lat - 1), 0)),
                  pl.BlockSpec((TE, d), lambda i: (jnp.maximum(i - n_lat, 0), 0)),
                  tbl, tbl],
        out_specs=[row, row],
        out_shape=[jax.ShapeDtypeStruct((n_lat_rows + n_ctx_rows, d), F32),
                   jax.ShapeDtypeStruct((n_lat_rows + n_ctx_rows, d), BF16)],
        compiler_params=_cp(("arbitrary",), 32),
        name="init_norm",
    )(x2, ctx2, sc_tbl, sh_tbl)


def _ln_kernel(x_ref, d_ref, gate_ref, lng_ref, lnb_ref, *rest, alpha, with_h):
    z = alpha * x_ref[...] + gate_ref[0] * d_ref[...].astype(F32)
    xn = _standardise(z) * lng_ref[...] + lnb_ref[...]
    if with_h:
        sc_ref, sh_ref, xo_ref, ho_ref = rest
        xo_ref[...] = xn
        ho_ref[...] = (xn * sc_ref[0] + sh_ref[0]).astype(BF16)
    else:
        (xo_ref,) = rest
        xo_ref[...] = xn


def _post_ln(x, delta, gate_tbl, ln_g, ln_b, sc_tbl, sh_tbl, *, n_rows, alpha, out_rows):
    d = x.shape[1]
    with_h = sc_tbl is not None
    tbl = pl.BlockSpec((1, 1, d), lambda i: ((i * TE) // TM, 0, 0))
    row = pl.BlockSpec((TE, d), lambda i: (i, 0))
    vec = pl.BlockSpec((1, d), lambda i: (0, 0))
    in_specs = [row, row, tbl, vec, vec]
    args = [x, delta, gate_tbl, ln_g.reshape(1, d), ln_b.reshape(1, d)]
    out_specs = [row]
    out_shape = [jax.ShapeDtypeStruct((out_rows, d), F32)]
    if with_h:
        in_specs += [tbl, tbl]
        args += [sc_tbl, sh_tbl]
        out_specs.append(row)
        out_shape.append(jax.ShapeDtypeStruct((out_rows, d), BF16))
    res = pl.pallas_call(
        functools.partial(_ln_kernel, alpha=alpha, with_h=with_h),
        grid=(n_rows // TE,),
        in_specs=in_specs, out_specs=out_specs, out_shape=out_shape,
        compiler_params=_cp(("arbitrary",), 32),
        name="post_ln",
    )(*args)
    return res if with_h else (res[0], None)


def _swap_select(x, near, far, bit):
    lane = lax.broadcasted_iota(jnp.int32, x.shape, 1)
    return jnp.where((lane & bit) == 0, pltpu.roll(x, far, 1), pltpu.roll(x, near, 1))


def _rope128(x, cos, sin):
    return x * cos + _swap_select(x, 32, LANE - 32, 32) * sin


def _rope64(x, cos, sin):
    return x * cos + _swap_select(x, 16, LANE - 16, 16) * sin


def _rms(x, g):
    return x * lax.rsqrt(jnp.mean(x * x, axis=-1, keepdims=True) + NORM_EPS) * g


def _proj_kernel(h_ref, wa_ref, wb_ref, tab_ref, aqk_ref, mq_ref, mkv_ref, up_ref, g_ref):
    j = pl.program_id(1)
    piece = h_ref.shape[0] // PROJ_ROW_SPLIT

    def block(k, epilogue, width=PROJ_TN):
        @pl.when(j == k)
        def _():
            w = wa_ref[:width, :].astype(BF16)
            for r in range(PROJ_ROW_SPLIT):
                rows = slice(r * piece, (r + 1) * piece)
                u = _dot_nt(h_ref[rows, :], w)
                cols = [u[:, c * LANE:(c + 1) * LANE] for c in range(width // LANE)]
                outs = epilogue(cols, rows)
                for c, val in enumerate(outs):
                    up_ref[rows, c * LANE:(c + 1) * LANE] = val.astype(BF16)

    def rope128(x, rows):
        return _rope128(x, tab_ref[0, rows, :], tab_ref[1, rows, :])

    def rope64(x, rows):
        return _rope64(x, tab_ref[2, rows, :], tab_ref[3, rows, :])

    block(COL_AQ // PROJ_TN, lambda cols, rows: [
        rope128(_rms(x, aqk_ref[0:1, :]), rows) * (HEAD_DIM ** -0.5) for x in cols])
    block(COL_AK // PROJ_TN, lambda cols, rows: [
        rope128(_rms(x, aqk_ref[1:2, :]), rows) for x in cols[:2]] + cols[2:])
    block(COL_BQ // PROJ_TN, lambda cols, rows: [rope128(x, rows) * (HEAD_DIM ** -0.5) for x in cols])
    block(COL_BK // PROJ_TN, lambda cols, rows: [rope128(x, rows) for x in cols[:2]] + cols[2:])
    block(COL_CQ // PROJ_TN, lambda cols, rows: [rope64(x, rows) * (C_QK_DIM ** -0.5) for x in cols])
    block(COL_CK // PROJ_TN, lambda cols, rows: [rope64(x, rows) for x in cols])
    block(COL_CV // PROJ_TN, lambda cols, rows: cols)

    def rms_wide(g_ref_):
        def epilogue(cols, rows):
            ms = sum(jnp.sum(x * x, axis=-1, keepdims=True) for x in cols) * (1.0 / (len(cols) * LANE))
            inv = lax.rsqrt(ms + NORM_EPS)
            return [x * inv * g_ref_[:, c * LANE:(c + 1) * LANE] for c, x in enumerate(cols)]
        return epilogue

    block(COL_DQA // PROJ_TN, rms_wide(mq_ref))
    block(COL_DKVA // PROJ_TN, rms_wide(mkv_ref))

    def rope_key(cols, rows):
        x = cols[0]
        lane = lax.broadcasted_iota(jnp.int32, x.shape, 1)
        return [jnp.where(lane < MLA_ROPE, rope64(x, rows), 0.0)]

    block(COL_DKR // PROJ_TN, rope_key, width=LANE)

    @pl.when(j == COL_DKR // PROJ_TN)
    def _():
        up_ref[:, LANE:] = jnp.zeros((up_ref.shape[0], PROJ_TN - LANE), BF16)

    @pl.when(j >= N_UP_BLOCKS)
    def _():
        w = jnp.concatenate([wa_ref[GATE_ROW_OFF:, :], wb_ref[...]], axis=0).astype(BF16)
        for r in range(PROJ_ROW_SPLIT):
            rows = slice(r * piece, (r + 1) * piece)
            g = _dot_nt(h_ref[rows, :], w)
            g_ref[rows, :] = (0.5 * jnp.tanh(0.5 * g) + 0.5).astype(BF16)


def _row_tile(n, choices):
    return next(t for t in choices if n % t == 0)


def _proj(h, w_in_t, tabs, aqk, mq, mkv, *, layer):
    n, d = h.shape
    tp = _row_tile(n, PROJ_ROW_TILES)
    nt = n // tp
    n_gate = w_in_t.shape[1] - QKV_W
    gate_blocks = n_gate // PROJ_TN
    first_gate_blk = (QKV_W - GATE_ROW_OFF) // PROJ_TN
    assert (QKV_W - GATE_ROW_OFF) % PROJ_TN == 0 and n_gate % PROJ_TN == 0 and first_gate_blk == N_UP_BLOCKS - 1

    def wa_idx(i, j):
        return (layer, jnp.where(j < N_UP_BLOCKS, j, j - 1), 0)

    def wb_idx(i, j):
        g = jnp.maximum(j - N_UP_BLOCKS, 0)
        return (layer, (QKV_W - GATE_ROW_OFF + (g + 1) * PROJ_TN) // GATE_ROW_OFF, 0)

    return pl.pallas_call(
        _proj_kernel,
        grid=(nt, N_UP_BLOCKS + gate_blocks),
        in_specs=[pl.BlockSpec((tp, d), lambda i, j: (i, 0)),
                  pl.BlockSpec((None, PROJ_TN, d), wa_idx),
                  pl.BlockSpec((None, GATE_ROW_OFF, d), wb_idx),
                  pl.BlockSpec((4, tp, LANE), lambda i, j: (0, i, 0)),
                  pl.BlockSpec((2, LANE), lambda i, j: (0, 0)),
                  pl.BlockSpec((1, MLA_RANK), lambda i, j: (0, 0)),
                  pl.BlockSpec((1, MLA_RANK), lambda i, j: (0, 0))],
        out_specs=[pl.BlockSpec((tp, PROJ_TN), lambda i, j: (i, jnp.minimum(j, N_UP_BLOCKS - 1))),
                   pl.BlockSpec((tp, PROJ_TN), lambda i, j: (i, jnp.maximum(j - N_UP_BLOCKS, 0)))],
        out_shape=[jax.ShapeDtypeStruct((n, UP_W), BF16),
                   jax.ShapeDtypeStruct((n, n_gate), BF16)],
        compiler_params=_cp(("arbitrary", "arbitrary"), 58),
        name="proj",
    )(h, w_in_t, w_in_t, tabs, aqk, mq.reshape(1, MLA_RANK), mkv.reshape(1, MLA_RANK))


def _mla_up_kernel(qa_ref, kva_ref, kr_ref, wq_ref, wkv_ref, tab_ref, dq_ref, dk_ref, dv_ref):
    cos, sin = tab_ref[2], tab_ref[3]
    q = _dot(qa_ref[...], wq_ref[...].astype(BF16))
    kv = _dot(kva_ref[...], wkv_ref[...].astype(BF16))
    scale = (MLA_NOPE + MLA_ROPE) ** -0.5
    kr = kr_ref[...]
    for hh in range(N_HEADS):
        base = hh * MLA_QK_PAD
        dq_ref[:, base:base + LANE] = (q[:, base:base + LANE] * scale).astype(BF16)
        dq_ref[:, base + LANE:base + 2 * LANE] = (
            _rope64(q[:, base + LANE:base + 2 * LANE], cos, sin) * scale).astype(BF16)
        dk_ref[:, base:base + LANE] = kv[:, hh * LANE:(hh + 1) * LANE].astype(BF16)
        dk_ref[:, base + LANE:base + 2 * LANE] = kr
    dv_ref[...] = kv[:, N_HEADS * LANE:].astype(BF16)


def _mla_up(up, wq_pad, wkv_perm, tabs, *, layer):
    n = up.shape[0]
    nt = n // TM
    wide = N_HEADS * MLA_QK_PAD
    return pl.pallas_call(
        _mla_up_kernel,
        grid=(nt,),
        in_specs=[pl.BlockSpec((TM, MLA_RANK), lambda i: (i, COL_DQA // MLA_RANK)),
                  pl.BlockSpec((TM, MLA_RANK), lambda i: (i, COL_DKVA // MLA_RANK)),
                  pl.BlockSpec((TM, LANE), lambda i: (i, COL_DKR // LANE)),
                  pl.BlockSpec((None, MLA_RANK, wide), lambda i: (layer, 0, 0)),
                  pl.BlockSpec((None, MLA_RANK, wide), lambda i: (layer, 0, 0)),
                  pl.BlockSpec((4, TM, LANE), lambda i: (0, i, 0))],
        out_specs=[pl.BlockSpec((TM, wide), lambda i: (i, 0)),
                   pl.BlockSpec((TM, wide), lambda i: (i, 0)),
                   pl.BlockSpec((TM, N_HEADS * LANE), lambda i: (i, 0))],
        out_shape=[jax.ShapeDtypeStruct((n, wide), BF16),
                   jax.ShapeDtypeStruct((n, wide), BF16),
                   jax.ShapeDtypeStruct((n, N_HEADS * LANE), BF16)],
        compiler_params=_cp(("arbitrary",), 48),
        name="mla_up",
    )(up, up, up, wq_pad, wkv_perm, tabs)


def _attend(q, pieces, extra=None):
    scores = []
    for k, _, mask in pieces:
        s = _dot_nt(q, k)
        if mask is not None:
            s = jnp.where(mask, s, NEG_INF)
        scores.append(s)
    m = functools.reduce(jnp.maximum, [jnp.max(s, axis=-1, keepdims=True) for s in scores])
    if extra is not None:
        m = jnp.maximum(m, extra)
    den = None
    out = None
    for s, (_, v, _) in zip(scores, pieces):
        e = jnp.exp(s - m)
        part = jnp.sum(e, axis=-1, keepdims=True)
        pv = _dot(e.astype(BF16), v)
        den = part if den is None else den + part
        out = pv if out is None else out + pv
    if extra is not None:
        den = den + jnp.exp(extra - m)
    return out / den


def _heads(ref, k, width=LANE):
    return ref[:, k * width:(k + 1) * width]


def _attn_a_kernel(q_ref, kl_ref, vl_ref, kc_ref, vc_ref, o_ref, *, nq):
    t = pl.program_id(1)

    def run(with_latent):
        for kvh in range(KV_HEADS):
            q2 = jnp.concatenate([_heads(q_ref, 2 * kvh), _heads(q_ref, 2 * kvh + 1)], axis=0)
            pieces = [(_heads(kc_ref, kvh), _heads(vc_ref, kvh), None)]
            if with_latent:
                pieces.append((_heads(kl_ref, kvh), _heads(vl_ref, kvh), None))
            o = _attend(q2, pieces)
            o_ref[:, (2 * kvh) * LANE:(2 * kvh + 1) * LANE] = o[:TQ].astype(BF16)
            o_ref[:, (2 * kvh + 1) * LANE:(2 * kvh + 2) * LANE] = o[TQ:].astype(BF16)

    @pl.when(t < nq)
    def _():
        run(True)

    @pl.when(t >= nq)
    def _():
        run(False)


def _attn_b_kernel(q_ref, kl_ref, vl_ref, kc_ref, vc_ref, sink_ref, o_ref, *, nq, seq, win_k):
    t = pl.program_id(1)

    def run(with_latent):
        if with_latent:
            q0 = t * TQ
            start = pl.multiple_of(jnp.clip(q0 - WINDOW, 0, seq - win_k), LANE)
            r = lax.broadcasted_iota(jnp.int32, (2 * TQ, win_k), 0)
            c = lax.broadcasted_iota(jnp.int32, (2 * TQ, win_k), 1)
            qpos = q0 + jnp.where(r >= TQ, r - TQ, r)
            mask = jnp.abs(qpos - (start + c)) <= WINDOW
        for kvh in range(KV_HEADS):
            q2 = jnp.concatenate([_heads(q_ref, 2 * kvh), _heads(q_ref, 2 * kvh + 1)], axis=0)
            r1 = lax.broadcasted_iota(jnp.int32, (2 * TQ, 1), 0)
            sink = jnp.where(r1 >= TQ, sink_ref[0:1, 2 * kvh + 1:2 * kvh + 2],
                             sink_ref[0:1, 2 * kvh:2 * kvh + 1])
            pieces = [(_heads(kc_ref, kvh), _heads(vc_ref, kvh), None)]
            if with_latent:
                kw = kl_ref[pl.ds(start, win_k), kvh * LANE:(kvh + 1) * LANE]
                vw = vl_ref[pl.ds(start, win_k), kvh * LANE:(kvh + 1) * LANE]
                pieces.append((kw, vw, mask))
            o = _attend(q2, pieces, extra=sink)
            o_ref[:, (2 * kvh) * LANE:(2 * kvh + 1) * LANE] = o[:TQ].astype(BF16)
            o_ref[:, (2 * kvh + 1) * LANE:(2 * kvh + 2) * LANE] = o[TQ:].astype(BF16)

    @pl.when(t < nq)
    def _():
        run(True)

    @pl.when(t >= nq)
    def _():
        run(False)


def _attn_c_kernel(q_ref, kl_ref, vl_ref, kc_ref, vc_ref, lam_ref, g_ref, o_ref, *, nq, lam_init):
    t = pl.program_id(1)
    lam = (jnp.exp(jnp.sum(lam_ref[0:1, :] * lam_ref[1:2, :], axis=-1, keepdims=True))
           - jnp.exp(jnp.sum(lam_ref[2:3, :] * lam_ref[3:4, :], axis=-1, keepdims=True)) + lam_init)

    def run(with_latent):
        lane = lax.broadcasted_iota(jnp.int32, (TQ, LANE), 1)
        for hh in range(N_HEADS):
            q = _heads(q_ref, hh)
            zero = jnp.zeros_like(q)
            q2 = jnp.concatenate([jnp.where(lane < C_QK_DIM, q, zero),
                                  jnp.where(lane >= C_QK_DIM, q, zero)], axis=0)
            pieces = [(_heads(kc_ref, hh), _heads(vc_ref, hh), None)]
            if with_latent:
                pieces.append((_heads(kl_ref, hh), _heads(vl_ref, hh), None))
            o = _attend(q2, pieces)
            o = o[:TQ] - lam * o[TQ:]
            o = _rms(o, g_ref[...]) * (1.0 - lam_init)
            o_ref[:, hh * LANE:(hh + 1) * LANE] = o.astype(BF16)

    @pl.when(t < nq)
    def _():
        run(True)

    @pl.when(t >= nq)
    def _():
        run(False)


def _attn_d_kernel(q_ref, kl_ref, vl_ref, kc_ref, vc_ref, o_ref, *, nq):
    t = pl.program_id(1)

    def run(with_latent):
        for hh in range(N_HEADS):
            q = _heads(q_ref, hh, MLA_QK_PAD)
            pieces = [(_heads(kc_ref, hh, MLA_QK_PAD), _heads(vc_ref, hh), None)]
            if with_latent:
                pieces.append((_heads(kl_ref, hh, MLA_QK_PAD), _heads(vl_ref, hh), None))
            o_ref[:, hh * LANE:(hh + 1) * LANE] = _attend(q, pieces).astype(BF16)

    @pl.when(t < nq)
    def _():
        run(True)

    @pl.when(t >= nq)
    def _():
        run(False)


def _attention(kern, name, q_src, k_src, v_src, extras, *, batch, seq, ctx, with_ctx_queries,
               q_w, q_blk, k_w, k_blk, v_w, v_blk):
    assert ctx == TQ
    n = batch * (seq + ctx) if with_ctx_queries else batch * seq
    nq = seq // TQ
    steps = nq + (1 if with_ctx_queries else 0)
    lat_blocks = batch * seq // TQ

    def qrow(b, t):
        return jnp.where(t < nq, b * nq + t, lat_blocks + b)

    in_specs = [pl.BlockSpec((TQ, q_w), lambda b, t: (qrow(b, t), q_blk)),
                pl.BlockSpec((seq, k_w), lambda b, t: (b, k_blk)),
                pl.BlockSpec((seq, v_w), lambda b, t: (b, v_blk)),
                pl.BlockSpec((ctx, k_w), lambda b, t: (batch * seq // ctx + b, k_blk)),
                pl.BlockSpec((ctx, v_w), lambda b, t: (batch * seq // ctx + b, v_blk))]
    args = [q_src, k_src, v_src, k_src, v_src]
    for e in extras:
        in_specs.append(pl.BlockSpec(e.shape, lambda b, t: (0, 0)))
        args.append(e)
    return pl.pallas_call(
        functools.partial(kern, nq=nq),
        grid=(batch, steps),
        in_specs=in_specs,
        out_specs=pl.BlockSpec((TQ, BRANCH_W), lambda b, t: (qrow(b, t), 0)),
        out_shape=jax.ShapeDtypeStruct((n, BRANCH_W), BF16),
        compiler_params=_cp(("arbitrary", "arbitrary"), 48),
        name=name,
    )(*args)


def _merge_kernel(oa_ref, ob_ref, oc_ref, od_ref, ga_ref, gb_ref, gc_ref, gd_ref, w_ref, y_ref):
    acc = None
    for m, (o_ref, g_ref) in enumerate(((oa_ref, ga_ref), (ob_ref, gb_ref), (oc_ref, gc_ref), (od_ref, gd_ref))):
        p = g_ref[...].astype(F32) * _dot(o_ref[...], w_ref[m].astype(BF16))
        acc = p if acc is None else acc + p
    y_ref[...] = acc.astype(BF16)


def _merge(outs, gates, w_branch, *, layer, n_tiles):
    n = n_tiles * TM
    d = w_branch.shape[3]
    tn = 512
    nb = d // tn
    o_spec = pl.BlockSpec((TM, BRANCH_W), lambda i, j: (i, 0))
    g_specs = [pl.BlockSpec((TM, tn), functools.partial(lambda i, j, m: (i, m * nb + j), m=m))
               for m in range(N_BRANCH)]
    return pl.pallas_call(
        _merge_kernel,
        grid=(n_tiles, nb),
        in_specs=[o_spec] * N_BRANCH + g_specs + [pl.BlockSpec((None, N_BRANCH, BRANCH_W, tn), lambda i, j: (layer, 0, 0, j))],
        out_specs=pl.BlockSpec((TM, tn), lambda i, j: (i, j)),
        out_shape=jax.ShapeDtypeStruct((n, d), BF16),
        compiler_params=_cp(("arbitrary", "arbitrary"), 48),
        name="merge",
    )(*outs, gates, gates, gates, gates, w_branch)


def _out_ln_kernel(y_ref, w_ref, x_ref, gate_ref, lng_ref, lnb_ref, sc_ref, sh_ref, xo_ref, ho_ref, *, alpha):
    piece = y_ref.shape[0] // 2
    for r in range(2):
        rows = slice(r * piece, (r + 1) * piece)
        mix = _dot(y_ref[rows, :], w_ref[...])
        z = alpha * x_ref[rows, :] + gate_ref[0] * mix
        xn = _standardise(z) * lng_ref[...] + lnb_ref[...]
        xo_ref[rows, :] = xn
        ho_ref[rows, :] = (xn * sc_ref[0] + sh_ref[0]).astype(BF16)


def _out_ln(y, w_out_bf16, x, gate_tbl, ln_g, ln_b, sc_tbl, sh_tbl, *, layer, n_rows, alpha):
    d = x.shape[1]
    tbl = pl.BlockSpec((1, 1, d), lambda i: ((i * TO) // TM, 0, 0))
    row = pl.BlockSpec((TO, d), lambda i: (i, 0))
    vec = pl.BlockSpec((1, d), lambda i: (0, 0))
    return pl.pallas_call(
        functools.partial(_out_ln_kernel, alpha=alpha),
        grid=(n_rows // TO,),
        in_specs=[row, pl.BlockSpec((None, d, d), lambda i: (layer, 0, 0)), row, tbl, vec, vec, tbl, tbl],
        out_specs=[row, row],
        out_shape=[jax.ShapeDtypeStruct((n_rows, d), F32), jax.ShapeDtypeStruct((n_rows, d), BF16)],
        compiler_params=_cp(("arbitrary",), 56),
        name="out_ln",
    )(y, w_out_bf16, x, gate_tbl, ln_g.reshape(1, d), ln_b.reshape(1, d), sc_tbl, sh_tbl)


def _ffn_kernel(h_ref, wg_ref, wu_ref, wo_ref, o_ref, acc_ref):
    f = pl.program_id(1)

    @pl.when(f == 0)
    def _():
        acc_ref[...] = jnp.zeros_like(acc_ref)

    h = h_ref[...]
    g = _dot(h, wg_ref[...].astype(BF16))
    u = _dot(h, wu_ref[...].astype(BF16))
    a = (g * (0.5 * jnp.tanh(0.5 * g) + 0.5) * u).astype(BF16)

    for n in range(acc_ref.shape[1] // FFN_ACC_TN):
        cols = slice(n * FFN_ACC_TN, (n + 1) * FFN_ACC_TN)
        acc_ref[:, cols] += _dot(a, wo_ref[:, cols].astype(BF16))

    @pl.when(f == pl.num_programs(1) - 1)
    def _():
        o_ref[...] = acc_ref[...].astype(BF16)


def _ffn(h, w_ffn_in, w_ffn_out, *, layer, n_tiles):
    d = h.shape[1]
    n = n_tiles * TM
    hidden = w_ffn_out.shape[1]
    tf = 512
    nf = hidden // tf
    return pl.pallas_call(
        _ffn_kernel,
        grid=(n_tiles, nf),
        in_specs=[pl.BlockSpec((TM, d), lambda i, f: (i, 0)),
                  pl.BlockSpec((None, d, tf), lambda i, f: (layer, 0, f)),
                  pl.BlockSpec((None, d, tf), lambda i, f: (layer, 0, nf + f)),
                  pl.BlockSpec((None, tf, d), lambda i, f: (layer, f, 0))],
        out_specs=pl.BlockSpec((TM, d), lambda i, f: (i, 0)),
        out_shape=jax.ShapeDtypeStruct((n, d), BF16),
        scratch_shapes=[pltpu.VMEM((TM, d), F32)],
        compiler_params=_cp(("arbitrary", "arbitrary"), 60),
        name="ffn",
    )(h, w_ffn_in, w_ffn_in, w_ffn_out)


def _rope_tables(seq, batch, ctx_rows):
    t = jnp.arange(seq, dtype=jnp.int32)
    pos_row = (t // GRID_W).astype(F32)[:, None]
    pos_col = (t % GRID_W).astype(F32)[:, None]

    def one(rot_dim, reps):
        axis_dim = rot_dim // 2
        freqs = ROPE_THETA ** (-jnp.arange(0, axis_dim, 2, dtype=F32) / axis_dim)
        ar, ac = pos_row * freqs[None, :], pos_col * freqs[None, :]
        cos = jnp.concatenate([jnp.cos(ar), jnp.cos(ar), jnp.cos(ac), jnp.cos(ac)], axis=-1)
        sin = jnp.concatenate([-jnp.sin(ar), jnp.sin(ar), -jnp.sin(ac), jnp.sin(ac)], axis=-1)
        return jnp.tile(cos, (1, reps)), jnp.tile(sin, (1, reps))

    c128, s128 = one(HEAD_DIM, 1)
    c64, s64 = one(C_QK_DIM, 2)
    ones = jnp.ones((ctx_rows, LANE), F32)
    zeros = jnp.zeros((ctx_rows, LANE), F32)

    def rows(tab, pad):
        return jnp.concatenate([jnp.tile(tab, (batch, 1)), pad])

    return jnp.stack([rows(c128, ones), rows(s128, zeros), rows(c64, ones), rows(s64, zeros)])


def kernel(x, c, ctx, c_ctx, w_ada, b_ada, w_in, a_qk_norm, b_sink, c_lambda, c_subln, mla_q_norm, mla_w_q_up, mla_kv_norm, mla_w_kv_up, w_branch, w_out, ln_mix_g, ln_mix_b, w_ffn_in, w_ffn_out, ln_ffn_g, ln_ffn_b):
    batch, seq, d = x.shape
    ctx_len = ctx.shape[1]
    depth = w_ada.shape[0]
    assert d == D_MODEL and seq % TM == 0 and (batch * ctx_len) % TM == 0 and batch + 1 <= 8
    n_lat_rows = batch * seq
    n_rows = n_lat_rows + batch * ctx_len
    tiles_per_seq = seq // TM
    n_lat_tiles = n_lat_rows // TM
    n_tiles = n_rows // TM
    alpha = (2 * depth) ** 0.25

    c_all = jnp.concatenate([c, c_ctx[None, :], jnp.zeros((8 - batch - 1, d), F32)], axis=0)
    ada = _ada_all(c_all, w_ada, b_ada)
    tile_row = np.array([b for b in range(batch) for _ in range(tiles_per_seq)]
                        + [batch] * (n_tiles - n_lat_tiles), np.int32)

    def table(l, chunk, plus_one=False):
        v = ada[l, :, chunk * d:(chunk + 1) * d][tile_row]
        return ((1.0 + v) if plus_one else v)[:, None, :]

    tabs = _rope_tables(seq, batch, batch * ctx_len)
    x_all, h = _init(x.reshape(n_lat_rows, d), ctx.reshape(batch * ctx_len, d),
                     table(0, 1, True), table(0, 0))

    win_k = min(TQ + 2 * WINDOW, seq)
    sink_pad = jnp.zeros((depth, 8, LANE), F32).at[:, 0, :N_HEADS].set(b_sink)
    wq_pad = jnp.pad(mla_w_q_up.reshape(depth, MLA_RANK, N_HEADS, MLA_NOPE + MLA_ROPE),
                     ((0, 0), (0, 0), (0, 0), (0, MLA_QK_PAD - MLA_NOPE - MLA_ROPE))
                     ).reshape(depth, MLA_RANK, N_HEADS * MLA_QK_PAD)
    wkv_pad = jnp.pad(mla_w_kv_up.reshape(depth, MLA_RANK, N_HEADS, 2, LANE).transpose(0, 1, 3, 2, 4)
                      .reshape(depth, MLA_RANK, 2 * N_HEADS * LANE),
                      ((0, 0), (0, 0), (0, N_HEADS * MLA_QK_PAD - 2 * N_HEADS * LANE)))

    w_in_t = jnp.swapaxes(w_in, 1, 2)
    w_out_bf16 = w_out.astype(BF16)

    for l in range(depth):
        last = l == depth - 1
        act_tiles = n_lat_tiles if last else n_tiles
        act_rows = act_tiles * TM
        lam_init = 0.8 - 0.6 * math.exp(-0.3 * l)

        up, gates = _proj(h, w_in_t, tabs, a_qk_norm[l], mla_q_norm[l], mla_kv_norm[l], layer=l)
        dq, dk, dv = _mla_up(up, wq_pad, wkv_pad, tabs, layer=l)

        common = dict(batch=batch, seq=seq, ctx=ctx_len, with_ctx_queries=not last)
        o_a = _attention(_attn_a_kernel, "attn_a", up, up, up, [], **common,
                         q_w=512, q_blk=COL_AQ // 512, k_w=256, k_blk=COL_AK // 256, v_w=256, v_blk=COL_AV // 256)
        o_b = _attention(functools.partial(_attn_b_kernel, seq=seq, win_k=win_k), "attn_b", up, up, up,
                         [sink_pad[l]], **common,
                         q_w=512, q_blk=COL_BQ // 512, k_w=256, k_blk=COL_BK // 256, v_w=256, v_blk=COL_BV // 256)
        o_c = _attention(functools.partial(_attn_c_kernel, lam_init=lam_init), "attn_c", up, up, up,
                         [c_lambda[l], c_subln[l].reshape(1, LANE)], **common,
                         q_w=512, q_blk=COL_CQ // 512, k_w=512, k_blk=COL_CK // 512, v_w=512, v_blk=COL_CV // 512)
        o_d = _attention(_attn_d_kernel, "attn_d", dq, dk, dv, [], **common,
                         q_w=N_HEADS * MLA_QK_PAD, q_blk=0, k_w=N_HEADS * MLA_QK_PAD, k_blk=0,
                         v_w=N_HEADS * LANE, v_blk=0)

        y = _merge([o_a, o_b, o_c, o_d], gates, w_branch, layer=l, n_tiles=act_tiles)
        x_all, h2 = _out_ln(y, w_out_bf16, x_all, table(l, 2), ln_mix_g[l], ln_mix_b[l],
                            table(l, 4, True), table(l, 3), layer=l, n_rows=act_rows, alpha=alpha)
        ffn = _ffn(h2, w_ffn_in, w_ffn_out, layer=l, n_tiles=act_tiles)
        if last:
            x_all, _ = _post_ln(x_all, ffn, table(l, 5), ln_ffn_g[l], ln_ffn_b[l], None, None,
                                n_rows=act_rows, alpha=alpha, out_rows=n_lat_rows)
        else:
            x_all, h = _post_ln(x_all, ffn, table(l, 5), ln_ffn_g[l], ln_ffn_b[l],
                                table(l + 1, 1, True), table(l + 1, 0),
                                n_rows=act_rows, alpha=alpha, out_rows=n_rows)
    return x_all.reshape(batch, seq, d)
```

```python
import functools
import math

import numpy as np
import jax
import jax.numpy as jnp
from jax import lax
from jax.experimental import pallas as pl
from jax.experimental.pallas import tpu as pltpu

F32 = jnp.float32
BF16 = jnp.bfloat16

D_MODEL = 2048
GRID_W = 64
HEAD_DIM = 128
WINDOW = 128
ROPE_THETA = 10000.0
NORM_EPS = 1e-6
NEG_INF = -1e30
N_HEADS = 4
KV_HEADS = 2
C_QK_DIM = 64
MLA_RANK = 512
MLA_NOPE = 128
MLA_ROPE = 64
MLA_QK_PAD = 256
BRANCH_W = 512
N_BRANCH = 4
FFN_HIDDEN = 5632
QKV_W = 4672
LANE = 128
GATE_ROW_OFF = QKV_W % LANE

UP_W = 5120
PROJ_TN = 512
N_UP_BLOCKS = UP_W // PROJ_TN
PROJ_ROW_SPLIT = 4
COL_AQ, COL_AK, COL_AV = 0, 512, 768
COL_BQ, COL_BK, COL_BV = 1024, 1536, 1792
COL_CQ, COL_CK, COL_CV = 2048, 2560, 3072
COL_DQA, COL_DKVA, COL_DKR = 3584, 4096, 4608

TM = 1024
PROJ_ROW_TILES = (1536, 1024, 512)
TAB_ROPE128, TAB_ROPE64, TAB_IDENT = 0, 1, 2
N_TAB_PLANES = 6
TO = 512
TE = 256
TQ = 256
FFN_ACC_TN = 256
MIB = 1024 * 1024


def _cp(sem, vmem_mib):
    return pltpu.CompilerParams(dimension_semantics=sem, vmem_limit_bytes=vmem_mib * MIB)


def _dot(a, b):
    return jnp.dot(a, b, preferred_element_type=F32)


def _dot_nt(a, b):
    return lax.dot_general(a, b, (((1,), (1,)), ((), ())), preferred_element_type=F32)


def _ada_kernel(c_ref, w_ref, b_ref, o_ref):
    k = pl.program_id(1)
    c = c_ref[...]
    a = (c * (0.5 * jnp.tanh(0.5 * c) + 0.5)).astype(BF16)
    part = _dot(a, w_ref[...].astype(BF16))

    @pl.when(k == 0)
    def _():
        o_ref[...] = part + b_ref[...]

    @pl.when(k > 0)
    def _():
        o_ref[...] += part


def _ada_all(c_all, w_ada, b_ada):
    depth, d, n6 = w_ada.shape
    tk = 128
    return pl.pallas_call(
        _ada_kernel,
        grid=(depth, d // tk),
        in_specs=[pl.BlockSpec((8, tk), lambda l, k: (0, k)),
                  pl.BlockSpec((None, tk, n6), lambda l, k: (l, k, 0)),
                  pl.BlockSpec((None, 1, n6), lambda l, k: (l, 0, 0))],
        out_specs=pl.BlockSpec((None, 8, n6), lambda l, k: (l, 0, 0)),
        out_shape=jax.ShapeDtypeStruct((depth, 8, n6), F32),
        compiler_params=_cp(("arbitrary", "arbitrary"), 40),
        name="ada",
    )(c_all, w_ada, b_ada.reshape(depth, 1, n6))


def _standardise(z):
    mu = jnp.mean(z, axis=-1, keepdims=True)
    zc = z - mu
    var = jnp.mean(zc * zc, axis=-1, keepdims=True)
    return zc * lax.rsqrt(var + NORM_EPS)


def _init_kernel(x_ref, ctx_ref, sc_ref, sh_ref, xo_ref, ho_ref, *, n_lat):
    i = pl.program_id(0)

    def body(src):
        xs = _standardise(src[...])
        xo_ref[...] = xs
        ho_ref[...] = (xs * sc_ref[0] + sh_ref[0]).astype(BF16)

    @pl.when(i < n_lat)
    def _():
        body(x_ref)

    @pl.when(i >= n_lat)
    def _():
        body(ctx_ref)


def _init(x2, ctx2, sc_tbl, sh_tbl):
    n_lat_rows, d = x2.shape
    n_ctx_rows = ctx2.shape[0]
    n_lat = n_lat_rows // TE
    n_all = (n_lat_rows + n_ctx_rows) // TE
    tbl = pl.BlockSpec((1, 1, d), lambda i: ((i * TE) // TM, 0, 0))
    row = pl.BlockSpec((TE, d), lambda i: (i, 0))
    return pl.pallas_call(
        functools.partial(_init_kernel, n_lat=n_lat),
        grid=(n_all,),
        in_specs=[pl.BlockSpec((TE, d), lambda i: (jnp.minimum(i, n_lat - 1), 0)),
                  pl.BlockSpec((TE, d), lambda i: (jnp.maximum(i - n_lat, 0), 0)),
                  tbl, tbl],
        out_specs=[row, row],
        out_shape=[jax.ShapeDtypeStruct((n_lat_rows + n_ctx_rows, d), F32),
                   jax.ShapeDtypeStruct((n_lat_rows + n_ctx_rows, d), BF16)],
        compiler_params=_cp(("arbitrary",), 32),
        name="init_norm",
    )(x2, ctx2, sc_tbl, sh_tbl)


def _ln_kernel(x_ref, d_ref, gate_ref, lng_ref, lnb_ref, *rest, alpha, with_h):
    z = alpha * x_ref[...] + gate_ref[0] * d_ref[...].astype(F32)
    xn = _standardise(z) * lng_ref[...] + lnb_ref[...]
    if with_h:
        sc_ref, sh_ref, xo_ref, ho_ref = rest
        xo_ref[...] = xn
        ho_ref[...] = (xn * sc_ref[0] + sh_ref[0]).astype(BF16)
    else:
        (xo_ref,) = rest
        xo_ref[...] = xn


def _post_ln(x, delta, gate_tbl, ln_g, ln_b, sc_tbl, sh_tbl, *, n_rows, alpha, out_rows):
    d = x.shape[1]
    with_h = sc_tbl is not None
    tbl = pl.BlockSpec((1, 1, d), lambda i: ((i * TE) // TM, 0, 0))
    row = pl.BlockSpec((TE, d), lambda i: (i, 0))
    vec = pl.BlockSpec((1, d), lambda i: (0, 0))
    in_specs = [row, row, tbl, vec, vec]
    args = [x, delta, gate_tbl, ln_g.reshape(1, d), ln_b.reshape(1, d)]
    out_specs = [row]
    out_shape = [jax.ShapeDtypeStruct((out_rows, d), F32)]
    if with_h:
        in_specs += [tbl, tbl]
        args += [sc_tbl, sh_tbl]
        out_specs.append(row)
        out_shape.append(jax.ShapeDtypeStruct((out_rows, d), BF16))
    res = pl.pallas_call(
        functools.partial(_ln_kernel, alpha=alpha, with_h=with_h),
        grid=(n_rows // TE,),
        in_specs=in_specs, out_specs=out_specs, out_shape=out_shape,
        compiler_params=_cp(("arbitrary",), 32),
        name="post_ln",
    )(*args)
    return res if with_h else (res[0], None)


def _swap_select(x, near, far, bit):
    lane = lax.broadcasted_iota(jnp.int32, x.shape, 1)
    return jnp.where((lane & bit) == 0, pltpu.roll(x, far, 1), pltpu.roll(x, near, 1))


def _rope128(x, cos, sin):
    return x * cos + _swap_select(x, 32, LANE - 32, 32) * sin


def _rope64(x, cos, sin):
    return x * cos + _swap_select(x, 16, LANE - 16, 16) * sin


def _rms(x, g):
    return x * lax.rsqrt(jnp.mean(x * x, axis=-1, keepdims=True) + NORM_EPS) * g


def _proj_kernel(h_ref, wa_ref, wb_ref, tab_ref, aqk_ref, mqkv_ref, up_ref, g_ref):
    j = pl.program_id(1)
    piece = h_ref.shape[0] // PROJ_ROW_SPLIT
    blk = lambda col: col // PROJ_TN

    def block(pred, epilogue, width=PROJ_TN):
        @pl.when(pred)
        def _():
            w = wa_ref[:width, :].astype(BF16)
            for r in range(PROJ_ROW_SPLIT):
                rows = slice(r * piece, (r + 1) * piece)
                u = _dot_nt(h_ref[rows, :], w)
                cols = [u[:, c * LANE:(c + 1) * LANE] for c in range(width // LANE)]
                outs = epilogue(cols, rows)
                for c, val in enumerate(outs):
                    up_ref[rows, c * LANE:(c + 1) * LANE] = val.astype(BF16)

    def rope(x, kind, near, rows):
        lane = lax.broadcasted_iota(jnp.int32, x.shape, 1)
        swapped = jnp.where((lane & near) == 0, pltpu.roll(x, LANE - near, 1), pltpu.roll(x, near, 1))
        return x * tab_ref[2 * kind, rows, :] + swapped * tab_ref[2 * kind + 1, rows, :]

    def a_heads(cols, rows):
        is_q = j == blk(COL_AQ)
        gain = aqk_ref[pl.ds(j - blk(COL_AQ), 1), :]
        scale = jnp.where(is_q, HEAD_DIM ** -0.5, 1.0)
        outs = []
        for c, x in enumerate(cols):
            y = rope(_rms(x, gain), TAB_ROPE128, 32, rows) * scale
            outs.append(y if c < KV_HEADS else jnp.where(is_q, y, x))
        return outs

    block((j == blk(COL_AQ)) | (j == blk(COL_AK)), a_heads)

    def rope_heads(cols, rows):
        wide = j < blk(COL_CQ)
        near = jnp.where(wide, 32, 16)
        kind = jnp.where(wide, TAB_ROPE128, TAB_ROPE64)
        scale = jnp.where(j == blk(COL_BQ), HEAD_DIM ** -0.5,
                          jnp.where(j == blk(COL_CQ), C_QK_DIM ** -0.5, 1.0))
        all_plain = j == blk(COL_CV)
        tail_plain = all_plain | (j == blk(COL_BK))
        outs = []
        for c, x in enumerate(cols):
            k = jnp.where(all_plain if c < KV_HEADS else tail_plain, TAB_IDENT, kind)
            outs.append(rope(x, k, near, rows) * scale)
        return outs

    block((j >= blk(COL_BQ)) & (j <= blk(COL_CV)), rope_heads)

    def latent_norm(cols, rows):
        gain = mqkv_ref[pl.ds(j - blk(COL_DQA), 1), :]
        ms = sum(jnp.sum(x * x, axis=-1, keepdims=True) for x in cols) * (1.0 / (len(cols) * LANE))
        inv = lax.rsqrt(ms + NORM_EPS)
        return [x * inv * gain[:, c * LANE:(c + 1) * LANE] for c, x in enumerate(cols)]

    block((j == blk(COL_DQA)) | (j == blk(COL_DKVA)), latent_norm)

    def rope_key(cols, rows):
        x = cols[0]
        lane = lax.broadcasted_iota(jnp.int32, x.shape, 1)
        return [jnp.where(lane < MLA_ROPE, rope(x, TAB_ROPE64, 16, rows), 0.0)]

    block(j == blk(COL_DKR), rope_key, width=LANE)

    @pl.when(j == blk(COL_DKR))
    def _():
        up_ref[:, LANE:] = jnp.zeros((up_ref.shape[0], PROJ_TN - LANE), BF16)

    @pl.when(j >= N_UP_BLOCKS)
    def _():
        w = jnp.concatenate([wa_ref[GATE_ROW_OFF:, :], wb_ref[...]], axis=0).astype(BF16)
        for r in range(PROJ_ROW_SPLIT):
            rows = slice(r * piece, (r + 1) * piece)
            g = _dot_nt(h_ref[rows, :], w)
            g_ref[rows, :] = (0.5 * jnp.tanh(0.5 * g) + 0.5).astype(BF16)


def _row_tile(n, choices):
    return next(t for t in choices if n % t == 0)


def _proj(h, w_in_t, tabs, aqk, mq, mkv, *, layer):
    n, d = h.shape
    tp = _row_tile(n, PROJ_ROW_TILES)
    nt = n // tp
    n_gate = w_in_t.shape[1] - QKV_W
    gate_blocks = n_gate // PROJ_TN
    first_gate_blk = (QKV_W - GATE_ROW_OFF) // PROJ_TN
    assert (QKV_W - GATE_ROW_OFF) % PROJ_TN == 0 and n_gate % PROJ_TN == 0 and first_gate_blk == N_UP_BLOCKS - 1

    def wa_idx(i, j):
        return (layer, jnp.where(j < N_UP_BLOCKS, j, j - 1), 0)

    def wb_idx(i, j):
        g = jnp.maximum(j - N_UP_BLOCKS, 0)
        return (layer, (QKV_W - GATE_ROW_OFF + (g + 1) * PROJ_TN) // GATE_ROW_OFF, 0)

    return pl.pallas_call(
        _proj_kernel,
        grid=(nt, N_UP_BLOCKS + gate_blocks),
        in_specs=[pl.BlockSpec((tp, d), lambda i, j: (i, 0)),
                  pl.BlockSpec((None, PROJ_TN, d), wa_idx),
                  pl.BlockSpec((None, GATE_ROW_OFF, d), wb_idx),
                  pl.BlockSpec((N_TAB_PLANES, tp, LANE), lambda i, j: (0, i, 0)),
                  pl.BlockSpec((2, LANE), lambda i, j: (0, 0)),
                  pl.BlockSpec((2, MLA_RANK), lambda i, j: (0, 0))],
        out_specs=[pl.BlockSpec((tp, PROJ_TN), lambda i, j: (i, jnp.minimum(j, N_UP_BLOCKS - 1))),
                   pl.BlockSpec((tp, PROJ_TN), lambda i, j: (i, jnp.maximum(j - N_UP_BLOCKS, 0)))],
        out_shape=[jax.ShapeDtypeStruct((n, UP_W), BF16),
                   jax.ShapeDtypeStruct((n, n_gate), BF16)],
        compiler_params=_cp(("arbitrary", "arbitrary"), 58),
        name="proj",
    )(h, w_in_t, w_in_t, tabs, aqk, jnp.stack([mq, mkv]))


def _mla_up_kernel(qa_ref, kva_ref, kr_ref, wq_ref, wkv_ref, tab_ref, dq_ref, dk_ref, dv_ref):
    cos, sin = tab_ref[2 * TAB_ROPE64], tab_ref[2 * TAB_ROPE64 + 1]
    q = _dot(qa_ref[...], wq_ref[...].astype(BF16))
    kv = _dot(kva_ref[...], wkv_ref[...].astype(BF16))
    scale = (MLA_NOPE + MLA_ROPE) ** -0.5
    kr = kr_ref[...]
    for hh in range(N_HEADS):
        base = hh * MLA_QK_PAD
        dq_ref[:, base:base + LANE] = (q[:, base:base + LANE] * scale).astype(BF16)
        dq_ref[:, base + LANE:base + 2 * LANE] = (
            _rope64(q[:, base + LANE:base + 2 * LANE], cos, sin) * scale).astype(BF16)
        dk_ref[:, base:base + LANE] = kv[:, hh * LANE:(hh + 1) * LANE].astype(BF16)
        dk_ref[:, base + LANE:base + 2 * LANE] = kr
    dv_ref[...] = kv[:, N_HEADS * LANE:].astype(BF16)


def _mla_up(up, wq_pad, wkv_perm, tabs, *, layer):
    n = up.shape[0]
    nt = n // TM
    wide = N_HEADS * MLA_QK_PAD
    return pl.pallas_call(
        _mla_up_kernel,
        grid=(nt,),
        in_specs=[pl.BlockSpec((TM, MLA_RANK), lambda i: (i, COL_DQA // MLA_RANK)),
                  pl.BlockSpec((TM, MLA_RANK), lambda i: (i, COL_DKVA // MLA_RANK)),
                  pl.BlockSpec((TM, LANE), lambda i: (i, COL_DKR // LANE)),
                  pl.BlockSpec((None, MLA_RANK, wide), lambda i: (layer, 0, 0)),
                  pl.BlockSpec((None, MLA_RANK, wide), lambda i: (layer, 0, 0)),
                  pl.BlockSpec((N_TAB_PLANES, TM, LANE), lambda i: (0, i, 0))],
        out_specs=[pl.BlockSpec((TM, wide), lambda i: (i, 0)),
                   pl.BlockSpec((TM, wide), lambda i: (i, 0)),
                   pl.BlockSpec((TM, N_HEADS * LANE), lambda i: (i, 0))],
        out_shape=[jax.ShapeDtypeStruct((n, wide), BF16),
                   jax.ShapeDtypeStruct((n, wide), BF16),
                   jax.ShapeDtypeStruct((n, N_HEADS * LANE), BF16)],
        compiler_params=_cp(("arbitrary",), 48),
        name="mla_up",
    )(up, up, up, wq_pad, wkv_perm, tabs)


def _attend(q, pieces, extra=None):
    scores = []
    for k, _, mask in pieces:
        s = _dot_nt(q, k)
        if mask is not None:
            s = jnp.where(mask, s, NEG_INF)
        scores.append(s)
    m = functools.reduce(jnp.maximum, [jnp.max(s, axis=-1, keepdims=True) for s in scores])
    if extra is not None:
        m = jnp.maximum(m, extra)
    den = None
    out = None
    for s, (_, v, _) in zip(scores, pieces):
        e = jnp.exp(s - m)
        part = jnp.sum(e, axis=-1, keepdims=True)
        pv = _dot(e.astype(BF16), v)
        den = part if den is None else den + part
        out = pv if out is None else out + pv
    if extra is not None:
        den = den + jnp.exp(extra - m)
    return out / den


def _heads(ref, k, width=LANE):
    return ref[:, k * width:(k + 1) * width]


def _attn_a_kernel(q_ref, kl_ref, vl_ref, kc_ref, vc_ref, o_ref, *, nq):
    t = pl.program_id(1)

    def run(with_latent):
        for kvh in range(KV_HEADS):
            q2 = jnp.concatenate([_heads(q_ref, 2 * kvh), _heads(q_ref, 2 * kvh + 1)], axis=0)
            pieces = [(_heads(kc_ref, kvh), _heads(vc_ref, kvh), None)]
            if with_latent:
                pieces.append((_heads(kl_ref, kvh), _heads(vl_ref, kvh), None))
            o = _attend(q2, pieces)
            o_ref[:, (2 * kvh) * LANE:(2 * kvh + 1) * LANE] = o[:TQ].astype(BF16)
            o_ref[:, (2 * kvh + 1) * LANE:(2 * kvh + 2) * LANE] = o[TQ:].astype(BF16)

    @pl.when(t < nq)
    def _():
        run(True)

    @pl.when(t >= nq)
    def _():
        run(False)


def _attn_b_kernel(q_ref, kl_ref, vl_ref, kc_ref, vc_ref, sink_ref, o_ref, *, nq, seq, win_k):
    t = pl.program_id(1)

    def run(with_latent):
        if with_latent:
            q0 = t * TQ
            start = pl.multiple_of(jnp.clip(q0 - WINDOW, 0, seq - win_k), LANE)
            r = lax.broadcasted_iota(jnp.int32, (2 * TQ, win_k), 0)
            c = lax.broadcasted_iota(jnp.int32, (2 * TQ, win_k), 1)
            qpos = q0 + jnp.where(r >= TQ, r - TQ, r)
            mask = jnp.abs(qpos - (start + c)) <= WINDOW
        for kvh in range(KV_HEADS):
            q2 = jnp.concatenate([_heads(q_ref, 2 * kvh), _heads(q_ref, 2 * kvh + 1)], axis=0)
            r1 = lax.broadcasted_iota(jnp.int32, (2 * TQ, 1), 0)
            sink = jnp.where(r1 >= TQ, sink_ref[0:1, 2 * kvh + 1:2 * kvh + 2],
                             sink_ref[0:1, 2 * kvh:2 * kvh + 1])
            pieces = [(_heads(kc_ref, kvh), _heads(vc_ref, kvh), None)]
            if with_latent:
                kw = kl_ref[pl.ds(start, win_k), kvh * LANE:(kvh + 1) * LANE]
                vw = vl_ref[pl.ds(start, win_k), kvh * LANE:(kvh + 1) * LANE]
                pieces.append((kw, vw, mask))
            o = _attend(q2, pieces, extra=sink)
            o_ref[:, (2 * kvh) * LANE:(2 * kvh + 1) * LANE] = o[:TQ].astype(BF16)
            o_ref[:, (2 * kvh + 1) * LANE:(2 * kvh + 2) * LANE] = o[TQ:].astype(BF16)

    @pl.when(t < nq)
    def _():
        run(True)

    @pl.when(t >= nq)
    def _():
        run(False)


def _attn_c_kernel(q_ref, kl_ref, vl_ref, kc_ref, vc_ref, lam_ref, g_ref, o_ref, *, nq, lam_init):
    t = pl.program_id(1)
    lam = (jnp.exp(jnp.sum(lam_ref[0:1, :] * lam_ref[1:2, :], axis=-1, keepdims=True))
           - jnp.exp(jnp.sum(lam_ref[2:3, :] * lam_ref[3:4, :], axis=-1, keepdims=True)) + lam_init)

    def run(with_latent):
        lane = lax.broadcasted_iota(jnp.int32, (TQ, LANE), 1)
        for hh in range(N_HEADS):
            q = _heads(q_ref, hh)
            zero = jnp.zeros_like(q)
            q2 = jnp.concatenate([jnp.where(lane < C_QK_DIM, q, zero),
                                  jnp.where(lane >= C_QK_DIM, q, zero)], axis=0)
            pieces = [(_heads(kc_ref, hh), _heads(vc_ref, hh), None)]
            if with_latent:
                pieces.append((_heads(kl_ref, hh), _heads(vl_ref, hh), None))
            o = _attend(q2, pieces)
            o = o[:TQ] - lam * o[TQ:]
            o = _rms(o, g_ref[...]) * (1.0 - lam_init)
            o_ref[:, hh * LANE:(hh + 1) * LANE] = o.astype(BF16)

    @pl.when(t < nq)
    def _():
        run(True)

    @pl.when(t >= nq)
    def _():
        run(False)


def _attn_d_kernel(q_ref, kl_ref, vl_ref, kc_ref, vc_ref, o_ref, *, nq):
    t = pl.program_id(1)

    def run(with_latent):
        for hh in range(N_HEADS):
            q = _heads(q_ref, hh, MLA_QK_PAD)
            pieces = [(_heads(kc_ref, hh, MLA_QK_PAD), _heads(vc_ref, hh), None)]
            if with_latent:
                pieces.append((_heads(kl_ref, hh, MLA_QK_PAD), _heads(vl_ref, hh), None))
            o_ref[:, hh * LANE:(hh + 1) * LANE] = _attend(q, pieces).astype(BF16)

    @pl.when(t < nq)
    def _():
        run(True)

    @pl.when(t >= nq)
    def _():
        run(False)


def _attention(kern, name, q_src, k_src, v_src, extras, *, batch, seq, ctx, with_ctx_queries,
               q_w, q_blk, k_w, k_blk, v_w, v_blk):
    assert ctx == TQ
    n = batch * (seq + ctx) if with_ctx_queries else batch * seq
    nq = seq // TQ
    steps = nq + (1 if with_ctx_queries else 0)
    lat_blocks = batch * seq // TQ

    def qrow(b, t):
        return jnp.where(t < nq, b * nq + t, lat_blocks + b)

    in_specs = [pl.BlockSpec((TQ, q_w), lambda b, t: (qrow(b, t), q_blk)),
                pl.BlockSpec((seq, k_w), lambda b, t: (b, k_blk)),
                pl.BlockSpec((seq, v_w), lambda b, t: (b, v_blk)),
                pl.BlockSpec((ctx, k_w), lambda b, t: (batch * seq // ctx + b, k_blk)),
                pl.BlockSpec((ctx, v_w), lambda b, t: (batch * seq // ctx + b, v_blk))]
    args = [q_src, k_src, v_src, k_src, v_src]
    for e in extras:
        in_specs.append(pl.BlockSpec(e.shape, lambda b, t: (0, 0)))
        args.append(e)
    return pl.pallas_call(
        functools.partial(kern, nq=nq),
        grid=(batch, steps),
        in_specs=in_specs,
        out_specs=pl.BlockSpec((TQ, BRANCH_W), lambda b, t: (qrow(b, t), 0)),
        out_shape=jax.ShapeDtypeStruct((n, BRANCH_W), BF16),
        compiler_params=_cp(("arbitrary", "arbitrary"), 48),
        name=name,
    )(*args)


def _merge_kernel(oa_ref, ob_ref, oc_ref, od_ref, ga_ref, gb_ref, gc_ref, gd_ref, w_ref, y_ref):
    acc = None
    for m, (o_ref, g_ref) in enumerate(((oa_ref, ga_ref), (ob_ref, gb_ref), (oc_ref, gc_ref), (od_ref, gd_ref))):
        p = g_ref[...].astype(F32) * _dot(o_ref[...], w_ref[m].astype(BF16))
        acc = p if acc is None else acc + p
    y_ref[...] = acc.astype(BF16)


def _merge(outs, gates, w_branch, *, layer, n_tiles):
    n = n_tiles * TM
    d = w_branch.shape[3]
    tn = 512
    nb = d // tn
    o_spec = pl.BlockSpec((TM, BRANCH_W), lambda i, j: (i, 0))
    g_specs = [pl.BlockSpec((TM, tn), functools.partial(lambda i, j, m: (i, m * nb + j), m=m))
               for m in range(N_BRANCH)]
    return pl.pallas_call(
        _merge_kernel,
        grid=(n_tiles, nb),
        in_specs=[o_spec] * N_BRANCH + g_specs + [pl.BlockSpec((None, N_BRANCH, BRANCH_W, tn), lambda i, j: (layer, 0, 0, j))],
        out_specs=pl.BlockSpec((TM, tn), lambda i, j: (i, j)),
        out_shape=jax.ShapeDtypeStruct((n, d), BF16),
        compiler_params=_cp(("arbitrary", "arbitrary"), 48),
        name="merge",
    )(*outs, gates, gates, gates, gates, w_branch)


def _out_ln_kernel(y_ref, w_ref, x_ref, gate_ref, lng_ref, lnb_ref, sc_ref, sh_ref, xo_ref, ho_ref, *, alpha):
    piece = y_ref.shape[0] // 2
    for r in range(2):
        rows = slice(r * piece, (r + 1) * piece)
        mix = _dot(y_ref[rows, :], w_ref[...])
        z = alpha * x_ref[rows, :] + gate_ref[0] * mix
        xn = _standardise(z) * lng_ref[...] + lnb_ref[...]
        xo_ref[rows, :] = xn
        ho_ref[rows, :] = (xn * sc_ref[0] + sh_ref[0]).astype(BF16)


def _out_ln(y, w_out_bf16, x, gate_tbl, ln_g, ln_b, sc_tbl, sh_tbl, *, layer, n_rows, alpha):
    d = x.shape[1]
    tbl = pl.BlockSpec((1, 1, d), lambda i: ((i * TO) // TM, 0, 0))
    row = pl.BlockSpec((TO, d), lambda i: (i, 0))
    vec = pl.BlockSpec((1, d), lambda i: (0, 0))
    return pl.pallas_call(
        functools.partial(_out_ln_kernel, alpha=alpha),
        grid=(n_rows // TO,),
        in_specs=[row, pl.BlockSpec((None, d, d), lambda i: (layer, 0, 0)), row, tbl, vec, vec, tbl, tbl],
        out_specs=[row, row],
        out_shape=[jax.ShapeDtypeStruct((n_rows, d), F32), jax.ShapeDtypeStruct((n_rows, d), BF16)],
        compiler_params=_cp(("arbitrary",), 56),
        name="out_ln",
    )(y, w_out_bf16, x, gate_tbl, ln_g.reshape(1, d), ln_b.reshape(1, d), sc_tbl, sh_tbl)


def _ffn_kernel(h_ref, wg_ref, wu_ref, wo_ref, o_ref, acc_ref):
    f = pl.program_id(1)

    @pl.when(f == 0)
    def _():
        acc_ref[...] = jnp.zeros_like(acc_ref)

    h = h_ref[...]
    g = _dot(h, wg_ref[...].astype(BF16))
    u = _dot(h, wu_ref[...].astype(BF16))
    a = (g * (0.5 * jnp.tanh(0.5 * g) + 0.5) * u).astype(BF16)

    for n in range(acc_ref.shape[1] // FFN_ACC_TN):
        cols = slice(n * FFN_ACC_TN, (n + 1) * FFN_ACC_TN)
        acc_ref[:, cols] += _dot(a, wo_ref[:, cols].astype(BF16))

    @pl.when(f == pl.num_programs(1) - 1)
    def _():
        o_ref[...] = acc_ref[...].astype(BF16)


def _ffn(h, w_ffn_in, w_ffn_out, *, layer, n_tiles):
    d = h.shape[1]
    n = n_tiles * TM
    hidden = w_ffn_out.shape[1]
    tf = 512
    nf = hidden // tf
    return pl.pallas_call(
        _ffn_kernel,
        grid=(n_tiles, nf),
        in_specs=[pl.BlockSpec((TM, d), lambda i, f: (i, 0)),
                  pl.BlockSpec((None, d, tf), lambda i, f: (layer, 0, f)),
                  pl.BlockSpec((None, d, tf), lambda i, f: (layer, 0, nf + f)),
                  pl.BlockSpec((None, tf, d), lambda i, f: (layer, f, 0))],
        out_specs=pl.BlockSpec((TM, d), lambda i, f: (i, 0)),
        out_shape=jax.ShapeDtypeStruct((n, d), BF16),
        scratch_shapes=[pltpu.VMEM((TM, d), F32)],
        compiler_params=_cp(("arbitrary", "arbitrary"), 60),
        name="ffn",
    )(h, w_ffn_in, w_ffn_in, w_ffn_out)


def _rope_tables(seq, batch, ctx_rows):
    t = jnp.arange(seq, dtype=jnp.int32)
    pos_row = (t // GRID_W).astype(F32)[:, None]
    pos_col = (t % GRID_W).astype(F32)[:, None]

    def one(rot_dim, reps):
        axis_dim = rot_dim // 2
        freqs = ROPE_THETA ** (-jnp.arange(0, axis_dim, 2, dtype=F32) / axis_dim)
        ar, ac = pos_row * freqs[None, :], pos_col * freqs[None, :]
        cos = jnp.concatenate([jnp.cos(ar), jnp.cos(ar), jnp.cos(ac), jnp.cos(ac)], axis=-1)
        sin = jnp.concatenate([-jnp.sin(ar), jnp.sin(ar), -jnp.sin(ac), jnp.sin(ac)], axis=-1)
        return jnp.tile(cos, (1, reps)), jnp.tile(sin, (1, reps))

    c128, s128 = one(HEAD_DIM, 1)
    c64, s64 = one(C_QK_DIM, 2)
    ones = jnp.ones((ctx_rows, LANE), F32)
    zeros = jnp.zeros((ctx_rows, LANE), F32)

    def rows(tab, pad):
        return jnp.concatenate([jnp.tile(tab, (batch, 1)), pad])

    n = batch * seq + ctx_rows
    return jnp.stack([rows(c128, ones), rows(s128, zeros), rows(c64, ones), rows(s64, zeros),
                      jnp.ones((n, LANE), F32), jnp.zeros((n, LANE), F32)])


def kernel(x, c, ctx, c_ctx, w_ada, b_ada, w_in, a_qk_norm, b_sink, c_lambda, c_subln, mla_q_norm, mla_w_q_up, mla_kv_norm, mla_w_kv_up, w_branch, w_out, ln_mix_g, ln_mix_b, w_ffn_in, w_ffn_out, ln_ffn_g, ln_ffn_b):
    batch, seq, d = x.shape
    ctx_len = ctx.shape[1]
    depth = w_ada.shape[0]
    assert d == D_MODEL and seq % TM == 0 and (batch * ctx_len) % TM == 0 and batch + 1 <= 8
    n_lat_rows = batch * seq
    n_rows = n_lat_rows + batch * ctx_len
    tiles_per_seq = seq // TM
    n_lat_tiles = n_lat_rows // TM
    n_tiles = n_rows // TM
    alpha = (2 * depth) ** 0.25

    c_all = jnp.concatenate([c, c_ctx[None, :], jnp.zeros((8 - batch - 1, d), F32)], axis=0)
    ada = _ada_all(c_all, w_ada, b_ada)
    tile_row = np.array([b for b in range(batch) for _ in range(tiles_per_seq)]
                        + [batch] * (n_tiles - n_lat_tiles), np.int32)

    def table(l, chunk, plus_one=False):
        v = ada[l, :, chunk * d:(chunk + 1) * d][tile_row]
        return ((1.0 + v) if plus_one else v)[:, None, :]

    tabs = _rope_tables(seq, batch, batch * ctx_len)
    x_all, h = _init(x.reshape(n_lat_rows, d), ctx.reshape(batch * ctx_len, d),
                     table(0, 1, True), table(0, 0))

    win_k = min(TQ + 2 * WINDOW, seq)
    sink_pad = jnp.zeros((depth, 8, LANE), F32).at[:, 0, :N_HEADS].set(b_sink)
    wq_pad = jnp.pad(mla_w_q_up.reshape(depth, MLA_RANK, N_HEADS, MLA_NOPE + MLA_ROPE),
                     ((0, 0), (0, 0), (0, 0), (0, MLA_QK_PAD - MLA_NOPE - MLA_ROPE))
                     ).reshape(depth, MLA_RANK, N_HEADS * MLA_QK_PAD)
    wkv_pad = jnp.pad(mla_w_kv_up.reshape(depth, MLA_RANK, N_HEADS, 2, LANE).transpose(0, 1, 3, 2, 4)
                      .reshape(depth, MLA_RANK, 2 * N_HEADS * LANE),
                      ((0, 0), (0, 0), (0, N_HEADS * MLA_QK_PAD - 2 * N_HEADS * LANE)))

    w_in_t = jnp.swapaxes(w_in, 1, 2)
    w_out_bf16 = w_out.astype(BF16)

    for l in range(depth):
        last = l == depth - 1
        act_tiles = n_lat_tiles if last else n_tiles
        act_rows = act_tiles * TM
        lam_init = 0.8 - 0.6 * math.exp(-0.3 * l)

        up, gates = _proj(h, w_in_t, tabs, a_qk_norm[l], mla_q_norm[l], mla_kv_norm[l], layer=l)
        dq, dk, dv = _mla_up(up, wq_pad, wkv_pad, tabs, layer=l)

        common = dict(batch=batch, seq=seq, ctx=ctx_len, with_ctx_queries=not last)
        o_a = _attention(_attn_a_kernel, "attn_a", up, up, up, [], **common,
                         q_w=512, q_blk=COL_AQ // 512, k_w=256, k_blk=COL_AK // 256, v_w=256, v_blk=COL_AV // 256)
        o_b = _attention(functools.partial(_attn_b_kernel, seq=seq, win_k=win_k), "attn_b", up, up, up,
                         [sink_pad[l]], **common,
                         q_w=512, q_blk=COL_BQ // 512, k_w=256, k_blk=COL_BK // 256, v_w=256, v_blk=COL_BV // 256)
        o_c = _attention(functools.partial(_attn_c_kernel, lam_init=lam_init), "attn_c", up, up, up,
                         [c_lambda[l], c_subln[l].reshape(1, LANE)], **common,
                         q_w=512, q_blk=COL_CQ // 512, k_w=512, k_blk=COL_CK // 512, v_w=512, v_blk=COL_CV // 512)
        o_d = _attention(_attn_d_kernel, "attn_d", dq, dk, dv, [], **common,
                         q_w=N_HEADS * MLA_QK_PAD, q_blk=0, k_w=N_HEADS * MLA_QK_PAD, k_blk=0,
                         v_w=N_HEADS * LANE, v_blk=0)

        y = _merge([o_a, o_b, o_c, o_d], gates, w_branch, layer=l, n_tiles=act_tiles)
        x_all, h2 = _out_ln(y, w_out_bf16, x_all, table(l, 2), ln_mix_g[l], ln_mix_b[l],
                            table(l, 4, True), table(l, 3), layer=l, n_rows=act_rows, alpha=alpha)
        ffn = _ffn(h2, w_ffn_in, w_ffn_out, layer=l, n_tiles=act_tiles)
        if last:
            x_all, _ = _post_ln(x_all, ffn, table(l, 5), ln_ffn_g[l], ln_ffn_b[l], None, None,
                                n_rows=act_rows, alpha=alpha, out_rows=n_lat_rows)
        else:
            x_all, h = _post_ln(x_all, ffn, table(l, 5), ln_ffn_g[l], ln_ffn_b[l],
                                table(l + 1, 1, True), table(l + 1, 0),
                                n_rows=act_rows, alpha=alpha, out_rows=n_rows)
    return x_all.reshape(batch, seq, d)
```

```python
import functools
import math

import numpy as np
import jax
import jax.numpy as jnp
from jax import lax
from jax.experimental import pallas as pl
from jax.experimental.pallas import tpu as pltpu

F32 = jnp.float32
BF16 = jnp.bfloat16

D_MODEL = 2048
GRID_W = 64
HEAD_DIM = 128
WINDOW = 128
ROPE_THETA = 10000.0
NORM_EPS = 1e-6
NEG_INF = -1e30
LOG2E = math.log2(math.e)
N_HEADS = 4
KV_HEADS = 2
C_QK_DIM = 64
MLA_RANK = 512
MLA_NOPE = 128
MLA_ROPE = 64
MLA_QK_PAD = 256
BRANCH_W = 512
N_BRANCH = 4
FFN_HIDDEN = 5632
QKV_W = 4672
LANE = 128
GATE_ROW_OFF = QKV_W % LANE

UP_W = 5120
PROJ_TN = 512
N_UP_BLOCKS = UP_W // PROJ_TN
PROJ_ROW_SPLIT = 4
COL_AQ, COL_AK, COL_AV = 0, 512, 768
COL_BQ, COL_BK, COL_BV = 1024, 1536, 1792
COL_CQ, COL_CK, COL_CV = 2048, 2560, 3072
COL_DQA, COL_DKVA, COL_DKR = 3584, 4096, 4608

TM = 1024
PROJ_ROW_TILES = (1536, 1024, 512)
TAB_ROPE128, TAB_ROPE64, TAB_IDENT = 0, 1, 2
N_TAB_PLANES = 6
TO = 512
TE = 256
TQ = 512
ATTN_D_ROW_SPLIT = 2
FFN_ACC_TN = 256
MIB = 1024 * 1024


def _cp(sem, vmem_mib):
    return pltpu.CompilerParams(dimension_semantics=sem, vmem_limit_bytes=vmem_mib * MIB)


def _dot(a, b):
    return jnp.dot(a, b, preferred_element_type=F32)


def _dot_nt(a, b):
    return lax.dot_general(a, b, (((1,), (1,)), ((), ())), preferred_element_type=F32)


def _ada_kernel(c_ref, w_ref, b_ref, o_ref):
    k = pl.program_id(1)
    c = c_ref[...]
    a = (c * (0.5 * jnp.tanh(0.5 * c) + 0.5)).astype(BF16)
    part = _dot(a, w_ref[...].astype(BF16))

    @pl.when(k == 0)
    def _():
        o_ref[...] = part + b_ref[...]

    @pl.when(k > 0)
    def _():
        o_ref[...] += part


def _ada_all(c_all, w_ada, b_ada):
    depth, d, n6 = w_ada.shape
    tk = 128
    return pl.pallas_call(
        _ada_kernel,
        grid=(depth, d // tk),
        in_specs=[pl.BlockSpec((8, tk), lambda l, k: (0, k)),
                  pl.BlockSpec((None, tk, n6), lambda l, k: (l, k, 0)),
                  pl.BlockSpec((None, 1, n6), lambda l, k: (l, 0, 0))],
        out_specs=pl.BlockSpec((None, 8, n6), lambda l, k: (l, 0, 0)),
        out_shape=jax.ShapeDtypeStruct((depth, 8, n6), F32),
        compiler_params=_cp(("arbitrary", "arbitrary"), 40),
        name="ada",
    )(c_all, w_ada, b_ada.reshape(depth, 1, n6))


def _standardise(z):
    mu = jnp.mean(z, axis=-1, keepdims=True)
    zc = z - mu
    var = jnp.mean(zc * zc, axis=-1, keepdims=True)
    return zc * lax.rsqrt(var + NORM_EPS)


def _init_kernel(x_ref, ctx_ref, sc_ref, sh_ref, xo_ref, ho_ref, *, n_lat):
    i = pl.program_id(0)

    def body(src):
        xs = _standardise(src[...])
        xo_ref[...] = xs
        ho_ref[...] = (xs * sc_ref[0] + sh_ref[0]).astype(BF16)

    @pl.when(i < n_lat)
    def _():
        body(x_ref)

    @pl.when(i >= n_lat)
    def _():
        body(ctx_ref)


def _init(x2, ctx2, sc_tbl, sh_tbl):
    n_lat_rows, d = x2.shape
    n_ctx_rows = ctx2.shape[0]
    n_lat = n_lat_rows // TE
    n_all = (n_lat_rows + n_ctx_rows) // TE
    tbl = pl.BlockSpec((1, 1, d), lambda i: ((i * TE) // TM, 0, 0))
    row = pl.BlockSpec((TE, d), lambda i: (i, 0))
    return pl.pallas_call(
        functools.partial(_init_kernel, n_lat=n_lat),
        grid=(n_all,),
        in_specs=[pl.BlockSpec((TE, d), lambda i: (jnp.minimum(i, n_lat - 1), 0)),
                  pl.BlockSpec((TE, d), lambda i: (jnp.maximum(i - n_lat, 0), 0)),
                  tbl, tbl],
        out_specs=[row, row],
        out_shape=[jax.ShapeDtypeStruct((n_lat_rows + n_ctx_rows, d), F32),
                   jax.ShapeDtypeStruct((n_lat_rows + n_ctx_rows, d), BF16)],
        compiler_params=_cp(("arbitrary",), 32),
        name="init_norm",
    )(x2, ctx2, sc_tbl, sh_tbl)


def _ln_kernel(x_ref, d_ref, gate_ref, lng_ref, lnb_ref, *rest, alpha, with_h):
    z = alpha * x_ref[...] + gate_ref[0] * d_ref[...].astype(F32)
    xn = _standardise(z) * lng_ref[...] + lnb_ref[...]
    if with_h:
        sc_ref, sh_ref, xo_ref, ho_ref = rest
        xo_ref[...] = xn
        ho_ref[...] = (xn * sc_ref[0] + sh_ref[0]).astype(BF16)
    else:
        (xo_ref,) = rest
        xo_ref[...] = xn


def _post_ln(x, delta, gate_tbl, ln_g, ln_b, sc_tbl, sh_tbl, *, n_rows, alpha, out_rows):
    d = x.shape[1]
    with_h = sc_tbl is not None
    tbl = pl.BlockSpec((1, 1, d), lambda i: ((i * TE) // TM, 0, 0))
    row = pl.BlockSpec((TE, d), lambda i: (i, 0))
    vec = pl.BlockSpec((1, d), lambda i: (0, 0))
    in_specs = [row, row, tbl, vec, vec]
    args = [x, delta, gate_tbl, ln_g.reshape(1, d), ln_b.reshape(1, d)]
    out_specs = [row]
    out_shape = [jax.ShapeDtypeStruct((out_rows, d), F32)]
    if with_h:
        in_specs += [tbl, tbl]
        args += [sc_tbl, sh_tbl]
        out_specs.append(row)
        out_shape.append(jax.ShapeDtypeStruct((out_rows, d), BF16))
    res = pl.pallas_call(
        functools.partial(_ln_kernel, alpha=alpha, with_h=with_h),
        grid=(n_rows // TE,),
        in_specs=in_specs, out_specs=out_specs, out_shape=out_shape,
        compiler_params=_cp(("arbitrary",), 32),
        name="post_ln",
    )(*args)
    return res if with_h else (res[0], None)


def _swap_select(x, near, far, bit):
    lane = lax.broadcasted_iota(jnp.int32, x.shape, 1)
    return jnp.where((lane & bit) == 0, pltpu.roll(x, far, 1), pltpu.roll(x, near, 1))


def _rope128(x, cos, sin):
    return x * cos + _swap_select(x, 32, LANE - 32, 32) * sin


def _rope64(x, cos, sin):
    return x * cos + _swap_select(x, 16, LANE - 16, 16) * sin


def _rms(x, g):
    return x * lax.rsqrt(jnp.mean(x * x, axis=-1, keepdims=True) + NORM_EPS) * g


def _proj_kernel(h_ref, wa_ref, wb_ref, tab_ref, aqk_ref, mqkv_ref, up_ref, g_ref):
    j = pl.program_id(1)
    piece = h_ref.shape[0] // PROJ_ROW_SPLIT
    blk = lambda col: col // PROJ_TN

    def block(pred, epilogue, width=PROJ_TN):
        @pl.when(pred)
        def _():
            w = wa_ref[:width, :].astype(BF16)
            for r in range(PROJ_ROW_SPLIT):
                rows = slice(r * piece, (r + 1) * piece)
                u = _dot_nt(h_ref[rows, :], w)
                cols = [u[:, c * LANE:(c + 1) * LANE] for c in range(width // LANE)]
                outs = epilogue(cols, rows)
                for c, val in enumerate(outs):
                    up_ref[rows, c * LANE:(c + 1) * LANE] = val.astype(BF16)

    def rope(x, kind, near, rows):
        lane = lax.broadcasted_iota(jnp.int32, x.shape, 1)
        swapped = jnp.where((lane & near) == 0, pltpu.roll(x, LANE - near, 1), pltpu.roll(x, near, 1))
        return x * tab_ref[2 * kind, rows, :] + swapped * tab_ref[2 * kind + 1, rows, :]

    def a_heads(cols, rows):
        is_q = j == blk(COL_AQ)
        gain = aqk_ref[pl.ds(j - blk(COL_AQ), 1), :]
        scale = jnp.where(is_q, LOG2E * HEAD_DIM ** -0.5, 1.0)
        outs = []
        for c, x in enumerate(cols):
            y = rope(_rms(x, gain), TAB_ROPE128, 32, rows) * scale
            outs.append(y if c < KV_HEADS else jnp.where(is_q, y, x))
        return outs

    block((j == blk(COL_AQ)) | (j == blk(COL_AK)), a_heads)

    def rope_heads(cols, rows):
        wide = j < blk(COL_CQ)
        near = jnp.where(wide, 32, 16)
        kind = jnp.where(wide, TAB_ROPE128, TAB_ROPE64)
        scale = jnp.where(j == blk(COL_BQ), LOG2E * HEAD_DIM ** -0.5,
                          jnp.where(j == blk(COL_CQ), LOG2E * C_QK_DIM ** -0.5, 1.0))
        all_plain = j == blk(COL_CV)
        tail_plain = all_plain | (j == blk(COL_BK))
        outs = []
        for c, x in enumerate(cols):
            k = jnp.where(all_plain if c < KV_HEADS else tail_plain, TAB_IDENT, kind)
            outs.append(rope(x, k, near, rows) * scale)
        return outs

    block((j >= blk(COL_BQ)) & (j <= blk(COL_CV)), rope_heads)

    def latent_norm(cols, rows):
        gain = mqkv_ref[pl.ds(j - blk(COL_DQA), 1), :]
        ms = sum(jnp.sum(x * x, axis=-1, keepdims=True) for x in cols) * (1.0 / (len(cols) * LANE))
        inv = lax.rsqrt(ms + NORM_EPS)
        return [x * inv * gain[:, c * LANE:(c + 1) * LANE] for c, x in enumerate(cols)]

    block((j == blk(COL_DQA)) | (j == blk(COL_DKVA)), latent_norm)

    def rope_key(cols, rows):
        x = cols[0]
        lane = lax.broadcasted_iota(jnp.int32, x.shape, 1)
        return [jnp.where(lane < MLA_ROPE, rope(x, TAB_ROPE64, 16, rows), 0.0)]

    block(j == blk(COL_DKR), rope_key, width=LANE)

    @pl.when(j == blk(COL_DKR))
    def _():
        up_ref[:, LANE:] = jnp.zeros((up_ref.shape[0], PROJ_TN - LANE), BF16)

    @pl.when(j >= N_UP_BLOCKS)
    def _():
        w = jnp.concatenate([wa_ref[GATE_ROW_OFF:, :], wb_ref[...]], axis=0).astype(BF16)
        for r in range(PROJ_ROW_SPLIT):
            rows = slice(r * piece, (r + 1) * piece)
            g = _dot_nt(h_ref[rows, :], w)
            g_ref[rows, :] = (0.5 * jnp.tanh(0.5 * g) + 0.5).astype(BF16)


def _row_tile(n, choices):
    return next(t for t in choices if n % t == 0)


def _proj(h, w_in_t, tabs, aqk, mq, mkv, *, layer):
    n, d = h.shape
    tp = _row_tile(n, PROJ_ROW_TILES)
    nt = n // tp
    n_gate = w_in_t.shape[1] - QKV_W
    gate_blocks = n_gate // PROJ_TN
    first_gate_blk = (QKV_W - GATE_ROW_OFF) // PROJ_TN
    assert (QKV_W - GATE_ROW_OFF) % PROJ_TN == 0 and n_gate % PROJ_TN == 0 and first_gate_blk == N_UP_BLOCKS - 1

    def wa_idx(i, j):
        return (layer, jnp.where(j < N_UP_BLOCKS, j, j - 1), 0)

    def wb_idx(i, j):
        g = jnp.maximum(j - N_UP_BLOCKS, 0)
        return (layer, (QKV_W - GATE_ROW_OFF + (g + 1) * PROJ_TN) // GATE_ROW_OFF, 0)

    return pl.pallas_call(
        _proj_kernel,
        grid=(nt, N_UP_BLOCKS + gate_blocks),
        in_specs=[pl.BlockSpec((tp, d), lambda i, j: (i, 0)),
                  pl.BlockSpec((None, PROJ_TN, d), wa_idx),
                  pl.BlockSpec((None, GATE_ROW_OFF, d), wb_idx),
                  pl.BlockSpec((N_TAB_PLANES, tp, LANE), lambda i, j: (0, i, 0)),
                  pl.BlockSpec((2, LANE), lambda i, j: (0, 0)),
                  pl.BlockSpec((2, MLA_RANK), lambda i, j: (0, 0))],
        out_specs=[pl.BlockSpec((tp, PROJ_TN), lambda i, j: (i, jnp.minimum(j, N_UP_BLOCKS - 1))),
                   pl.BlockSpec((tp, PROJ_TN), lambda i, j: (i, jnp.maximum(j - N_UP_BLOCKS, 0)))],
        out_shape=[jax.ShapeDtypeStruct((n, UP_W), BF16),
                   jax.ShapeDtypeStruct((n, n_gate), BF16)],
        compiler_params=_cp(("arbitrary", "arbitrary"), 58),
        name="proj",
    )(h, w_in_t, w_in_t, tabs, aqk, jnp.stack([mq, mkv]))


def _mla_up_kernel(qa_ref, kva_ref, kr_ref, wq_ref, wkv_ref, tab_ref, dq_ref, dk_ref, dv_ref):
    cos, sin = tab_ref[2 * TAB_ROPE64], tab_ref[2 * TAB_ROPE64 + 1]
    q = _dot(qa_ref[...], wq_ref[...].astype(BF16))
    kv = _dot(kva_ref[...], wkv_ref[...].astype(BF16))
    scale = LOG2E * (MLA_NOPE + MLA_ROPE) ** -0.5
    kr = kr_ref[...]
    for hh in range(N_HEADS):
        base = hh * MLA_QK_PAD
        dq_ref[:, base:base + LANE] = (q[:, base:base + LANE] * scale).astype(BF16)
        dq_ref[:, base + LANE:base + 2 * LANE] = (
            _rope64(q[:, base + LANE:base + 2 * LANE], cos, sin) * scale).astype(BF16)
        dk_ref[:, base:base + LANE] = kv[:, hh * LANE:(hh + 1) * LANE].astype(BF16)
        dk_ref[:, base + LANE:base + 2 * LANE] = kr
    dv_ref[...] = kv[:, N_HEADS * LANE:].astype(BF16)


def _mla_up(up, wq_pad, wkv_perm, tabs, *, layer):
    n = up.shape[0]
    nt = n // TM
    wide = N_HEADS * MLA_QK_PAD
    return pl.pallas_call(
        _mla_up_kernel,
        grid=(nt,),
        in_specs=[pl.BlockSpec((TM, MLA_RANK), lambda i: (i, COL_DQA // MLA_RANK)),
                  pl.BlockSpec((TM, MLA_RANK), lambda i: (i, COL_DKVA // MLA_RANK)),
                  pl.BlockSpec((TM, LANE), lambda i: (i, COL_DKR // LANE)),
                  pl.BlockSpec((None, MLA_RANK, wide), lambda i: (layer, 0, 0)),
                  pl.BlockSpec((None, MLA_RANK, wide), lambda i: (layer, 0, 0)),
                  pl.BlockSpec((N_TAB_PLANES, TM, LANE), lambda i: (0, i, 0))],
        out_specs=[pl.BlockSpec((TM, wide), lambda i: (i, 0)),
                   pl.BlockSpec((TM, wide), lambda i: (i, 0)),
                   pl.BlockSpec((TM, N_HEADS * LANE), lambda i: (i, 0))],
        out_shape=[jax.ShapeDtypeStruct((n, wide), BF16),
                   jax.ShapeDtypeStruct((n, wide), BF16),
                   jax.ShapeDtypeStruct((n, N_HEADS * LANE), BF16)],
        compiler_params=_cp(("arbitrary",), 48),
        name="mla_up",
    )(up, up, up, wq_pad, wkv_perm, tabs)


def _attend(q, pieces, extra=None):
    scores = []
    for k, _, mask in pieces:
        s = _dot_nt(q, k)
        if mask is not None:
            s = jnp.where(mask, s, NEG_INF)
        scores.append(s)
    m = functools.reduce(jnp.maximum, [jnp.max(s, axis=-1, keepdims=True) for s in scores])
    if extra is not None:
        m = jnp.maximum(m, extra)
    den = None
    out = None
    for s, (_, v, _) in zip(scores, pieces):
        e = jnp.exp2(s - m)
        part = jnp.sum(e, axis=-1, keepdims=True)
        pv = _dot(e.astype(BF16), v)
        den = part if den is None else den + part
        out = pv if out is None else out + pv
    if extra is not None:
        den = den + jnp.exp2(extra - m)
    return out / den


def _heads(ref, k, width=LANE):
    return ref[:, k * width:(k + 1) * width]


def _query_steps(run, q_ref, o_ref, qc_ref, oc_ref, nq):
    t = pl.program_id(1)
    if qc_ref is None:
        run(q_ref, o_ref, True)
        return

    @pl.when(t < nq)
    def _():
        run(q_ref, o_ref, True)

    @pl.when(t >= nq)
    def _():
        run(qc_ref, oc_ref, False)


def _split_refs(refs, n_extra):
    with_ctx = len(refs) == 8 + n_extra
    q, rest = refs[0], refs[1:]
    qc = None
    if with_ctx:
        qc, rest = rest[0], rest[1:]
    kl, vl, kc, vc = rest[:4]
    extras = rest[4:4 + n_extra]
    outs = rest[4 + n_extra:]
    return q, qc, kl, vl, kc, vc, extras, outs[0], (outs[1] if with_ctx else None)


def _attn_a_kernel(*refs, nq):
    q_ref, qc_ref, kl_ref, vl_ref, kc_ref, vc_ref, _, o_ref, oc_ref = _split_refs(refs, 0)

    def run(q_ref, o_ref, with_latent):
        for hh in range(N_HEADS):
            kvh = hh // (N_HEADS // KV_HEADS)
            pieces = [(_heads(kc_ref, kvh), _heads(vc_ref, kvh), None)]
            if with_latent:
                pieces.append((_heads(kl_ref, kvh), _heads(vl_ref, kvh), None))
            o_ref[:, hh * LANE:(hh + 1) * LANE] = _attend(_heads(q_ref, hh), pieces).astype(BF16)

    _query_steps(run, q_ref, o_ref, qc_ref, oc_ref, nq)


def _attn_b_kernel(*refs, nq, seq, win_k):
    q_ref, qc_ref, kl_ref, vl_ref, kc_ref, vc_ref, (sink_ref,), o_ref, oc_ref = _split_refs(refs, 1)
    t = pl.program_id(1)

    def run(q_ref, o_ref, with_latent):
        rows = q_ref.shape[0]
        if with_latent:
            q0 = t * rows
            start = pl.multiple_of(jnp.clip(q0 - WINDOW, 0, seq - win_k), LANE)
            r = lax.broadcasted_iota(jnp.int32, (rows, win_k), 0)
            c = lax.broadcasted_iota(jnp.int32, (rows, win_k), 1)
            mask = jnp.abs(q0 + r - (start + c)) <= WINDOW
        for hh in range(N_HEADS):
            kvh = hh // (N_HEADS // KV_HEADS)
            sink = LOG2E * sink_ref[0:1, hh:hh + 1]
            pieces = [(_heads(kc_ref, kvh), _heads(vc_ref, kvh), None)]
            if with_latent:
                kw = kl_ref[pl.ds(start, win_k), kvh * LANE:(kvh + 1) * LANE]
                vw = vl_ref[pl.ds(start, win_k), kvh * LANE:(kvh + 1) * LANE]
                pieces.append((kw, vw, mask))
            o_ref[:, hh * LANE:(hh + 1) * LANE] = _attend(_heads(q_ref, hh), pieces, extra=sink).astype(BF16)

    _query_steps(run, q_ref, o_ref, qc_ref, oc_ref, nq)


def _attn_c_kernel(*refs, nq, lam_init):
    q_ref, qc_ref, kl_ref, vl_ref, kc_ref, vc_ref, (lam_ref, g_ref), o_ref, oc_ref = _split_refs(refs, 2)
    lam = (jnp.exp(jnp.sum(lam_ref[0:1, :] * lam_ref[1:2, :], axis=-1, keepdims=True))
           - jnp.exp(jnp.sum(lam_ref[2:3, :] * lam_ref[3:4, :], axis=-1, keepdims=True)) + lam_init)

    def run(q_ref, o_ref, with_latent):
        rows = q_ref.shape[0]
        lane = lax.broadcasted_iota(jnp.int32, (rows, LANE), 1)
        for hh in range(N_HEADS):
            q = _heads(q_ref, hh)
            zero = jnp.zeros_like(q)
            pieces = [(_heads(kc_ref, hh), _heads(vc_ref, hh), None)]
            if with_latent:
                pieces.append((_heads(kl_ref, hh), _heads(vl_ref, hh), None))
            o1 = _attend(jnp.where(lane < C_QK_DIM, q, zero), pieces)
            o2 = _attend(jnp.where(lane >= C_QK_DIM, q, zero), pieces)
            o = _rms(o1 - lam * o2, g_ref[...]) * (1.0 - lam_init)
            o_ref[:, hh * LANE:(hh + 1) * LANE] = o.astype(BF16)

    _query_steps(run, q_ref, o_ref, qc_ref, oc_ref, nq)


def _attn_d_kernel(*refs, nq):
    q_ref, qc_ref, kl_ref, vl_ref, kc_ref, vc_ref, _, o_ref, oc_ref = _split_refs(refs, 0)

    def run(q_ref, o_ref, with_latent):
        split = ATTN_D_ROW_SPLIT if with_latent else 1
        piece = q_ref.shape[0] // split
        for hh in range(N_HEADS):
            pieces = [(_heads(kc_ref, hh, MLA_QK_PAD), _heads(vc_ref, hh), None)]
            if with_latent:
                pieces.append((_heads(kl_ref, hh, MLA_QK_PAD), _heads(vl_ref, hh), None))
            for r in range(split):
                rows = slice(r * piece, (r + 1) * piece)
                q = q_ref[rows, hh * MLA_QK_PAD:(hh + 1) * MLA_QK_PAD]
                o_ref[rows, hh * LANE:(hh + 1) * LANE] = _attend(q, pieces).astype(BF16)

    _query_steps(run, q_ref, o_ref, qc_ref, oc_ref, nq)


def _attention(kern, name, q_src, k_src, v_src, extras, *, batch, seq, ctx, with_ctx_queries,
               q_w, q_blk, k_w, k_blk, v_w, v_blk):
    nq = seq // TQ
    steps = nq + (1 if with_ctx_queries else 0)
    ctx_blk0 = batch * seq // ctx

    def qrow(b, t):
        return b * nq + jnp.minimum(t, nq - 1)

    in_specs = [pl.BlockSpec((TQ, q_w), lambda b, t: (qrow(b, t), q_blk))]
    args = [q_src]
    if with_ctx_queries:
        in_specs.append(pl.BlockSpec((ctx, q_w), lambda b, t: (ctx_blk0 + b, q_blk)))
        args.append(q_src)
    in_specs += [pl.BlockSpec((seq, k_w), lambda b, t: (b, k_blk)),
                 pl.BlockSpec((seq, v_w), lambda b, t: (b, v_blk)),
                 pl.BlockSpec((ctx, k_w), lambda b, t: (ctx_blk0 + b, k_blk)),
                 pl.BlockSpec((ctx, v_w), lambda b, t: (ctx_blk0 + b, v_blk))]
    args += [k_src, v_src, k_src, v_src]
    for e in extras:
        in_specs.append(pl.BlockSpec(e.shape, lambda b, t: (0, 0)))
        args.append(e)
    out_specs = [pl.BlockSpec((TQ, BRANCH_W), lambda b, t: (qrow(b, t), 0))]
    out_shape = [jax.ShapeDtypeStruct((batch * seq, BRANCH_W), BF16)]
    if with_ctx_queries:
        out_specs.append(pl.BlockSpec((ctx, BRANCH_W), lambda b, t: (b, 0)))
        out_shape.append(jax.ShapeDtypeStruct((batch * ctx, BRANCH_W), BF16))
    res = pl.pallas_call(
        functools.partial(kern, nq=nq),
        grid=(batch, steps),
        in_specs=in_specs, out_specs=out_specs, out_shape=out_shape,
        compiler_params=_cp(("arbitrary", "arbitrary"), 56),
        name=name,
    )(*args)
    return (res[0], res[1]) if with_ctx_queries else (res[0], None)


def _merge_kernel(*refs, n_lat_tiles, with_ctx):
    n_src = 2 * N_BRANCH if with_ctx else N_BRANCH
    lat_refs, ctx_refs = refs[:N_BRANCH], refs[N_BRANCH:n_src]
    g_refs = refs[n_src:n_src + N_BRANCH]
    w_ref, y_ref = refs[n_src + N_BRANCH:]

    def body(o_refs):
        acc = None
        for m, (o_ref, g_ref) in enumerate(zip(o_refs, g_refs)):
            p = g_ref[...].astype(F32) * _dot(o_ref[...], w_ref[m].astype(BF16))
            acc = p if acc is None else acc + p
        y_ref[...] = acc.astype(BF16)

    if not with_ctx:
        body(lat_refs)
        return
    i = pl.program_id(0)

    @pl.when(i < n_lat_tiles)
    def _():
        body(lat_refs)

    @pl.when(i >= n_lat_tiles)
    def _():
        body(ctx_refs)


def _merge(lat_outs, ctx_outs, gates, w_branch, *, layer, n_lat_tiles, n_tiles):
    n = n_tiles * TM
    d = w_branch.shape[3]
    tn = 512
    nb = d // tn
    with_ctx = ctx_outs is not None
    srcs = list(lat_outs) + (list(ctx_outs) if with_ctx else [])
    specs = [pl.BlockSpec((TM, BRANCH_W), lambda i, j: (jnp.minimum(i, n_lat_tiles - 1), 0))] * N_BRANCH
    if with_ctx:
        specs += [pl.BlockSpec((TM, BRANCH_W), lambda i, j: (jnp.maximum(i - n_lat_tiles, 0), 0))] * N_BRANCH
    g_specs = [pl.BlockSpec((TM, tn), functools.partial(lambda i, j, m: (i, m * nb + j), m=m))
               for m in range(N_BRANCH)]
    return pl.pallas_call(
        functools.partial(_merge_kernel, n_lat_tiles=n_lat_tiles, with_ctx=with_ctx),
        grid=(n_tiles, nb),
        in_specs=specs + g_specs + [pl.BlockSpec((None, N_BRANCH, BRANCH_W, tn), lambda i, j: (layer, 0, 0, j))],
        out_specs=pl.BlockSpec((TM, tn), lambda i, j: (i, j)),
        out_shape=jax.ShapeDtypeStruct((n, d), BF16),
        compiler_params=_cp(("arbitrary", "arbitrary"), 48),
        name="merge",
    )(*srcs, gates, gates, gates, gates, w_branch)


def _out_ln_kernel(y_ref, w_ref, x_ref, gate_ref, lng_ref, lnb_ref, sc_ref, sh_ref, xo_ref, ho_ref, *, alpha):
    piece = y_ref.shape[0] // 2
    for r in range(2):
        rows = slice(r * piece, (r + 1) * piece)
        mix = _dot(y_ref[rows, :], w_ref[...])
        z = alpha * x_ref[rows, :] + gate_ref[0] * mix
        xn = _standardise(z) * lng_ref[...] + lnb_ref[...]
        xo_ref[rows, :] = xn
        ho_ref[rows, :] = (xn * sc_ref[0] + sh_ref[0]).astype(BF16)


def _out_ln(y, w_out_bf16, x, gate_tbl, ln_g, ln_b, sc_tbl, sh_tbl, *, layer, n_rows, alpha):
    d = x.shape[1]
    tbl = pl.BlockSpec((1, 1, d), lambda i: ((i * TO) // TM, 0, 0))
    row = pl.BlockSpec((TO, d), lambda i: (i, 0))
    vec = pl.BlockSpec((1, d), lambda i: (0, 0))
    return pl.pallas_call(
        functools.partial(_out_ln_kernel, alpha=alpha),
        grid=(n_rows // TO,),
        in_specs=[row, pl.BlockSpec((None, d, d), lambda i: (layer, 0, 0)), row, tbl, vec, vec, tbl, tbl],
        out_specs=[row, row],
        out_shape=[jax.ShapeDtypeStruct((n_rows, d), F32), jax.ShapeDtypeStruct((n_rows, d), BF16)],
        compiler_params=_cp(("arbitrary",), 56),
        name="out_ln",
    )(y, w_out_bf16, x, gate_tbl, ln_g.reshape(1, d), ln_b.reshape(1, d), sc_tbl, sh_tbl)


def _ffn_kernel(h_ref, wg_ref, wu_ref, wo_ref, o_ref, acc_ref):
    f = pl.program_id(1)

    @pl.when(f == 0)
    def _():
        acc_ref[...] = jnp.zeros_like(acc_ref)

    h = h_ref[...]
    g = _dot(h, wg_ref[...].astype(BF16))
    u = _dot(h, wu_ref[...].astype(BF16))
    a = (g * (0.5 * jnp.tanh(0.5 * g) + 0.5) * u).astype(BF16)

    for n in range(acc_ref.shape[1] // FFN_ACC_TN):
        cols = slice(n * FFN_ACC_TN, (n + 1) * FFN_ACC_TN)
        acc_ref[:, cols] += _dot(a, wo_ref[:, cols].astype(BF16))

    @pl.when(f == pl.num_programs(1) - 1)
    def _():
        o_ref[...] = acc_ref[...].astype(BF16)


def _ffn(h, w_ffn_in, w_ffn_out, *, layer, n_tiles):
    d = h.shape[1]
    n = n_tiles * TM
    hidden = w_ffn_out.shape[1]
    tf = 512
    nf = hidden // tf
    return pl.pallas_call(
        _ffn_kernel,
        grid=(n_tiles, nf),
        in_specs=[pl.BlockSpec((TM, d), lambda i, f: (i, 0)),
                  pl.BlockSpec((None, d, tf), lambda i, f: (layer, 0, f)),
                  pl.BlockSpec((None, d, tf), lambda i, f: (layer, 0, nf + f)),
                  pl.BlockSpec((None, tf, d), lambda i, f: (layer, f, 0))],
        out_specs=pl.BlockSpec((TM, d), lambda i, f: (i, 0)),
        out_shape=jax.ShapeDtypeStruct((n, d), BF16),
        scratch_shapes=[pltpu.VMEM((TM, d), F32)],
        compiler_params=_cp(("arbitrary", "arbitrary"), 60),
        name="ffn",
    )(h, w_ffn_in, w_ffn_in, w_ffn_out)


def _rope_tables(seq, batch, ctx_rows):
    t = jnp.arange(seq, dtype=jnp.int32)
    pos_row = (t // GRID_W).astype(F32)[:, None]
    pos_col = (t % GRID_W).astype(F32)[:, None]

    def one(rot_dim, reps):
        axis_dim = rot_dim // 2
        freqs = ROPE_THETA ** (-jnp.arange(0, axis_dim, 2, dtype=F32) / axis_dim)
        ar, ac = pos_row * freqs[None, :], pos_col * freqs[None, :]
        cos = jnp.concatenate([jnp.cos(ar), jnp.cos(ar), jnp.cos(ac), jnp.cos(ac)], axis=-1)
        sin = jnp.concatenate([-jnp.sin(ar), jnp.sin(ar), -jnp.sin(ac), jnp.sin(ac)], axis=-1)
        return jnp.tile(cos, (1, reps)), jnp.tile(sin, (1, reps))

    c128, s128 = one(HEAD_DIM, 1)
    c64, s64 = one(C_QK_DIM, 2)
    ones = jnp.ones((ctx_rows, LANE), F32)
    zeros = jnp.zeros((ctx_rows, LANE), F32)

    def rows(tab, pad):
        return jnp.concatenate([jnp.tile(tab, (batch, 1)), pad])

    n = batch * seq + ctx_rows
    return jnp.stack([rows(c128, ones), rows(s128, zeros), rows(c64, ones), rows(s64, zeros),
                      jnp.ones((n, LANE), F32), jnp.zeros((n, LANE), F32)])


def kernel(x, c, ctx, c_ctx, w_ada, b_ada, w_in, a_qk_norm, b_sink, c_lambda, c_subln, mla_q_norm, mla_w_q_up, mla_kv_norm, mla_w_kv_up, w_branch, w_out, ln_mix_g, ln_mix_b, w_ffn_in, w_ffn_out, ln_ffn_g, ln_ffn_b):
    batch, seq, d = x.shape
    ctx_len = ctx.shape[1]
    depth = w_ada.shape[0]
    assert d == D_MODEL and seq % TM == 0 and (batch * ctx_len) % TM == 0 and batch + 1 <= 8
    n_lat_rows = batch * seq
    n_rows = n_lat_rows + batch * ctx_len
    tiles_per_seq = seq // TM
    n_lat_tiles = n_lat_rows // TM
    n_tiles = n_rows // TM
    alpha = (2 * depth) ** 0.25

    c_all = jnp.concatenate([c, c_ctx[None, :], jnp.zeros((8 - batch - 1, d), F32)], axis=0)
    ada = _ada_all(c_all, w_ada, b_ada)
    tile_row = np.array([b for b in range(batch) for _ in range(tiles_per_seq)]
                        + [batch] * (n_tiles - n_lat_tiles), np.int32)

    def table(l, chunk, plus_one=False):
        v = ada[l, :, chunk * d:(chunk + 1) * d][tile_row]
        return ((1.0 + v) if plus_one else v)[:, None, :]

    tabs = _rope_tables(seq, batch, batch * ctx_len)
    x_all, h = _init(x.reshape(n_lat_rows, d), ctx.reshape(batch * ctx_len, d),
                     table(0, 1, True), table(0, 0))

    win_k = min(TQ + 2 * WINDOW, seq)
    sink_pad = jnp.zeros((depth, 8, LANE), F32).at[:, 0, :N_HEADS].set(b_sink)
    wq_pad = jnp.pad(mla_w_q_up.reshape(depth, MLA_RANK, N_HEADS, MLA_NOPE + MLA_ROPE),
                     ((0, 0), (0, 0), (0, 0), (0, MLA_QK_PAD - MLA_NOPE - MLA_ROPE))
                     ).reshape(depth, MLA_RANK, N_HEADS * MLA_QK_PAD)
    wkv_pad = jnp.pad(mla_w_kv_up.reshape(depth, MLA_RANK, N_HEADS, 2, LANE).transpose(0, 1, 3, 2, 4)
                      .reshape(depth, MLA_RANK, 2 * N_HEADS * LANE),
                      ((0, 0), (0, 0), (0, N_HEADS * MLA_QK_PAD - 2 * N_HEADS * LANE)))

    w_in_t = jnp.swapaxes(w_in, 1, 2)
    w_out_bf16 = w_out.astype(BF16)

    for l in range(depth):
        last = l == depth - 1
        act_tiles = n_lat_tiles if last else n_tiles
        act_rows = act_tiles * TM
        lam_init = 0.8 - 0.6 * math.exp(-0.3 * l)

        up, gates = _proj(h, w_in_t, tabs, a_qk_norm[l], mla_q_norm[l], mla_kv_norm[l], layer=l)
        dq, dk, dv = _mla_up(up, wq_pad, wkv_pad, tabs, layer=l)

        common = dict(batch=batch, seq=seq, ctx=ctx_len, with_ctx_queries=not last)
        o_a = _attention(_attn_a_kernel, "attn_a", up, up, up, [], **common,
                         q_w=512, q_blk=COL_AQ // 512, k_w=256, k_blk=COL_AK // 256, v_w=256, v_blk=COL_AV // 256)
        o_b = _attention(functools.partial(_attn_b_kernel, seq=seq, win_k=win_k), "attn_b", up, up, up,
                         [sink_pad[l]], **common,
                         q_w=512, q_blk=COL_BQ // 512, k_w=256, k_blk=COL_BK // 256, v_w=256, v_blk=COL_BV // 256)
        o_c = _attention(functools.partial(_attn_c_kernel, lam_init=lam_init), "attn_c", up, up, up,
                         [c_lambda[l], c_subln[l].reshape(1, LANE)], **common,
                         q_w=512, q_blk=COL_CQ // 512, k_w=512, k_blk=COL_CK // 512, v_w=512, v_blk=COL_CV // 512)
        o_d = _attention(_attn_d_kernel, "attn_d", dq, dk, dv, [], **common,
                         q_w=N_HEADS * MLA_QK_PAD, q_blk=0, k_w=N_HEADS * MLA_QK_PAD, k_blk=0,
                         v_w=N_HEADS * LANE, v_blk=0)

        lat_outs, ctx_outs = zip(o_a, o_b, o_c, o_d)
        y = _merge(lat_outs, None if last else ctx_outs, gates, w_branch,
                   layer=l, n_lat_tiles=n_lat_tiles, n_tiles=act_tiles)
        x_all, h2 = _out_ln(y, w_out_bf16, x_all, table(l, 2), ln_mix_g[l], ln_mix_b[l],
                            table(l, 4, True), table(l, 3), layer=l, n_rows=act_rows, alpha=alpha)
        ffn = _ffn(h2, w_ffn_in, w_ffn_out, layer=l, n_tiles=act_tiles)
        if last:
            x_all, _ = _post_ln(x_all, ffn, table(l, 5), ln_ffn_g[l], ln_ffn_b[l], None, None,
                                n_rows=act_rows, alpha=alpha, out_rows=n_lat_rows)
        else:
            x_all, h = _post_ln(x_all, ffn, table(l, 5), ln_ffn_g[l], ln_ffn_b[l],
                                table(l + 1, 1, True), table(l + 1, 0),
                                n_rows=act_rows, alpha=alpha, out_rows=n_rows)
    return x_all.reshape(batch, seq, d)
```

```python
import functools
import math

import numpy as np
import jax
import jax.numpy as jnp
from jax import lax
from jax.experimental import pallas as pl
from jax.experimental.pallas import tpu as pltpu

F32 = jnp.float32
BF16 = jnp.bfloat16

D_MODEL = 2048
GRID_W = 64
HEAD_DIM = 128
WINDOW = 128
ROPE_THETA = 10000.0
NORM_EPS = 1e-6
NEG_INF = -1e30
LOG2E = math.log2(math.e)
N_HEADS = 4
KV_HEADS = 2
C_QK_DIM = 64
MLA_RANK = 512
MLA_NOPE = 128
MLA_ROPE = 64
MLA_QK_PAD = 256
BRANCH_W = 512
N_BRANCH = 4
FFN_HIDDEN = 5632
QKV_W = 4672
LANE = 128
GATE_ROW_OFF = QKV_W % LANE

UP_W = 5120
PROJ_TN = 512
N_UP_BLOCKS = UP_W // PROJ_TN
PROJ_ROW_SPLIT = 4
COL_AQ, COL_AK, COL_AV = 0, 512, 768
COL_BQ, COL_BK, COL_BV = 1024, 1536, 1792
COL_CQ, COL_CK, COL_CV = 2048, 2560, 3072
COL_DQA, COL_DKVA, COL_DKR = 3584, 4096, 4608

TM = 1024
PROJ_ROW_TILES = (1536, 1024, 512)
TAB_ROPE128, TAB_ROPE64, TAB_IDENT = 0, 1, 2
N_TAB_PLANES = 6
TO = 512
MERGE_TN = 512
TE = 512
TQ = 512
ATTN_D_ROW_SPLIT = 2
FFN_ACC_TN = 256
MIB = 1024 * 1024


def _cp(sem, vmem_mib):
    return pltpu.CompilerParams(dimension_semantics=sem, vmem_limit_bytes=vmem_mib * MIB)


def _dot(a, b):
    return jnp.dot(a, b, preferred_element_type=F32)


def _dot_nt(a, b):
    return lax.dot_general(a, b, (((1,), (1,)), ((), ())), preferred_element_type=F32)


def _ada_kernel(c_ref, w_ref, b_ref, o_ref):
    k = pl.program_id(1)
    c = c_ref[...]
    a = (c * (0.5 * jnp.tanh(0.5 * c) + 0.5)).astype(BF16)
    part = _dot(a, w_ref[...].astype(BF16))

    @pl.when(k == 0)
    def _():
        o_ref[...] = part + b_ref[...]

    @pl.when(k > 0)
    def _():
        o_ref[...] += part


def _ada_all(c_all, w_ada, b_ada):
    depth, d, n6 = w_ada.shape
    tk = 128
    return pl.pallas_call(
        _ada_kernel,
        grid=(depth, d // tk),
        in_specs=[pl.BlockSpec((8, tk), lambda l, k: (0, k)),
                  pl.BlockSpec((None, tk, n6), lambda l, k: (l, k, 0)),
                  pl.BlockSpec((None, 1, n6), lambda l, k: (l, 0, 0))],
        out_specs=pl.BlockSpec((None, 8, n6), lambda l, k: (l, 0, 0)),
        out_shape=jax.ShapeDtypeStruct((depth, 8, n6), F32),
        compiler_params=_cp(("arbitrary", "arbitrary"), 40),
        name="ada",
    )(c_all, w_ada, b_ada.reshape(depth, 1, n6))


def _standardise(z):
    mu = jnp.mean(z, axis=-1, keepdims=True)
    zc = z - mu
    var = jnp.mean(zc * zc, axis=-1, keepdims=True)
    return zc * lax.rsqrt(var + NORM_EPS)


def _init_kernel(x_ref, ctx_ref, sc_ref, sh_ref, xo_ref, ho_ref, *, n_lat):
    i = pl.program_id(0)

    def body(src):
        xs = _standardise(src[...])
        xo_ref[...] = xs
        ho_ref[...] = (xs * sc_ref[0] + sh_ref[0]).astype(BF16)

    @pl.when(i < n_lat)
    def _():
        body(x_ref)

    @pl.when(i >= n_lat)
    def _():
        body(ctx_ref)


def _init(x2, ctx2, sc_tbl, sh_tbl):
    n_lat_rows, d = x2.shape
    n_ctx_rows = ctx2.shape[0]
    n_lat = n_lat_rows // TE
    n_all = (n_lat_rows + n_ctx_rows) // TE
    tbl = pl.BlockSpec((1, 1, d), lambda i: ((i * TE) // TM, 0, 0))
    row = pl.BlockSpec((TE, d), lambda i: (i, 0))
    return pl.pallas_call(
        functools.partial(_init_kernel, n_lat=n_lat),
        grid=(n_all,),
        in_specs=[pl.BlockSpec((TE, d), lambda i: (jnp.minimum(i, n_lat - 1), 0)),
                  pl.BlockSpec((TE, d), lambda i: (jnp.maximum(i - n_lat, 0), 0)),
                  tbl, tbl],
        out_specs=[row, row],
        out_shape=[jax.ShapeDtypeStruct((n_lat_rows + n_ctx_rows, d), F32),
                   jax.ShapeDtypeStruct((n_lat_rows + n_ctx_rows, d), BF16)],
        compiler_params=_cp(("arbitrary",), 40),
        name="init_norm",
    )(x2, ctx2, sc_tbl, sh_tbl)


def _ln_kernel(x_ref, d_ref, gate_ref, lng_ref, lnb_ref, *rest, alpha, with_h):
    z = alpha * x_ref[...] + gate_ref[0] * d_ref[...].astype(F32)
    xn = _standardise(z) * lng_ref[...] + lnb_ref[...]
    if with_h:
        sc_ref, sh_ref, xo_ref, ho_ref = rest
        xo_ref[...] = xn
        ho_ref[...] = (xn * sc_ref[0] + sh_ref[0]).astype(BF16)
    else:
        (xo_ref,) = rest
        xo_ref[...] = xn


def _post_ln(x, delta, gate_tbl, ln_g, ln_b, sc_tbl, sh_tbl, *, n_rows, alpha, out_rows):
    d = x.shape[1]
    with_h = sc_tbl is not None
    tbl = pl.BlockSpec((1, 1, d), lambda i: ((i * TE) // TM, 0, 0))
    row = pl.BlockSpec((TE, d), lambda i: (i, 0))
    vec = pl.BlockSpec((1, d), lambda i: (0, 0))
    in_specs = [row, row, tbl, vec, vec]
    args = [x, delta, gate_tbl, ln_g.reshape(1, d), ln_b.reshape(1, d)]
    out_specs = [row]
    out_shape = [jax.ShapeDtypeStruct((out_rows, d), F32)]
    if with_h:
        in_specs += [tbl, tbl]
        args += [sc_tbl, sh_tbl]
        out_specs.append(row)
        out_shape.append(jax.ShapeDtypeStruct((out_rows, d), BF16))
    res = pl.pallas_call(
        functools.partial(_ln_kernel, alpha=alpha, with_h=with_h),
        grid=(n_rows // TE,),
        in_specs=in_specs, out_specs=out_specs, out_shape=out_shape,
        compiler_params=_cp(("arbitrary",), 40),
        name="post_ln",
    )(*args)
    return res if with_h else (res[0], None)


def _swap_select(x, near, far, bit):
    lane = lax.broadcasted_iota(jnp.int32, x.shape, 1)
    return jnp.where((lane & bit) == 0, pltpu.roll(x, far, 1), pltpu.roll(x, near, 1))


def _rope128(x, cos, sin):
    return x * cos + _swap_select(x, 32, LANE - 32, 32) * sin


def _rope64(x, cos, sin):
    return x * cos + _swap_select(x, 16, LANE - 16, 16) * sin


def _rms(x, g):
    return x * lax.rsqrt(jnp.mean(x * x, axis=-1, keepdims=True) + NORM_EPS) * g


def _proj_kernel(h_ref, wa_ref, wb_ref, tab_ref, aqk_ref, mqkv_ref, up_ref, g_ref):
    j = pl.program_id(1)
    piece = h_ref.shape[0] // PROJ_ROW_SPLIT
    blk = lambda col: col // PROJ_TN

    def block(pred, epilogue, width=PROJ_TN):
        @pl.when(pred)
        def _():
            w = wa_ref[:width, :].astype(BF16)
            for r in range(PROJ_ROW_SPLIT):
                rows = slice(r * piece, (r + 1) * piece)
                u = _dot_nt(h_ref[rows, :], w)
                cols = [u[:, c * LANE:(c + 1) * LANE] for c in range(width // LANE)]
                outs = epilogue(cols, rows)
                for c, val in enumerate(outs):
                    up_ref[rows, c * LANE:(c + 1) * LANE] = val.astype(BF16)

    def rope(x, kind, near, rows):
        lane = lax.broadcasted_iota(jnp.int32, x.shape, 1)
        swapped = jnp.where((lane & near) == 0, pltpu.roll(x, LANE - near, 1), pltpu.roll(x, near, 1))
        return x * tab_ref[2 * kind, rows, :] + swapped * tab_ref[2 * kind + 1, rows, :]

    def a_heads(cols, rows):
        is_q = j == blk(COL_AQ)
        gain = aqk_ref[pl.ds(j - blk(COL_AQ), 1), :]
        scale = jnp.where(is_q, LOG2E * HEAD_DIM ** -0.5, 1.0)
        outs = []
        for c, x in enumerate(cols):
            y = rope(_rms(x, gain), TAB_ROPE128, 32, rows) * scale
            outs.append(y if c < KV_HEADS else jnp.where(is_q, y, x))
        return outs

    block((j == blk(COL_AQ)) | (j == blk(COL_AK)), a_heads)

    def rope_heads(cols, rows):
        wide = j < blk(COL_CQ)
        near = jnp.where(wide, 32, 16)
        kind = jnp.where(wide, TAB_ROPE128, TAB_ROPE64)
        scale = jnp.where(j == blk(COL_BQ), LOG2E * HEAD_DIM ** -0.5,
                          jnp.where(j == blk(COL_CQ), LOG2E * C_QK_DIM ** -0.5, 1.0))
        all_plain = j == blk(COL_CV)
        tail_plain = all_plain | (j == blk(COL_BK))
        outs = []
        for c, x in enumerate(cols):
            k = jnp.where(all_plain if c < KV_HEADS else tail_plain, TAB_IDENT, kind)
            outs.append(rope(x, k, near, rows) * scale)
        return outs

    block((j >= blk(COL_BQ)) & (j <= blk(COL_CV)), rope_heads)

    def latent_norm(cols, rows):
        gain = mqkv_ref[pl.ds(j - blk(COL_DQA), 1), :]
        ms = sum(jnp.sum(x * x, axis=-1, keepdims=True) for x in cols) * (1.0 / (len(cols) * LANE))
        inv = lax.rsqrt(ms + NORM_EPS)
        return [x * inv * gain[:, c * LANE:(c + 1) * LANE] for c, x in enumerate(cols)]

    block((j == blk(COL_DQA)) | (j == blk(COL_DKVA)), latent_norm)

    def rope_key(cols, rows):
        x = cols[0]
        lane = lax.broadcasted_iota(jnp.int32, x.shape, 1)
        return [jnp.where(lane < MLA_ROPE, rope(x, TAB_ROPE64, 16, rows), 0.0)]

    block(j == blk(COL_DKR), rope_key, width=LANE)

    @pl.when(j == blk(COL_DKR))
    def _():
        up_ref[:, LANE:] = jnp.zeros((up_ref.shape[0], PROJ_TN - LANE), BF16)

    @pl.when(j >= N_UP_BLOCKS)
    def _():
        w = jnp.concatenate([wa_ref[GATE_ROW_OFF:, :], wb_ref[...]], axis=0).astype(BF16)
        for r in range(PROJ_ROW_SPLIT):
            rows = slice(r * piece, (r + 1) * piece)
            g = _dot_nt(h_ref[rows, :], w)
            g_ref[rows, :] = (0.5 * jnp.tanh(0.5 * g) + 0.5).astype(BF16)


def _row_tile(n, choices):
    return next(t for t in choices if n % t == 0)


def _proj(h, w_in_t, tabs, aqk, mq, mkv, *, layer):
    n, d = h.shape
    tp = _row_tile(n, PROJ_ROW_TILES)
    nt = n // tp
    n_gate = w_in_t.shape[1] - QKV_W
    gate_blocks = n_gate // PROJ_TN
    first_gate_blk = (QKV_W - GATE_ROW_OFF) // PROJ_TN
    assert (QKV_W - GATE_ROW_OFF) % PROJ_TN == 0 and n_gate % PROJ_TN == 0 and first_gate_blk == N_UP_BLOCKS - 1

    def wa_idx(i, j):
        return (layer, jnp.where(j < N_UP_BLOCKS, j, j - 1), 0)

    def wb_idx(i, j):
        g = jnp.maximum(j - N_UP_BLOCKS, 0)
        return (layer, (QKV_W - GATE_ROW_OFF + (g + 1) * PROJ_TN) // GATE_ROW_OFF, 0)

    return pl.pallas_call(
        _proj_kernel,
        grid=(nt, N_UP_BLOCKS + gate_blocks),
        in_specs=[pl.BlockSpec((tp, d), lambda i, j: (i, 0)),
                  pl.BlockSpec((None, PROJ_TN, d), wa_idx),
                  pl.BlockSpec((None, GATE_ROW_OFF, d), wb_idx),
                  pl.BlockSpec((N_TAB_PLANES, tp, LANE), lambda i, j: (0, i, 0)),
                  pl.BlockSpec((2, LANE), lambda i, j: (0, 0)),
                  pl.BlockSpec((2, MLA_RANK), lambda i, j: (0, 0))],
        out_specs=[pl.BlockSpec((tp, PROJ_TN), lambda i, j: (i, jnp.minimum(j, N_UP_BLOCKS - 1))),
                   pl.BlockSpec((tp, PROJ_TN), lambda i, j: (i, jnp.maximum(j - N_UP_BLOCKS, 0)))],
        out_shape=[jax.ShapeDtypeStruct((n, UP_W), BF16),
                   jax.ShapeDtypeStruct((n, n_gate), BF16)],
        compiler_params=_cp(("arbitrary", "arbitrary"), 58),
        name="proj",
    )(h, w_in_t, w_in_t, tabs, aqk, jnp.stack([mq, mkv]))


def _mla_up_kernel(qa_ref, kva_ref, kr_ref, wq_ref, wkv_ref, tab_ref, dq_ref, dk_ref, dv_ref):
    cos, sin = tab_ref[2 * TAB_ROPE64], tab_ref[2 * TAB_ROPE64 + 1]
    q = _dot(qa_ref[...], wq_ref[...].astype(BF16))
    kv = _dot(kva_ref[...], wkv_ref[...].astype(BF16))
    scale = LOG2E * (MLA_NOPE + MLA_ROPE) ** -0.5
    kr = kr_ref[...]
    for hh in range(N_HEADS):
        base = hh * MLA_QK_PAD
        dq_ref[:, base:base + LANE] = (q[:, base:base + LANE] * scale).astype(BF16)
        dq_ref[:, base + LANE:base + 2 * LANE] = (
            _rope64(q[:, base + LANE:base + 2 * LANE], cos, sin) * scale).astype(BF16)
        dk_ref[:, base:base + LANE] = kv[:, hh * LANE:(hh + 1) * LANE].astype(BF16)
        dk_ref[:, base + LANE:base + 2 * LANE] = kr
    dv_ref[...] = kv[:, N_HEADS * LANE:].astype(BF16)


def _mla_up(up, wq_pad, wkv_perm, tabs, *, layer):
    n = up.shape[0]
    nt = n // TM
    wide = N_HEADS * MLA_QK_PAD
    return pl.pallas_call(
        _mla_up_kernel,
        grid=(nt,),
        in_specs=[pl.BlockSpec((TM, MLA_RANK), lambda i: (i, COL_DQA // MLA_RANK)),
                  pl.BlockSpec((TM, MLA_RANK), lambda i: (i, COL_DKVA // MLA_RANK)),
                  pl.BlockSpec((TM, LANE), lambda i: (i, COL_DKR // LANE)),
                  pl.BlockSpec((None, MLA_RANK, wide), lambda i: (layer, 0, 0)),
                  pl.BlockSpec((None, MLA_RANK, wide), lambda i: (layer, 0, 0)),
                  pl.BlockSpec((N_TAB_PLANES, TM, LANE), lambda i: (0, i, 0))],
        out_specs=[pl.BlockSpec((TM, wide), lambda i: (i, 0)),
                   pl.BlockSpec((TM, wide), lambda i: (i, 0)),
                   pl.BlockSpec((TM, N_HEADS * LANE), lambda i: (i, 0))],
        out_shape=[jax.ShapeDtypeStruct((n, wide), BF16),
                   jax.ShapeDtypeStruct((n, wide), BF16),
                   jax.ShapeDtypeStruct((n, N_HEADS * LANE), BF16)],
        compiler_params=_cp(("arbitrary",), 48),
        name="mla_up",
    )(up, up, up, wq_pad, wkv_perm, tabs)


def _attend(q, pieces, extra=None):
    scores = []
    for k, _, mask in pieces:
        s = _dot_nt(q, k)
        if mask is not None:
            s = jnp.where(mask, s, NEG_INF)
        scores.append(s)
    m = functools.reduce(jnp.maximum, [jnp.max(s, axis=-1, keepdims=True) for s in scores])
    if extra is not None:
        m = jnp.maximum(m, extra)
    den = None
    out = None
    for s, (_, v, _) in zip(scores, pieces):
        e = jnp.exp2(s - m)
        part = jnp.sum(e, axis=-1, keepdims=True)
        pv = _dot(e.astype(BF16), v)
        den = part if den is None else den + part
        out = pv if out is None else out + pv
    if extra is not None:
        den = den + jnp.exp2(extra - m)
    return out / den


def _heads(ref, k, width=LANE):
    return ref[:, k * width:(k + 1) * width]


def _query_steps(run, q_ref, o_ref, qc_ref, oc_ref, nq):
    t = pl.program_id(1)
    if qc_ref is None:
        run(q_ref, o_ref, True)
        return

    @pl.when(t < nq)
    def _():
        run(q_ref, o_ref, True)

    @pl.when(t >= nq)
    def _():
        run(qc_ref, oc_ref, False)


def _split_refs(refs, n_extra):
    with_ctx = len(refs) == 8 + n_extra
    q, rest = refs[0], refs[1:]
    qc = None
    if with_ctx:
        qc, rest = rest[0], rest[1:]
    kl, vl, kc, vc = rest[:4]
    extras = rest[4:4 + n_extra]
    outs = rest[4 + n_extra:]
    return q, qc, kl, vl, kc, vc, extras, outs[0], (outs[1] if with_ctx else None)


def _attn_a_kernel(*refs, nq):
    q_ref, qc_ref, kl_ref, vl_ref, kc_ref, vc_ref, _, o_ref, oc_ref = _split_refs(refs, 0)

    def run(q_ref, o_ref, with_latent):
        for hh in range(N_HEADS):
            kvh = hh // (N_HEADS // KV_HEADS)
            pieces = [(_heads(kc_ref, kvh), _heads(vc_ref, kvh), None)]
            if with_latent:
                pieces.append((_heads(kl_ref, kvh), _heads(vl_ref, kvh), None))
            o_ref[:, hh * LANE:(hh + 1) * LANE] = _attend(_heads(q_ref, hh), pieces).astype(BF16)

    _query_steps(run, q_ref, o_ref, qc_ref, oc_ref, nq)


def _attn_b_kernel(*refs, nq, seq, win_k):
    q_ref, qc_ref, kl_ref, vl_ref, kc_ref, vc_ref, (sink_ref,), o_ref, oc_ref = _split_refs(refs, 1)
    t = pl.program_id(1)

    def run(q_ref, o_ref, with_latent):
        rows = q_ref.shape[0]
        if with_latent:
            q0 = t * rows
            start = pl.multiple_of(jnp.clip(q0 - WINDOW, 0, seq - win_k), LANE)
            r = lax.broadcasted_iota(jnp.int32, (rows, win_k), 0)
            c = lax.broadcasted_iota(jnp.int32, (rows, win_k), 1)
            mask = jnp.abs(q0 + r - (start + c)) <= WINDOW
        for hh in range(N_HEADS):
            kvh = hh // (N_HEADS // KV_HEADS)
            sink = LOG2E * sink_ref[0:1, hh:hh + 1]
            pieces = [(_heads(kc_ref, kvh), _heads(vc_ref, kvh), None)]
            if with_latent:
                kw = kl_ref[pl.ds(start, win_k), kvh * LANE:(kvh + 1) * LANE]
                vw = vl_ref[pl.ds(start, win_k), kvh * LANE:(kvh + 1) * LANE]
                pieces.append((kw, vw, mask))
            o_ref[:, hh * LANE:(hh + 1) * LANE] = _attend(_heads(q_ref, hh), pieces, extra=sink).astype(BF16)

    _query_steps(run, q_ref, o_ref, qc_ref, oc_ref, nq)


def _attn_c_kernel(*refs, nq, lam_init):
    q_ref, qc_ref, kl_ref, vl_ref, kc_ref, vc_ref, (lam_ref, g_ref), o_ref, oc_ref = _split_refs(refs, 2)
    lam = (jnp.exp(jnp.sum(lam_ref[0:1, :] * lam_ref[1:2, :], axis=-1, keepdims=True))
           - jnp.exp(jnp.sum(lam_ref[2:3, :] * lam_ref[3:4, :], axis=-1, keepdims=True)) + lam_init)

    def run(q_ref, o_ref, with_latent):
        rows = q_ref.shape[0]
        lane = lax.broadcasted_iota(jnp.int32, (rows, LANE), 1)
        for hh in range(N_HEADS):
            q = _heads(q_ref, hh)
            zero = jnp.zeros_like(q)
            pieces = [(_heads(kc_ref, hh), _heads(vc_ref, hh), None)]
            if with_latent:
                pieces.append((_heads(kl_ref, hh), _heads(vl_ref, hh), None))
            o1 = _attend(jnp.where(lane < C_QK_DIM, q, zero), pieces)
            o2 = _attend(jnp.where(lane >= C_QK_DIM, q, zero), pieces)
            o = _rms(o1 - lam * o2, g_ref[...]) * (1.0 - lam_init)
            o_ref[:, hh * LANE:(hh + 1) * LANE] = o.astype(BF16)

    _query_steps(run, q_ref, o_ref, qc_ref, oc_ref, nq)


def _attn_d_kernel(*refs, nq):
    q_ref, qc_ref, kl_ref, vl_ref, kc_ref, vc_ref, _, o_ref, oc_ref = _split_refs(refs, 0)

    def run(q_ref, o_ref, with_latent):
        split = ATTN_D_ROW_SPLIT if with_latent else 1
        piece = q_ref.shape[0] // split
        for hh in range(N_HEADS):
            pieces = [(_heads(kc_ref, hh, MLA_QK_PAD), _heads(vc_ref, hh), None)]
            if with_latent:
                pieces.append((_heads(kl_ref, hh, MLA_QK_PAD), _heads(vl_ref, hh), None))
            for r in range(split):
                rows = slice(r * piece, (r + 1) * piece)
                q = q_ref[rows, hh * MLA_QK_PAD:(hh + 1) * MLA_QK_PAD]
                o_ref[rows, hh * LANE:(hh + 1) * LANE] = _attend(q, pieces).astype(BF16)

    _query_steps(run, q_ref, o_ref, qc_ref, oc_ref, nq)


def _attention(kern, name, q_src, k_src, v_src, extras, *, batch, seq, ctx, with_ctx_queries,
               q_w, q_blk, k_w, k_blk, v_w, v_blk):
    nq = seq // TQ
    steps = nq + (1 if with_ctx_queries else 0)
    ctx_blk0 = batch * seq // ctx

    def qrow(b, t):
        return b * nq + jnp.minimum(t, nq - 1)

    in_specs = [pl.BlockSpec((TQ, q_w), lambda b, t: (qrow(b, t), q_blk))]
    args = [q_src]
    if with_ctx_queries:
        in_specs.append(pl.BlockSpec((ctx, q_w), lambda b, t: (ctx_blk0 + b, q_blk)))
        args.append(q_src)
    in_specs += [pl.BlockSpec((seq, k_w), lambda b, t: (b, k_blk)),
                 pl.BlockSpec((seq, v_w), lambda b, t: (b, v_blk)),
                 pl.BlockSpec((ctx, k_w), lambda b, t: (ctx_blk0 + b, k_blk)),
                 pl.BlockSpec((ctx, v_w), lambda b, t: (ctx_blk0 + b, v_blk))]
    args += [k_src, v_src, k_src, v_src]
    for e in extras:
        in_specs.append(pl.BlockSpec(e.shape, lambda b, t: (0, 0)))
        args.append(e)
    out_specs = [pl.BlockSpec((TQ, BRANCH_W), lambda b, t: (qrow(b, t), 0))]
    out_shape = [jax.ShapeDtypeStruct((batch * seq, BRANCH_W), BF16)]
    if with_ctx_queries:
        out_specs.append(pl.BlockSpec((ctx, BRANCH_W), lambda b, t: (b, 0)))
        out_shape.append(jax.ShapeDtypeStruct((batch * ctx, BRANCH_W), BF16))
    res = pl.pallas_call(
        functools.partial(kern, nq=nq),
        grid=(batch, steps),
        in_specs=in_specs, out_specs=out_specs, out_shape=out_shape,
        compiler_params=_cp(("arbitrary", "arbitrary"), 56),
        name=name,
    )(*args)
    return (res[0], res[1]) if with_ctx_queries else (res[0], None)


def _merge_kernel(*refs, n_lat_tiles, with_ctx):
    n_src = 2 * N_BRANCH if with_ctx else N_BRANCH
    lat_refs, ctx_refs = refs[:N_BRANCH], refs[N_BRANCH:n_src]
    g_ref, w_ref, y_ref = refs[n_src:]
    d = y_ref.shape[1]

    def body(o_refs):
        for c in range(d // MERGE_TN):
            cols = slice(c * MERGE_TN, (c + 1) * MERGE_TN)
            acc = None
            for m, o_ref in enumerate(o_refs):
                gate = g_ref[:, m * d + c * MERGE_TN:m * d + (c + 1) * MERGE_TN].astype(F32)
                p = gate * _dot(o_ref[...], w_ref[m, :, cols])
                acc = p if acc is None else acc + p
            y_ref[:, cols] = acc.astype(BF16)

    if not with_ctx:
        body(lat_refs)
        return
    i = pl.program_id(0)

    @pl.when(i < n_lat_tiles)
    def _():
        body(lat_refs)

    @pl.when(i >= n_lat_tiles)
    def _():
        body(ctx_refs)


def _merge(lat_outs, ctx_outs, gates, w_branch_bf16, *, layer, n_lat_rows, n_rows):
    d = w_branch_bf16.shape[3]
    n_lat_tiles = n_lat_rows // TO
    with_ctx = ctx_outs is not None
    srcs = list(lat_outs) + (list(ctx_outs) if with_ctx else [])
    specs = [pl.BlockSpec((TO, BRANCH_W), lambda i: (jnp.minimum(i, n_lat_tiles - 1), 0))] * N_BRANCH
    if with_ctx:
        specs += [pl.BlockSpec((TO, BRANCH_W), lambda i: (jnp.maximum(i - n_lat_tiles, 0), 0))] * N_BRANCH
    return pl.pallas_call(
        functools.partial(_merge_kernel, n_lat_tiles=n_lat_tiles, with_ctx=with_ctx),
        grid=(n_rows // TO,),
        in_specs=specs + [pl.BlockSpec((TO, N_BRANCH * d), lambda i: (i, 0)),
                          pl.BlockSpec((None, N_BRANCH, BRANCH_W, d), lambda i: (layer, 0, 0, 0))],
        out_specs=pl.BlockSpec((TO, d), lambda i: (i, 0)),
        out_shape=jax.ShapeDtypeStruct((n_rows, d), BF16),
        compiler_params=_cp(("arbitrary",), 56),
        name="merge",
    )(*srcs, gates, w_branch_bf16)


def _out_ln_kernel(y_ref, w_ref, x_ref, gate_ref, lng_ref, lnb_ref, sc_ref, sh_ref, xo_ref, ho_ref, *, alpha):
    piece = y_ref.shape[0] // 2
    for r in range(2):
        rows = slice(r * piece, (r + 1) * piece)
        mix = _dot(y_ref[rows, :], w_ref[...])
        z = alpha * x_ref[rows, :] + gate_ref[0] * mix
        xn = _standardise(z) * lng_ref[...] + lnb_ref[...]
        xo_ref[rows, :] = xn
        ho_ref[rows, :] = (xn * sc_ref[0] + sh_ref[0]).astype(BF16)


def _out_ln(y, w_out_bf16, x, gate_tbl, ln_g, ln_b, sc_tbl, sh_tbl, *, layer, n_rows, alpha):
    d = x.shape[1]
    tbl = pl.BlockSpec((1, 1, d), lambda i: ((i * TO) // TM, 0, 0))
    row = pl.BlockSpec((TO, d), lambda i: (i, 0))
    vec = pl.BlockSpec((1, d), lambda i: (0, 0))
    return pl.pallas_call(
        functools.partial(_out_ln_kernel, alpha=alpha),
        grid=(n_rows // TO,),
        in_specs=[row, pl.BlockSpec((None, d, d), lambda i: (layer, 0, 0)), row, tbl, vec, vec, tbl, tbl],
        out_specs=[row, row],
        out_shape=[jax.ShapeDtypeStruct((n_rows, d), F32), jax.ShapeDtypeStruct((n_rows, d), BF16)],
        compiler_params=_cp(("arbitrary",), 56),
        name="out_ln",
    )(y, w_out_bf16, x, gate_tbl, ln_g.reshape(1, d), ln_b.reshape(1, d), sc_tbl, sh_tbl)


def _ffn_kernel(h_ref, wg_ref, wu_ref, wo_ref, o_ref, acc_ref):
    f = pl.program_id(1)

    @pl.when(f == 0)
    def _():
        acc_ref[...] = jnp.zeros_like(acc_ref)

    h = h_ref[...]
    g = _dot(h, wg_ref[...].astype(BF16))
    u = _dot(h, wu_ref[...].astype(BF16))
    a = (g * (0.5 * jnp.tanh(0.5 * g) + 0.5) * u).astype(BF16)

    for n in range(acc_ref.shape[1] // FFN_ACC_TN):
        cols = slice(n * FFN_ACC_TN, (n + 1) * FFN_ACC_TN)
        acc_ref[:, cols] += _dot(a, wo_ref[:, cols].astype(BF16))

    @pl.when(f == pl.num_programs(1) - 1)
    def _():
        o_ref[...] = acc_ref[...].astype(BF16)


def _ffn(h, w_ffn_in, w_ffn_out, *, layer, n_tiles):
    d = h.shape[1]
    n = n_tiles * TM
    hidden = w_ffn_out.shape[1]
    tf = 512
    nf = hidden // tf
    return pl.pallas_call(
        _ffn_kernel,
        grid=(n_tiles, nf),
        in_specs=[pl.BlockSpec((TM, d), lambda i, f: (i, 0)),
                  pl.BlockSpec((None, d, tf), lambda i, f: (layer, 0, f)),
                  pl.BlockSpec((None, d, tf), lambda i, f: (layer, 0, nf + f)),
                  pl.BlockSpec((None, tf, d), lambda i, f: (layer, f, 0))],
        out_specs=pl.BlockSpec((TM, d), lambda i, f: (i, 0)),
        out_shape=jax.ShapeDtypeStruct((n, d), BF16),
        scratch_shapes=[pltpu.VMEM((TM, d), F32)],
        compiler_params=_cp(("arbitrary", "arbitrary"), 60),
        name="ffn",
    )(h, w_ffn_in, w_ffn_in, w_ffn_out)


def _rope_tables(seq, batch, ctx_rows):
    t = jnp.arange(seq, dtype=jnp.int32)
    pos_row = (t // GRID_W).astype(F32)[:, None]
    pos_col = (t % GRID_W).astype(F32)[:, None]

    def one(rot_dim, reps):
        axis_dim = rot_dim // 2
        freqs = ROPE_THETA ** (-jnp.arange(0, axis_dim, 2, dtype=F32) / axis_dim)
        ar, ac = pos_row * freqs[None, :], pos_col * freqs[None, :]
        cos = jnp.concatenate([jnp.cos(ar), jnp.cos(ar), jnp.cos(ac), jnp.cos(ac)], axis=-1)
        sin = jnp.concatenate([-jnp.sin(ar), jnp.sin(ar), -jnp.sin(ac), jnp.sin(ac)], axis=-1)
        return jnp.tile(cos, (1, reps)), jnp.tile(sin, (1, reps))

    c128, s128 = one(HEAD_DIM, 1)
    c64, s64 = one(C_QK_DIM, 2)
    ones = jnp.ones((ctx_rows, LANE), F32)
    zeros = jnp.zeros((ctx_rows, LANE), F32)

    def rows(tab, pad):
        return jnp.concatenate([jnp.tile(tab, (batch, 1)), pad])

    n = batch * seq + ctx_rows
    return jnp.stack([rows(c128, ones), rows(s128, zeros), rows(c64, ones), rows(s64, zeros),
                      jnp.ones((n, LANE), F32), jnp.zeros((n, LANE), F32)])


def kernel(x, c, ctx, c_ctx, w_ada, b_ada, w_in, a_qk_norm, b_sink, c_lambda, c_subln, mla_q_norm, mla_w_q_up, mla_kv_norm, mla_w_kv_up, w_branch, w_out, ln_mix_g, ln_mix_b, w_ffn_in, w_ffn_out, ln_ffn_g, ln_ffn_b):
    batch, seq, d = x.shape
    ctx_len = ctx.shape[1]
    depth = w_ada.shape[0]
    assert d == D_MODEL and seq % TM == 0 and (batch * ctx_len) % TM == 0 and batch + 1 <= 8
    n_lat_rows = batch * seq
    n_rows = n_lat_rows + batch * ctx_len
    tiles_per_seq = seq // TM
    n_lat_tiles = n_lat_rows // TM
    n_tiles = n_rows // TM
    alpha = (2 * depth) ** 0.25

    c_all = jnp.concatenate([c, c_ctx[None, :], jnp.zeros((8 - batch - 1, d), F32)], axis=0)
    ada = _ada_all(c_all, w_ada, b_ada)
    tile_row = np.array([b for b in range(batch) for _ in range(tiles_per_seq)]
                        + [batch] * (n_tiles - n_lat_tiles), np.int32)

    def table(l, chunk, plus_one=False):
        v = ada[l, :, chunk * d:(chunk + 1) * d][tile_row]
        return ((1.0 + v) if plus_one else v)[:, None, :]

    tabs = _rope_tables(seq, batch, batch * ctx_len)
    x_all, h = _init(x.reshape(n_lat_rows, d), ctx.reshape(batch * ctx_len, d),
                     table(0, 1, True), table(0, 0))

    win_k = min(TQ + 2 * WINDOW, seq)
    sink_pad = jnp.zeros((depth, 8, LANE), F32).at[:, 0, :N_HEADS].set(b_sink)
    wq_pad = jnp.pad(mla_w_q_up.reshape(depth, MLA_RANK, N_HEADS, MLA_NOPE + MLA_ROPE),
                     ((0, 0), (0, 0), (0, 0), (0, MLA_QK_PAD - MLA_NOPE - MLA_ROPE))
                     ).reshape(depth, MLA_RANK, N_HEADS * MLA_QK_PAD)
    wkv_pad = jnp.pad(mla_w_kv_up.reshape(depth, MLA_RANK, N_HEADS, 2, LANE).transpose(0, 1, 3, 2, 4)
                      .reshape(depth, MLA_RANK, 2 * N_HEADS * LANE),
                      ((0, 0), (0, 0), (0, N_HEADS * MLA_QK_PAD - 2 * N_HEADS * LANE)))

    w_in_t = jnp.swapaxes(w_in, 1, 2)
    w_out_bf16 = w_out.astype(BF16)
    w_branch_bf16 = w_branch.astype(BF16)

    for l in range(depth):
        last = l == depth - 1
        act_tiles = n_lat_tiles if last else n_tiles
        act_rows = act_tiles * TM
        lam_init = 0.8 - 0.6 * math.exp(-0.3 * l)

        up, gates = _proj(h, w_in_t, tabs, a_qk_norm[l], mla_q_norm[l], mla_kv_norm[l], layer=l)
        dq, dk, dv = _mla_up(up, wq_pad, wkv_pad, tabs, layer=l)

        common = dict(batch=batch, seq=seq, ctx=ctx_len, with_ctx_queries=not last)
        o_a = _attention(_attn_a_kernel, "attn_a", up, up, up, [], **common,
                         q_w=512, q_blk=COL_AQ // 512, k_w=256, k_blk=COL_AK // 256, v_w=256, v_blk=COL_AV // 256)
        o_b = _attention(functools.partial(_attn_b_kernel, seq=seq, win_k=win_k), "attn_b", up, up, up,
                         [sink_pad[l]], **common,
                         q_w=512, q_blk=COL_BQ // 512, k_w=256, k_blk=COL_BK // 256, v_w=256, v_blk=COL_BV // 256)
        o_c = _attention(functools.partial(_attn_c_kernel, lam_init=lam_init), "attn_c", up, up, up,
                         [c_lambda[l], c_subln[l].reshape(1, LANE)], **common,
                         q_w=512, q_blk=COL_CQ // 512, k_w=512, k_blk=COL_CK // 512, v_w=512, v_blk=COL_CV // 512)
        o_d = _attention(_attn_d_kernel, "attn_d", dq, dk, dv, [], **common,
                         q_w=N_HEADS * MLA_QK_PAD, q_blk=0, k_w=N_HEADS * MLA_QK_PAD, k_blk=0,
                         v_w=N_HEADS * LANE, v_blk=0)

        lat_outs, ctx_outs = zip(o_a, o_b, o_c, o_d)
        y = _merge(lat_outs, None if last else ctx_outs, gates, w_branch_bf16,
                   layer=l, n_lat_rows=n_lat_rows, n_rows=act_rows)
        x_all, h2 = _out_ln(y, w_out_bf16, x_all, table(l, 2), ln_mix_g[l], ln_mix_b[l],
                            table(l, 4, True), table(l, 3), layer=l, n_rows=act_rows, alpha=alpha)
        ffn = _ffn(h2, w_ffn_in, w_ffn_out, layer=l, n_tiles=act_tiles)
        if last:
            x_all, _ = _post_ln(x_all, ffn, table(l, 5), ln_ffn_g[l], ln_ffn_b[l], None, None,
                                n_rows=act_rows, alpha=alpha, out_rows=n_lat_rows)
        else:
            x_all, h = _post_ln(x_all, ffn, table(l, 5), ln_ffn_g[l], ln_ffn_b[l],
                                table(l + 1, 1, True), table(l + 1, 0),
                                n_rows=act_rows, alpha=alpha, out_rows=n_rows)
    return x_all.reshape(batch, seq, d)
```

```python
import functools
import math

import numpy as np
import jax
import jax.numpy as jnp
from jax import lax
from jax.experimental import pallas as pl
from jax.experimental.pallas import tpu as pltpu

F32 = jnp.float32
BF16 = jnp.bfloat16

D_MODEL = 2048
GRID_W = 64
HEAD_DIM = 128
WINDOW = 128
ROPE_THETA = 10000.0
NORM_EPS = 1e-6
NEG_INF = -1e30
LOG2E = math.log2(math.e)
N_HEADS = 4
KV_HEADS = 2
C_QK_DIM = 64
MLA_RANK = 512
MLA_NOPE = 128
MLA_ROPE = 64
MLA_QK_PAD = 256
BRANCH_W = 512
N_BRANCH = 4
QKV_W = 4672
LANE = 128
GATE_ROW_OFF = QKV_W % LANE

UP_W = 5120
PROJ_TN = 512
N_UP_BLOCKS = UP_W // PROJ_TN
PROJ_ROW_SPLIT = 4
COL_AQ, COL_AK, COL_AV = 0, 512, 768
COL_BQ, COL_BK, COL_BV = 1024, 1536, 1792
COL_CQ, COL_CK, COL_CV = 2048, 2560, 3072
COL_DQA, COL_DKVA, COL_DKR = 3584, 4096, 4608

TM = 1024
PROJ_ROW_TILES = (1536, 1024, 512)
TAB_ROPE128, TAB_ROPE64 = 0, 1
N_TAB_PLANES = 4
TO = 512
MERGE_TN = 512
TE = 512
TQ = 512
ATTN_ROW_SPLIT = 2
FFN_ACC_TN = 256
MIB = 1024 * 1024


def _cp(sem, vmem_mib):
    return pltpu.CompilerParams(dimension_semantics=sem, vmem_limit_bytes=vmem_mib * MIB)


def _dot(a, b):
    return jnp.dot(a, b, preferred_element_type=F32)


def _dot_nt(a, b):
    return lax.dot_general(a, b, (((1,), (1,)), ((), ())), preferred_element_type=F32)


def _ada_kernel(c_ref, w_ref, b_ref, o_ref):
    k = pl.program_id(1)
    c = c_ref[...]
    a = (c * (0.5 * jnp.tanh(0.5 * c) + 0.5)).astype(BF16)
    part = _dot(a, w_ref[...].astype(BF16))

    @pl.when(k == 0)
    def _():
        o_ref[...] = part + b_ref[...]

    @pl.when(k > 0)
    def _():
        o_ref[...] += part


def _ada_all(c_all, w_ada, b_ada):
    depth, d, n6 = w_ada.shape
    tk = 128
    return pl.pallas_call(
        _ada_kernel,
        grid=(depth, d // tk),
        in_specs=[pl.BlockSpec((8, tk), lambda l, k: (0, k)),
                  pl.BlockSpec((None, tk, n6), lambda l, k: (l, k, 0)),
                  pl.BlockSpec((None, 1, n6), lambda l, k: (l, 0, 0))],
        out_specs=pl.BlockSpec((None, 8, n6), lambda l, k: (l, 0, 0)),
        out_shape=jax.ShapeDtypeStruct((depth, 8, n6), F32),
        compiler_params=_cp(("arbitrary", "arbitrary"), 40),
        name="ada",
    )(c_all, w_ada, b_ada.reshape(depth, 1, n6))


def _standardise(z):
    mu = jnp.mean(z, axis=-1, keepdims=True)
    zc = z - mu
    var = jnp.mean(zc * zc, axis=-1, keepdims=True)
    return zc * lax.rsqrt(var + NORM_EPS)


def _init_kernel(x_ref, ctx_ref, sc_ref, sh_ref, xo_ref, ho_ref, *, n_lat):
    i = pl.program_id(0)

    def body(src):
        xs = _standardise(src[...])
        xo_ref[...] = xs
        ho_ref[...] = (xs * sc_ref[0] + sh_ref[0]).astype(BF16)

    @pl.when(i < n_lat)
    def _():
        body(x_ref)

    @pl.when(i >= n_lat)
    def _():
        body(ctx_ref)


def _init(x2, ctx2, sc_tbl, sh_tbl):
    n_lat_rows, d = x2.shape
    n_ctx_rows = ctx2.shape[0]
    n_lat = n_lat_rows // TE
    n_all = (n_lat_rows + n_ctx_rows) // TE
    tbl = pl.BlockSpec((1, 1, d), lambda i: ((i * TE) // TM, 0, 0))
    row = pl.BlockSpec((TE, d), lambda i: (i, 0))
    return pl.pallas_call(
        functools.partial(_init_kernel, n_lat=n_lat),
        grid=(n_all,),
        in_specs=[pl.BlockSpec((TE, d), lambda i: (jnp.minimum(i, n_lat - 1), 0)),
                  pl.BlockSpec((TE, d), lambda i: (jnp.maximum(i - n_lat, 0), 0)),
                  tbl, tbl],
        out_specs=[row, row],
        out_shape=[jax.ShapeDtypeStruct((n_lat_rows + n_ctx_rows, d), F32),
                   jax.ShapeDtypeStruct((n_lat_rows + n_ctx_rows, d), BF16)],
        compiler_params=_cp(("arbitrary",), 40),
        name="init_norm",
    )(x2, ctx2, sc_tbl, sh_tbl)


def _ln_kernel(x_ref, d_ref, gate_ref, lng_ref, lnb_ref, *rest, alpha, with_h):
    z = alpha * x_ref[...] + gate_ref[0] * d_ref[...].astype(F32)
    xn = _standardise(z) * lng_ref[...] + lnb_ref[...]
    if with_h:
        sc_ref, sh_ref, xo_ref, ho_ref = rest
        xo_ref[...] = xn
        ho_ref[...] = (xn * sc_ref[0] + sh_ref[0]).astype(BF16)
    else:
        (xo_ref,) = rest
        xo_ref[...] = xn


def _post_ln(x, delta, gate_tbl, ln_g, ln_b, sc_tbl, sh_tbl, *, n_rows, alpha, out_rows):
    d = x.shape[1]
    with_h = sc_tbl is not None
    tbl = pl.BlockSpec((1, 1, d), lambda i: ((i * TE) // TM, 0, 0))
    row = pl.BlockSpec((TE, d), lambda i: (i, 0))
    vec = pl.BlockSpec((1, d), lambda i: (0, 0))
    in_specs = [row, row, tbl, vec, vec]
    args = [x, delta, gate_tbl, ln_g.reshape(1, d), ln_b.reshape(1, d)]
    out_specs = [row]
    out_shape = [jax.ShapeDtypeStruct((out_rows, d), F32)]
    if with_h:
        in_specs += [tbl, tbl]
        args += [sc_tbl, sh_tbl]
        out_specs.append(row)
        out_shape.append(jax.ShapeDtypeStruct((out_rows, d), BF16))
    res = pl.pallas_call(
        functools.partial(_ln_kernel, alpha=alpha, with_h=with_h),
        grid=(n_rows // TE,),
        in_specs=in_specs, out_specs=out_specs, out_shape=out_shape,
        compiler_params=_cp(("arbitrary",), 40),
        name="post_ln",
    )(*args)
    return res if with_h else (res[0], None)


def _swap_select(x, near, far, bit):
    lane = lax.broadcasted_iota(jnp.int32, x.shape, 1)
    return jnp.where((lane & bit) == 0, pltpu.roll(x, far, 1), pltpu.roll(x, near, 1))


def _rope64(x, cos, sin):
    return x * cos + _swap_select(x, 16, LANE - 16, 16) * sin


def _rms(x, g):
    return x * lax.rsqrt(jnp.mean(x * x, axis=-1, keepdims=True) + NORM_EPS) * g


def _proj_kernel(h_ref, wa_ref, wb_ref, tab_ref, aqk_ref, mqkv_ref, up_ref, g_ref):
    j = pl.program_id(1)
    piece = h_ref.shape[0] // PROJ_ROW_SPLIT
    blk = lambda col: col // PROJ_TN

    def for_row_pieces(fn):
        for r in range(PROJ_ROW_SPLIT):
            fn(slice(r * piece, (r + 1) * piece))

    def block(pred, epilogue, width=PROJ_TN):
        @pl.when(pred)
        def _():
            w = wa_ref[:width, :].astype(BF16)

            def one(rows):
                u = _dot_nt(h_ref[rows, :], w)
                cols = [u[:, c * LANE:(c + 1) * LANE] for c in range(width // LANE)]
                outs = epilogue(cols, rows)
                for c, val in enumerate(outs):
                    up_ref[rows, c * LANE:(c + 1) * LANE] = val.astype(BF16)

            for_row_pieces(one)

    def rope(x, kind, near, rows):
        lane = lax.broadcasted_iota(jnp.int32, x.shape, 1)
        swapped = jnp.where((lane & near) == 0, pltpu.roll(x, LANE - near, 1), pltpu.roll(x, near, 1))
        return x * tab_ref[2 * kind, rows, :] + swapped * tab_ref[2 * kind + 1, rows, :]

    def a_heads(cols, rows):
        is_q = j == blk(COL_AQ)
        gain = aqk_ref[pl.ds(j - blk(COL_AQ), 1), :]
        scale = jnp.where(is_q, LOG2E * HEAD_DIM ** -0.5, 1.0)
        outs = []
        for c, x in enumerate(cols):
            y = rope(_rms(x, gain), TAB_ROPE128, 32, rows) * scale
            outs.append(y if c < KV_HEADS else jnp.where(is_q, y, x))
        return outs

    block((j == blk(COL_AQ)) | (j == blk(COL_AK)), a_heads)

    def rope_heads(cols, rows):
        wide = j < blk(COL_CQ)
        near = jnp.where(wide, 32, 16)
        kind = jnp.where(wide, TAB_ROPE128, TAB_ROPE64)
        scale = jnp.where(j == blk(COL_BQ), LOG2E * HEAD_DIM ** -0.5,
                          jnp.where(j == blk(COL_CQ), LOG2E * C_QK_DIM ** -0.5, 1.0))
        all_plain = j == blk(COL_CV)
        tail_plain = all_plain | (j == blk(COL_BK))
        outs = []
        for c, x in enumerate(cols):
            plain = all_plain if c < KV_HEADS else tail_plain
            outs.append(jnp.where(plain, x, rope(x, kind, near, rows) * scale))
        return outs

    block((j >= blk(COL_BQ)) & (j <= blk(COL_CV)), rope_heads)

    def latent_norm(cols, rows):
        gain = mqkv_ref[pl.ds(j - blk(COL_DQA), 1), :]
        ms = sum(jnp.sum(x * x, axis=-1, keepdims=True) for x in cols) * (1.0 / (len(cols) * LANE))
        inv = lax.rsqrt(ms + NORM_EPS)
        return [x * inv * gain[:, c * LANE:(c + 1) * LANE] for c, x in enumerate(cols)]

    block((j == blk(COL_DQA)) | (j == blk(COL_DKVA)), latent_norm)

    def rope_key(cols, rows):
        x = cols[0]
        lane = lax.broadcasted_iota(jnp.int32, x.shape, 1)
        return [jnp.where(lane < MLA_ROPE, rope(x, TAB_ROPE64, 16, rows), 0.0)]

    block(j == blk(COL_DKR), rope_key, width=LANE)

    @pl.when(j == blk(COL_DKR))
    def _():
        up_ref[:, LANE:] = jnp.zeros((up_ref.shape[0], PROJ_TN - LANE), BF16)

    @pl.when(j >= N_UP_BLOCKS)
    def _():
        w = jnp.concatenate([wa_ref[GATE_ROW_OFF:, :], wb_ref[...]], axis=0).astype(BF16)

        def one(rows):
            g = _dot_nt(h_ref[rows, :], w)
            g_ref[rows, :] = (0.5 * jnp.tanh(0.5 * g) + 0.5).astype(BF16)

        for_row_pieces(one)


def _row_tile(n, choices):
    return next(t for t in choices if n % t == 0)


def _proj(h, w_in_t, tabs, aqk, mq, mkv, *, layer):
    n, d = h.shape
    tp = _row_tile(n, PROJ_ROW_TILES)
    nt = n // tp
    n_gate = w_in_t.shape[1] - QKV_W
    gate_blocks = n_gate // PROJ_TN
    first_gate_blk = (QKV_W - GATE_ROW_OFF) // PROJ_TN
    assert (QKV_W - GATE_ROW_OFF) % PROJ_TN == 0 and n_gate % PROJ_TN == 0 and first_gate_blk == N_UP_BLOCKS - 1

    def wa_idx(i, j):
        return (layer, jnp.where(j < N_UP_BLOCKS, j, j - 1), 0)

    def wb_idx(i, j):
        g = jnp.maximum(j - N_UP_BLOCKS, 0)
        return (layer, (QKV_W - GATE_ROW_OFF + (g + 1) * PROJ_TN) // GATE_ROW_OFF, 0)

    return pl.pallas_call(
        _proj_kernel,
        grid=(nt, N_UP_BLOCKS + gate_blocks),
        in_specs=[pl.BlockSpec((tp, d), lambda i, j: (i, 0)),
                  pl.BlockSpec((None, PROJ_TN, d), wa_idx),
                  pl.BlockSpec((None, GATE_ROW_OFF, d), wb_idx),
                  pl.BlockSpec((N_TAB_PLANES, tp, LANE), lambda i, j: (0, i, 0)),
                  pl.BlockSpec((2, LANE), lambda i, j: (0, 0)),
                  pl.BlockSpec((2, MLA_RANK), lambda i, j: (0, 0))],
        out_specs=[pl.BlockSpec((tp, PROJ_TN), lambda i, j: (i, jnp.minimum(j, N_UP_BLOCKS - 1))),
                   pl.BlockSpec((tp, PROJ_TN), lambda i, j: (i, jnp.maximum(j - N_UP_BLOCKS, 0)))],
        out_shape=[jax.ShapeDtypeStruct((n, UP_W), BF16),
                   jax.ShapeDtypeStruct((n, n_gate), BF16)],
        compiler_params=_cp(("arbitrary", "arbitrary"), 58),
        name="proj",
    )(h, w_in_t, w_in_t, tabs, aqk, jnp.stack([mq, mkv]))


def _mla_up_kernel(qa_ref, kva_ref, kr_ref, wq_ref, wkv_ref, tab_ref, dq_ref, dk_ref, dv_ref):
    cos, sin = tab_ref[2 * TAB_ROPE64], tab_ref[2 * TAB_ROPE64 + 1]
    q = _dot(qa_ref[...], wq_ref[...].astype(BF16))
    kv = _dot(kva_ref[...], wkv_ref[...].astype(BF16))
    scale = LOG2E * (MLA_NOPE + MLA_ROPE) ** -0.5
    kr = kr_ref[...]
    for hh in range(N_HEADS):
        base = hh * MLA_QK_PAD
        dq_ref[:, base:base + LANE] = (q[:, base:base + LANE] * scale).astype(BF16)
        dq_ref[:, base + LANE:base + 2 * LANE] = (
            _rope64(q[:, base + LANE:base + 2 * LANE], cos, sin) * scale).astype(BF16)
        dk_ref[:, base:base + LANE] = kv[:, hh * LANE:(hh + 1) * LANE].astype(BF16)
        dk_ref[:, base + LANE:base + 2 * LANE] = kr
    dv_ref[...] = kv[:, N_HEADS * LANE:].astype(BF16)


def _mla_up(up, wq_pad, wkv_perm, tabs, *, layer):
    n = up.shape[0]
    nt = n // TM
    wide = N_HEADS * MLA_QK_PAD
    return pl.pallas_call(
        _mla_up_kernel,
        grid=(nt,),
        in_specs=[pl.BlockSpec((TM, MLA_RANK), lambda i: (i, COL_DQA // MLA_RANK)),
                  pl.BlockSpec((TM, MLA_RANK), lambda i: (i, COL_DKVA // MLA_RANK)),
                  pl.BlockSpec((TM, LANE), lambda i: (i, COL_DKR // LANE)),
                  pl.BlockSpec((None, MLA_RANK, wide), lambda i: (layer, 0, 0)),
                  pl.BlockSpec((None, MLA_RANK, wide), lambda i: (layer, 0, 0)),
                  pl.BlockSpec((N_TAB_PLANES, TM, LANE), lambda i: (0, i, 0))],
        out_specs=[pl.BlockSpec((TM, wide), lambda i: (i, 0)),
                   pl.BlockSpec((TM, wide), lambda i: (i, 0)),
                   pl.BlockSpec((TM, N_HEADS * LANE), lambda i: (i, 0))],
        out_shape=[jax.ShapeDtypeStruct((n, wide), BF16),
                   jax.ShapeDtypeStruct((n, wide), BF16),
                   jax.ShapeDtypeStruct((n, N_HEADS * LANE), BF16)],
        compiler_params=_cp(("arbitrary",), 48),
        name="mla_up",
    )(up, up, up, wq_pad, wkv_perm, tabs)


def _attend(q, pieces, extra=None):
    scores = []
    for k, _, mask in pieces:
        s = _dot_nt(q, k)
        if mask is not None:
            s = jnp.where(mask, s, NEG_INF)
        scores.append(s)
    m = functools.reduce(jnp.maximum, [jnp.max(s, axis=-1, keepdims=True) for s in scores])
    if extra is not None:
        m = jnp.maximum(m, extra)
    den = None
    out = None
    for s, (_, v, _) in zip(scores, pieces):
        e = jnp.exp2(s - m)
        part = jnp.sum(e, axis=-1, keepdims=True)
        pv = _dot(e.astype(BF16), v)
        den = part if den is None else den + part
        out = pv if out is None else out + pv
    if extra is not None:
        den = den + jnp.exp2(extra - m)
    return out / den


def _heads(ref, k, width=LANE):
    return ref[:, k * width:(k + 1) * width]


def _query_steps(run, q_ref, o_ref, qc_ref, oc_ref, nq):
    t = pl.program_id(1)
    if qc_ref is None:
        run(q_ref, o_ref, True)
        return

    @pl.when(t < nq)
    def _():
        run(q_ref, o_ref, True)

    @pl.when(t >= nq)
    def _():
        run(qc_ref, oc_ref, False)


def _split_refs(refs, n_extra):
    with_ctx = len(refs) == 8 + n_extra
    q, rest = refs[0], refs[1:]
    qc = None
    if with_ctx:
        qc, rest = rest[0], rest[1:]
    kl, vl, kc, vc = rest[:4]
    extras = rest[4:4 + n_extra]
    outs = rest[4 + n_extra:]
    return q, qc, kl, vl, kc, vc, extras, outs[0], (outs[1] if with_ctx else None)


def _attn_a_kernel(*refs, nq):
    q_ref, qc_ref, kl_ref, vl_ref, kc_ref, vc_ref, _, o_ref, oc_ref = _split_refs(refs, 0)

    def run(q_ref, o_ref, with_latent):
        split = ATTN_ROW_SPLIT if with_latent else 1
        piece = q_ref.shape[0] // split
        for hh in range(N_HEADS):
            kvh = hh // (N_HEADS // KV_HEADS)
            pieces = [(_heads(kc_ref, kvh), _heads(vc_ref, kvh), None)]
            if with_latent:
                pieces.append((_heads(kl_ref, kvh), _heads(vl_ref, kvh), None))
            for r in range(split):
                rows = slice(r * piece, (r + 1) * piece)
                q = q_ref[rows, hh * LANE:(hh + 1) * LANE]
                o_ref[rows, hh * LANE:(hh + 1) * LANE] = _attend(q, pieces).astype(BF16)

    _query_steps(run, q_ref, o_ref, qc_ref, oc_ref, nq)


def _attn_b_kernel(*refs, nq, seq, win_k):
    q_ref, qc_ref, kl_ref, vl_ref, kc_ref, vc_ref, (sink_ref,), o_ref, oc_ref = _split_refs(refs, 1)
    t = pl.program_id(1)

    def run(q_ref, o_ref, with_latent):
        rows = q_ref.shape[0]
        if with_latent:
            q0 = t * rows
            start = pl.multiple_of(jnp.clip(q0 - WINDOW, 0, seq - win_k), LANE)
            r = lax.broadcasted_iota(jnp.int32, (rows, win_k), 0)
            c = lax.broadcasted_iota(jnp.int32, (rows, win_k), 1)
            mask = jnp.abs(q0 + r - (start + c)) <= WINDOW
        for hh in range(N_HEADS):
            kvh = hh // (N_HEADS // KV_HEADS)
            sink = LOG2E * sink_ref[0:1, hh:hh + 1]
            pieces = [(_heads(kc_ref, kvh), _heads(vc_ref, kvh), None)]
            if with_latent:
                kw = kl_ref[pl.ds(start, win_k), kvh * LANE:(kvh + 1) * LANE]
                vw = vl_ref[pl.ds(start, win_k), kvh * LANE:(kvh + 1) * LANE]
                pieces.append((kw, vw, mask))
            o_ref[:, hh * LANE:(hh + 1) * LANE] = _attend(_heads(q_ref, hh), pieces, extra=sink).astype(BF16)

    _query_steps(run, q_ref, o_ref, qc_ref, oc_ref, nq)


def _attn_c_kernel(*refs, nq, lam_init):
    q_ref, qc_ref, kl_ref, vl_ref, kc_ref, vc_ref, (lam_ref, g_ref), o_ref, oc_ref = _split_refs(refs, 2)
    lam = (jnp.exp(jnp.sum(lam_ref[0:1, :] * lam_ref[1:2, :], axis=-1, keepdims=True))
           - jnp.exp(jnp.sum(lam_ref[2:3, :] * lam_ref[3:4, :], axis=-1, keepdims=True)) + lam_init)

    def run(q_ref, o_ref, with_latent):
        rows = q_ref.shape[0]
        lane = lax.broadcasted_iota(jnp.int32, (rows, LANE), 1)
        for hh in range(N_HEADS):
            q = _heads(q_ref, hh)
            zero = jnp.zeros_like(q)
            pieces = [(_heads(kc_ref, hh), _heads(vc_ref, hh), None)]
            if with_latent:
                pieces.append((_heads(kl_ref, hh), _heads(vl_ref, hh), None))
            o1 = _attend(jnp.where(lane < C_QK_DIM, q, zero), pieces)
            o2 = _attend(jnp.where(lane >= C_QK_DIM, q, zero), pieces)
            o = _rms(o1 - lam * o2, g_ref[...]) * (1.0 - lam_init)
            o_ref[:, hh * LANE:(hh + 1) * LANE] = o.astype(BF16)

    _query_steps(run, q_ref, o_ref, qc_ref, oc_ref, nq)


def _attn_d_kernel(*refs, nq):
    q_ref, qc_ref, kl_ref, vl_ref, kc_ref, vc_ref, _, o_ref, oc_ref = _split_refs(refs, 0)

    def run(q_ref, o_ref, with_latent):
        split = ATTN_ROW_SPLIT if with_latent else 1
        piece = q_ref.shape[0] // split
        for hh in range(N_HEADS):
            pieces = [(_heads(kc_ref, hh, MLA_QK_PAD), _heads(vc_ref, hh), None)]
            if with_latent:
                pieces.append((_heads(kl_ref, hh, MLA_QK_PAD), _heads(vl_ref, hh), None))
            for r in range(split):
                rows = slice(r * piece, (r + 1) * piece)
                q = q_ref[rows, hh * MLA_QK_PAD:(hh + 1) * MLA_QK_PAD]
                o_ref[rows, hh * LANE:(hh + 1) * LANE] = _attend(q, pieces).astype(BF16)

    _query_steps(run, q_ref, o_ref, qc_ref, oc_ref, nq)


def _attention(kern, name, q_src, k_src, v_src, extras, *, batch, seq, ctx, with_ctx_queries,
               q_w, q_blk, k_w, k_blk, v_w, v_blk):
    nq = seq // TQ
    steps = nq + (1 if with_ctx_queries else 0)
    ctx_blk0 = batch * seq // ctx

    def qrow(b, t):
        return b * nq + jnp.minimum(t, nq - 1)

    in_specs = [pl.BlockSpec((TQ, q_w), lambda b, t: (qrow(b, t), q_blk))]
    args = [q_src]
    if with_ctx_queries:
        in_specs.append(pl.BlockSpec((ctx, q_w), lambda b, t: (ctx_blk0 + b, q_blk)))
        args.append(q_src)
    in_specs += [pl.BlockSpec((seq, k_w), lambda b, t: (b, k_blk)),
                 pl.BlockSpec((seq, v_w), lambda b, t: (b, v_blk)),
                 pl.BlockSpec((ctx, k_w), lambda b, t: (ctx_blk0 + b, k_blk)),
                 pl.BlockSpec((ctx, v_w), lambda b, t: (ctx_blk0 + b, v_blk))]
    args += [k_src, v_src, k_src, v_src]
    for e in extras:
        in_specs.append(pl.BlockSpec(e.shape, lambda b, t: (0, 0)))
        args.append(e)
    out_specs = [pl.BlockSpec((TQ, BRANCH_W), lambda b, t: (qrow(b, t), 0))]
    out_shape = [jax.ShapeDtypeStruct((batch * seq, BRANCH_W), BF16)]
    if with_ctx_queries:
        out_specs.append(pl.BlockSpec((ctx, BRANCH_W), lambda b, t: (b, 0)))
        out_shape.append(jax.ShapeDtypeStruct((batch * ctx, BRANCH_W), BF16))
    res = pl.pallas_call(
        functools.partial(kern, nq=nq),
        grid=(batch, steps),
        in_specs=in_specs, out_specs=out_specs, out_shape=out_shape,
        compiler_params=_cp(("arbitrary", "arbitrary"), 56),
        name=name,
    )(*args)
    return (res[0], res[1]) if with_ctx_queries else (res[0], None)


def _merge_kernel(*refs, n_lat_tiles, with_ctx):
    n_src = 2 * N_BRANCH if with_ctx else N_BRANCH
    lat_refs, ctx_refs = refs[:N_BRANCH], refs[N_BRANCH:n_src]
    g_ref, w_ref, y_ref = refs[n_src:]
    d = y_ref.shape[1]

    def body(o_refs):
        for c in range(d // MERGE_TN):
            cols = slice(c * MERGE_TN, (c + 1) * MERGE_TN)
            acc = None
            for m, o_ref in enumerate(o_refs):
                gate = g_ref[:, m * d + c * MERGE_TN:m * d + (c + 1) * MERGE_TN].astype(F32)
                p = gate * _dot(o_ref[...], w_ref[m, :, cols])
                acc = p if acc is None else acc + p
            y_ref[:, cols] = acc.astype(BF16)

    if not with_ctx:
        body(lat_refs)
        return
    i = pl.program_id(0)

    @pl.when(i < n_lat_tiles)
    def _():
        body(lat_refs)

    @pl.when(i >= n_lat_tiles)
    def _():
        body(ctx_refs)


def _merge(lat_outs, ctx_outs, gates, w_branch_bf16, *, layer, n_lat_rows, n_rows):
    d = w_branch_bf16.shape[3]
    n_lat_tiles = n_lat_rows // TO
    with_ctx = ctx_outs is not None
    srcs = list(lat_outs) + (list(ctx_outs) if with_ctx else [])
    specs = [pl.BlockSpec((TO, BRANCH_W), lambda i: (jnp.minimum(i, n_lat_tiles - 1), 0))] * N_BRANCH
    if with_ctx:
        specs += [pl.BlockSpec((TO, BRANCH_W), lambda i: (jnp.maximum(i - n_lat_tiles, 0), 0))] * N_BRANCH
    return pl.pallas_call(
        functools.partial(_merge_kernel, n_lat_tiles=n_lat_tiles, with_ctx=with_ctx),
        grid=(n_rows // TO,),
        in_specs=specs + [pl.BlockSpec((TO, N_BRANCH * d), lambda i: (i, 0)),
                          pl.BlockSpec((None, N_BRANCH, BRANCH_W, d), lambda i: (layer, 0, 0, 0))],
        out_specs=pl.BlockSpec((TO, d), lambda i: (i, 0)),
        out_shape=jax.ShapeDtypeStruct((n_rows, d), BF16),
        compiler_params=_cp(("arbitrary",), 56),
        name="merge",
    )(*srcs, gates, w_branch_bf16)


def _out_ln_kernel(y_ref, w_ref, x_ref, gate_ref, lng_ref, lnb_ref, sc_ref, sh_ref, xo_ref, ho_ref, *, alpha):
    piece = y_ref.shape[0] // 2
    for r in range(2):
        rows = slice(r * piece, (r + 1) * piece)
        mix = _dot(y_ref[rows, :], w_ref[...])
        z = alpha * x_ref[rows, :] + gate_ref[0] * mix
        xn = _standardise(z) * lng_ref[...] + lnb_ref[...]
        xo_ref[rows, :] = xn
        ho_ref[rows, :] = (xn * sc_ref[0] + sh_ref[0]).astype(BF16)


def _out_ln(y, w_out_bf16, x, gate_tbl, ln_g, ln_b, sc_tbl, sh_tbl, *, layer, n_rows, alpha):
    d = x.shape[1]
    tbl = pl.BlockSpec((1, 1, d), lambda i: ((i * TO) // TM, 0, 0))
    row = pl.BlockSpec((TO, d), lambda i: (i, 0))
    vec = pl.BlockSpec((1, d), lambda i: (0, 0))
    return pl.pallas_call(
        functools.partial(_out_ln_kernel, alpha=alpha),
        grid=(n_rows // TO,),
        in_specs=[row, pl.BlockSpec((None, d, d), lambda i: (layer, 0, 0)), row, tbl, vec, vec, tbl, tbl],
        out_specs=[row, row],
        out_shape=[jax.ShapeDtypeStruct((n_rows, d), F32), jax.ShapeDtypeStruct((n_rows, d), BF16)],
        compiler_params=_cp(("arbitrary",), 56),
        name="out_ln",
    )(y, w_out_bf16, x, gate_tbl, ln_g.reshape(1, d), ln_b.reshape(1, d), sc_tbl, sh_tbl)


def _ffn_kernel(h_ref, wg_ref, wu_ref, wo_ref, o_ref, acc_ref):
    f = pl.program_id(1)

    @pl.when(f == 0)
    def _():
        acc_ref[...] = jnp.zeros_like(acc_ref)

    h = h_ref[...]
    g = _dot(h, wg_ref[...].astype(BF16))
    u = _dot(h, wu_ref[...].astype(BF16))
    a = (g * (0.5 * jnp.tanh(0.5 * g) + 0.5) * u).astype(BF16)

    for n in range(acc_ref.shape[1] // FFN_ACC_TN):
        cols = slice(n * FFN_ACC_TN, (n + 1) * FFN_ACC_TN)
        acc_ref[:, cols] += _dot(a, wo_ref[:, cols].astype(BF16))

    @pl.when(f == pl.num_programs(1) - 1)
    def _():
        o_ref[...] = acc_ref[...].astype(BF16)


def _ffn(h, w_ffn_in, w_ffn_out, *, layer, n_tiles):
    d = h.shape[1]
    n = n_tiles * TM
    hidden = w_ffn_out.shape[1]
    tf = 512
    nf = hidden // tf
    return pl.pallas_call(
        _ffn_kernel,
        grid=(n_tiles, nf),
        in_specs=[pl.BlockSpec((TM, d), lambda i, f: (i, 0)),
                  pl.BlockSpec((None, d, tf), lambda i, f: (layer, 0, f)),
                  pl.BlockSpec((None, d, tf), lambda i, f: (layer, 0, nf + f)),
                  pl.BlockSpec((None, tf, d), lambda i, f: (layer, f, 0))],
        out_specs=pl.BlockSpec((TM, d), lambda i, f: (i, 0)),
        out_shape=jax.ShapeDtypeStruct((n, d), BF16),
        scratch_shapes=[pltpu.VMEM((TM, d), F32)],
        compiler_params=_cp(("arbitrary", "arbitrary"), 60),
        name="ffn",
    )(h, w_ffn_in, w_ffn_in, w_ffn_out)


def _rope_tables(seq, batch, ctx_rows):
    t = jnp.arange(seq, dtype=jnp.int32)
    pos_row = (t // GRID_W).astype(F32)[:, None]
    pos_col = (t % GRID_W).astype(F32)[:, None]

    def one(rot_dim, reps):
        axis_dim = rot_dim // 2
        freqs = ROPE_THETA ** (-jnp.arange(0, axis_dim, 2, dtype=F32) / axis_dim)
        ar, ac = pos_row * freqs[None, :], pos_col * freqs[None, :]
        cos = jnp.concatenate([jnp.cos(ar), jnp.cos(ar), jnp.cos(ac), jnp.cos(ac)], axis=-1)
        sin = jnp.concatenate([-jnp.sin(ar), jnp.sin(ar), -jnp.sin(ac), jnp.sin(ac)], axis=-1)
        return jnp.tile(cos, (1, reps)), jnp.tile(sin, (1, reps))

    c128, s128 = one(HEAD_DIM, 1)
    c64, s64 = one(C_QK_DIM, 2)
    ones = jnp.ones((ctx_rows, LANE), F32)
    zeros = jnp.zeros((ctx_rows, LANE), F32)

    def rows(tab, pad):
        return jnp.concatenate([jnp.tile(tab, (batch, 1)), pad])

    return jnp.stack([rows(c128, ones), rows(s128, zeros), rows(c64, ones), rows(s64, zeros)])


def kernel(x, c, ctx, c_ctx, w_ada, b_ada, w_in, a_qk_norm, b_sink, c_lambda, c_subln, mla_q_norm, mla_w_q_up, mla_kv_norm, mla_w_kv_up, w_branch, w_out, ln_mix_g, ln_mix_b, w_ffn_in, w_ffn_out, ln_ffn_g, ln_ffn_b):
    batch, seq, d = x.shape
    ctx_len = ctx.shape[1]
    depth = w_ada.shape[0]
    assert d == D_MODEL and seq % TM == 0 and (batch * ctx_len) % TM == 0 and batch + 1 <= 8
    n_lat_rows = batch * seq
    n_rows = n_lat_rows + batch * ctx_len
    tiles_per_seq = seq // TM
    n_lat_tiles = n_lat_rows // TM
    n_tiles = n_rows // TM
    alpha = (2 * depth) ** 0.25

    c_all = jnp.concatenate([c, c_ctx[None, :], jnp.zeros((8 - batch - 1, d), F32)], axis=0)
    ada = _ada_all(c_all, w_ada, b_ada)
    tile_row = np.array([b for b in range(batch) for _ in range(tiles_per_seq)]
                        + [batch] * (n_tiles - n_lat_tiles), np.int32)

    def table(l, chunk, plus_one=False):
        v = ada[l, :, chunk * d:(chunk + 1) * d][tile_row]
        return ((1.0 + v) if plus_one else v)[:, None, :]

    tabs = _rope_tables(seq, batch, batch * ctx_len)
    x_all, h = _init(x.reshape(n_lat_rows, d), ctx.reshape(batch * ctx_len, d),
                     table(0, 1, True), table(0, 0))

    win_k = min(TQ + 2 * WINDOW, seq)
    sink_pad = jnp.zeros((depth, 8, LANE), F32).at[:, 0, :N_HEADS].set(b_sink)
    wq_pad = jnp.pad(mla_w_q_up.reshape(depth, MLA_RANK, N_HEADS, MLA_NOPE + MLA_ROPE),
                     ((0, 0), (0, 0), (0, 0), (0, MLA_QK_PAD - MLA_NOPE - MLA_ROPE))
                     ).reshape(depth, MLA_RANK, N_HEADS * MLA_QK_PAD)
    wkv_pad = jnp.pad(mla_w_kv_up.reshape(depth, MLA_RANK, N_HEADS, 2, LANE).transpose(0, 1, 3, 2, 4)
                      .reshape(depth, MLA_RANK, 2 * N_HEADS * LANE),
                      ((0, 0), (0, 0), (0, N_HEADS * MLA_QK_PAD - 2 * N_HEADS * LANE)))

    w_in_t = jnp.swapaxes(w_in, 1, 2)
    w_out_bf16 = w_out.astype(BF16)
    w_branch_bf16 = w_branch.astype(BF16)

    for l in range(depth):
        last = l == depth - 1
        act_tiles = n_lat_tiles if last else n_tiles
        act_rows = act_tiles * TM
        lam_init = 0.8 - 0.6 * math.exp(-0.3 * l)

        up, gates = _proj(h, w_in_t, tabs, a_qk_norm[l], mla_q_norm[l], mla_kv_norm[l], layer=l)
        dq, dk, dv = _mla_up(up, wq_pad, wkv_pad, tabs, layer=l)

        common = dict(batch=batch, seq=seq, ctx=ctx_len, with_ctx_queries=not last)
        o_a = _attention(_attn_a_kernel, "attn_a", up, up, up, [], **common,
                         q_w=512, q_blk=COL_AQ // 512, k_w=256, k_blk=COL_AK // 256, v_w=256, v_blk=COL_AV // 256)
        o_b = _attention(functools.partial(_attn_b_kernel, seq=seq, win_k=win_k), "attn_b", up, up, up,
                         [sink_pad[l]], **common,
                         q_w=512, q_blk=COL_BQ // 512, k_w=256, k_blk=COL_BK // 256, v_w=256, v_blk=COL_BV // 256)
        o_c = _attention(functools.partial(_attn_c_kernel, lam_init=lam_init), "attn_c", up, up, up,
                         [c_lambda[l], c_subln[l].reshape(1, LANE)], **common,
                         q_w=512, q_blk=COL_CQ // 512, k_w=512, k_blk=COL_CK // 512, v_w=512, v_blk=COL_CV // 512)
        o_d = _attention(_attn_d_kernel, "attn_d", dq, dk, dv, [], **common,
                         q_w=N_HEADS * MLA_QK_PAD, q_blk=0, k_w=N_HEADS * MLA_QK_PAD, k_blk=0,
                         v_w=N_HEADS * LANE, v_blk=0)

        lat_outs, ctx_outs = zip(o_a, o_b, o_c, o_d)
        y = _merge(lat_outs, None if last else ctx_outs, gates, w_branch_bf16,
                   layer=l, n_lat_rows=n_lat_rows, n_rows=act_rows)
        x_all, h2 = _out_ln(y, w_out_bf16, x_all, table(l, 2), ln_mix_g[l], ln_mix_b[l],
                            table(l, 4, True), table(l, 3), layer=l, n_rows=act_rows, alpha=alpha)
        ffn = _ffn(h2, w_ffn_in, w_ffn_out, layer=l, n_tiles=act_tiles)
        if last:
            x_all, _ = _post_ln(x_all, ffn, table(l, 5), ln_ffn_g[l], ln_ffn_b[l], None, None,
                                n_rows=act_rows, alpha=alpha, out_rows=n_lat_rows)
        else:
            x_all, h = _post_ln(x_all, ffn, table(l, 5), ln_ffn_g[l], ln_ffn_b[l],
                                table(l + 1, 1, True), table(l + 1, 0),
                                n_rows=act_rows, alpha=alpha, out_rows=n_rows)
    return x_all.reshape(batch, seq, d)
```

```python
import functools
import math

import numpy as np
import jax
import jax.numpy as jnp
from jax import lax
from jax.experimental import pallas as pl
from jax.experimental.pallas import tpu as pltpu

F32 = jnp.float32
BF16 = jnp.bfloat16

D_MODEL = 2048
GRID_W = 64
HEAD_DIM = 128
WINDOW = 128
ROPE_THETA = 10000.0
NORM_EPS = 1e-6
NEG_INF = -1e30
LOG2E = math.log2(math.e)
N_HEADS = 4
KV_HEADS = 2
C_QK_DIM = 64
MLA_RANK = 512
MLA_NOPE = 128
MLA_ROPE = 64
MLA_QK_PAD = 256
BRANCH_W = 512
N_BRANCH = 4
QKV_W = 4672
LANE = 128
GATE_ROW_OFF = QKV_W % LANE

UP_W = 5120
PROJ_TN = 512
N_UP_BLOCKS = UP_W // PROJ_TN
PROJ_ROW_SPLIT = 4
COL_AQ, COL_AK, COL_AV = 0, 512, 768
COL_BQ, COL_BK, COL_BV = 1024, 1536, 1792
COL_CQ, COL_CK, COL_CV = 2048, 2560, 3072
COL_DQA, COL_DKVA, COL_DKR = 3584, 4096, 4608

TM = 1024
PROJ_ROW_TILES = (1536, 1024, 512)
TAB_ROPE128, TAB_ROPE64 = 0, 1
N_TAB_PLANES = 4
TO = 512
MERGE_TN = 512
TE = 512
TQ = 512
TQ_TALL = 1024
FFN_ACC_TN = 256
MIB = 1024 * 1024


def _cp(sem, vmem_mib):
    return pltpu.CompilerParams(dimension_semantics=sem, vmem_limit_bytes=vmem_mib * MIB)


def _dot(a, b):
    return jnp.dot(a, b, preferred_element_type=F32)


def _dot_nt(a, b):
    return lax.dot_general(a, b, (((1,), (1,)), ((), ())), preferred_element_type=F32)


def _ada_kernel(c_ref, w_ref, b_ref, o_ref):
    k = pl.program_id(1)
    c = c_ref[...]
    a = (c * (0.5 * jnp.tanh(0.5 * c) + 0.5)).astype(BF16)
    part = _dot(a, w_ref[...].astype(BF16))

    @pl.when(k == 0)
    def _():
        o_ref[...] = part + b_ref[...]

    @pl.when(k > 0)
    def _():
        o_ref[...] += part


def _ada_all(c_all, w_ada, b_ada):
    depth, d, n6 = w_ada.shape
    tk = 128
    return pl.pallas_call(
        _ada_kernel,
        grid=(depth, d // tk),
        in_specs=[pl.BlockSpec((8, tk), lambda l, k: (0, k)),
                  pl.BlockSpec((None, tk, n6), lambda l, k: (l, k, 0)),
                  pl.BlockSpec((None, 1, n6), lambda l, k: (l, 0, 0))],
        out_specs=pl.BlockSpec((None, 8, n6), lambda l, k: (l, 0, 0)),
        out_shape=jax.ShapeDtypeStruct((depth, 8, n6), F32),
        compiler_params=_cp(("arbitrary", "arbitrary"), 40),
        name="ada",
    )(c_all, w_ada, b_ada.reshape(depth, 1, n6))


def _standardise(z):
    mu = jnp.mean(z, axis=-1, keepdims=True)
    zc = z - mu
    var = jnp.mean(zc * zc, axis=-1, keepdims=True)
    return zc * lax.rsqrt(var + NORM_EPS)


def _init_kernel(x_ref, ctx_ref, sc_ref, sh_ref, xo_ref, ho_ref, *, n_lat):
    i = pl.program_id(0)

    def body(src):
        xs = _standardise(src[...])
        xo_ref[...] = xs
        ho_ref[...] = (xs * sc_ref[0] + sh_ref[0]).astype(BF16)

    @pl.when(i < n_lat)
    def _():
        body(x_ref)

    @pl.when(i >= n_lat)
    def _():
        body(ctx_ref)


def _init(x2, ctx2, sc_tbl, sh_tbl):
    n_lat_rows, d = x2.shape
    n_ctx_rows = ctx2.shape[0]
    n_lat = n_lat_rows // TE
    n_all = (n_lat_rows + n_ctx_rows) // TE
    tbl = pl.BlockSpec((1, 1, d), lambda i: ((i * TE) // TM, 0, 0))
    row = pl.BlockSpec((TE, d), lambda i: (i, 0))
    return pl.pallas_call(
        functools.partial(_init_kernel, n_lat=n_lat),
        grid=(n_all,),
        in_specs=[pl.BlockSpec((TE, d), lambda i: (jnp.minimum(i, n_lat - 1), 0)),
                  pl.BlockSpec((TE, d), lambda i: (jnp.maximum(i - n_lat, 0), 0)),
                  tbl, tbl],
        out_specs=[row, row],
        out_shape=[jax.ShapeDtypeStruct((n_lat_rows + n_ctx_rows, d), F32),
                   jax.ShapeDtypeStruct((n_lat_rows + n_ctx_rows, d), BF16)],
        compiler_params=_cp(("arbitrary",), 40),
        name="init_norm",
    )(x2, ctx2, sc_tbl, sh_tbl)


def _ln_kernel(x_ref, d_ref, gate_ref, lng_ref, lnb_ref, *rest, alpha, with_h):
    z = alpha * x_ref[...] + gate_ref[0] * d_ref[...].astype(F32)
    xn = _standardise(z) * lng_ref[...] + lnb_ref[...]
    if with_h:
        sc_ref, sh_ref, xo_ref, ho_ref = rest
        xo_ref[...] = xn
        ho_ref[...] = (xn * sc_ref[0] + sh_ref[0]).astype(BF16)
    else:
        (xo_ref,) = rest
        xo_ref[...] = xn


def _post_ln(x, delta, gate_tbl, ln_g, ln_b, sc_tbl, sh_tbl, *, n_rows, alpha, out_rows):
    d = x.shape[1]
    with_h = sc_tbl is not None
    tbl = pl.BlockSpec((1, 1, d), lambda i: ((i * TE) // TM, 0, 0))
    row = pl.BlockSpec((TE, d), lambda i: (i, 0))
    vec = pl.BlockSpec((1, d), lambda i: (0, 0))
    in_specs = [row, row, tbl, vec, vec]
    args = [x, delta, gate_tbl, ln_g.reshape(1, d), ln_b.reshape(1, d)]
    out_specs = [row]
    out_shape = [jax.ShapeDtypeStruct((out_rows, d), F32)]
    if with_h:
        in_specs += [tbl, tbl]
        args += [sc_tbl, sh_tbl]
        out_specs.append(row)
        out_shape.append(jax.ShapeDtypeStruct((out_rows, d), BF16))
    res = pl.pallas_call(
        functools.partial(_ln_kernel, alpha=alpha, with_h=with_h),
        grid=(n_rows // TE,),
        in_specs=in_specs, out_specs=out_specs, out_shape=out_shape,
        compiler_params=_cp(("arbitrary",), 40),
        name="post_ln",
    )(*args)
    return res if with_h else (res[0], None)


def _swap_select(x, near, far, bit):
    lane = lax.broadcasted_iota(jnp.int32, x.shape, 1)
    return jnp.where((lane & bit) == 0, pltpu.roll(x, far, 1), pltpu.roll(x, near, 1))


def _rope64(x, cos, sin):
    return x * cos + _swap_select(x, 16, LANE - 16, 16) * sin


def _rms(x, g):
    return x * lax.rsqrt(jnp.mean(x * x, axis=-1, keepdims=True) + NORM_EPS) * g


def _proj_kernel(h_ref, wa_ref, wb_ref, tab_ref, aqk_ref, mqkv_ref, up_ref, g_ref):
    j = pl.program_id(1)
    piece = h_ref.shape[0] // PROJ_ROW_SPLIT
    blk = lambda col: col // PROJ_TN

    def for_row_pieces(fn):
        for r in range(PROJ_ROW_SPLIT):
            fn(slice(r * piece, (r + 1) * piece))

    def block(pred, epilogue, width=PROJ_TN):
        @pl.when(pred)
        def _():
            w = wa_ref[:width, :].astype(BF16)

            def one(rows):
                u = _dot_nt(h_ref[rows, :], w)
                cols = [u[:, c * LANE:(c + 1) * LANE] for c in range(width // LANE)]
                outs = epilogue(cols, rows)
                for c, val in enumerate(outs):
                    up_ref[rows, c * LANE:(c + 1) * LANE] = val.astype(BF16)

            for_row_pieces(one)

    def rope(x, kind, near, rows):
        lane = lax.broadcasted_iota(jnp.int32, x.shape, 1)
        swapped = jnp.where((lane & near) == 0, pltpu.roll(x, LANE - near, 1), pltpu.roll(x, near, 1))
        return x * tab_ref[2 * kind, rows, :] + swapped * tab_ref[2 * kind + 1, rows, :]

    def a_heads(cols, rows):
        is_q = j == blk(COL_AQ)
        gain = aqk_ref[pl.ds(j - blk(COL_AQ), 1), :]
        scale = jnp.where(is_q, LOG2E * HEAD_DIM ** -0.5, 1.0)
        outs = []
        for c, x in enumerate(cols):
            y = rope(_rms(x, gain), TAB_ROPE128, 32, rows) * scale
            outs.append(y if c < KV_HEADS else jnp.where(is_q, y, x))
        return outs

    block((j == blk(COL_AQ)) | (j == blk(COL_AK)), a_heads)

    def rope_heads(cols, rows):
        wide = j < blk(COL_CQ)
        near = jnp.where(wide, 32, 16)
        kind = jnp.where(wide, TAB_ROPE128, TAB_ROPE64)
        scale = jnp.where(j == blk(COL_BQ), LOG2E * HEAD_DIM ** -0.5,
                          jnp.where(j == blk(COL_CQ), LOG2E * C_QK_DIM ** -0.5, 1.0))
        all_plain = j == blk(COL_CV)
        tail_plain = all_plain | (j == blk(COL_BK))
        outs = []
        for c, x in enumerate(cols):
            plain = all_plain if c < KV_HEADS else tail_plain
            outs.append(jnp.where(plain, x, rope(x, kind, near, rows) * scale))
        return outs

    block((j >= blk(COL_BQ)) & (j <= blk(COL_CV)), rope_heads)

    def latent_norm(cols, rows):
        gain = mqkv_ref[pl.ds(j - blk(COL_DQA), 1), :]
        ms = sum(jnp.sum(x * x, axis=-1, keepdims=True) for x in cols) * (1.0 / (len(cols) * LANE))
        inv = lax.rsqrt(ms + NORM_EPS)
        return [x * inv * gain[:, c * LANE:(c + 1) * LANE] for c, x in enumerate(cols)]

    block((j == blk(COL_DQA)) | (j == blk(COL_DKVA)), latent_norm)

    def rope_key(cols, rows):
        x = cols[0]
        lane = lax.broadcasted_iota(jnp.int32, x.shape, 1)
        return [jnp.where(lane < MLA_ROPE, rope(x, TAB_ROPE64, 16, rows), 0.0)]

    block(j == blk(COL_DKR), rope_key, width=LANE)

    @pl.when(j == blk(COL_DKR))
    def _():
        up_ref[:, LANE:] = jnp.zeros((up_ref.shape[0], PROJ_TN - LANE), BF16)

    @pl.when(j >= N_UP_BLOCKS)
    def _():
        w = jnp.concatenate([wa_ref[GATE_ROW_OFF:, :], wb_ref[...]], axis=0).astype(BF16)

        def one(rows):
            g = _dot_nt(h_ref[rows, :], w)
            g_ref[rows, :] = (0.5 * jnp.tanh(0.5 * g) + 0.5).astype(BF16)

        for_row_pieces(one)


def _row_tile(n, choices):
    return next(t for t in choices if n % t == 0)


def _proj(h, w_in_t, tabs, aqk, mq, mkv, *, layer):
    n, d = h.shape
    tp = _row_tile(n, PROJ_ROW_TILES)
    nt = n // tp
    n_gate = w_in_t.shape[1] - QKV_W
    gate_blocks = n_gate // PROJ_TN
    first_gate_blk = (QKV_W - GATE_ROW_OFF) // PROJ_TN
    assert (QKV_W - GATE_ROW_OFF) % PROJ_TN == 0 and n_gate % PROJ_TN == 0 and first_gate_blk == N_UP_BLOCKS - 1

    def wa_idx(i, j):
        return (layer, jnp.where(j < N_UP_BLOCKS, j, j - 1), 0)

    def wb_idx(i, j):
        g = jnp.maximum(j - N_UP_BLOCKS, 0)
        return (layer, (QKV_W - GATE_ROW_OFF + (g + 1) * PROJ_TN) // GATE_ROW_OFF, 0)

    return pl.pallas_call(
        _proj_kernel,
        grid=(nt, N_UP_BLOCKS + gate_blocks),
        in_specs=[pl.BlockSpec((tp, d), lambda i, j: (i, 0)),
                  pl.BlockSpec((None, PROJ_TN, d), wa_idx),
                  pl.BlockSpec((None, GATE_ROW_OFF, d), wb_idx),
                  pl.BlockSpec((N_TAB_PLANES, tp, LANE), lambda i, j: (0, i, 0)),
                  pl.BlockSpec((2, LANE), lambda i, j: (0, 0)),
                  pl.BlockSpec((2, MLA_RANK), lambda i, j: (0, 0))],
        out_specs=[pl.BlockSpec((tp, PROJ_TN), lambda i, j: (i, jnp.minimum(j, N_UP_BLOCKS - 1))),
                   pl.BlockSpec((tp, PROJ_TN), lambda i, j: (i, jnp.maximum(j - N_UP_BLOCKS, 0)))],
        out_shape=[jax.ShapeDtypeStruct((n, UP_W), BF16),
                   jax.ShapeDtypeStruct((n, n_gate), BF16)],
        compiler_params=_cp(("arbitrary", "arbitrary"), 58),
        name="proj",
    )(h, w_in_t, w_in_t, tabs, aqk, jnp.stack([mq, mkv]))


def _mla_up_kernel(qa_ref, kva_ref, kr_ref, wq_ref, wkv_ref, tab_ref, dq_ref, dk_ref, dv_ref):
    cos, sin = tab_ref[2 * TAB_ROPE64], tab_ref[2 * TAB_ROPE64 + 1]
    q = _dot(qa_ref[...], wq_ref[...].astype(BF16))
    kv = _dot(kva_ref[...], wkv_ref[...].astype(BF16))
    scale = LOG2E * (MLA_NOPE + MLA_ROPE) ** -0.5
    kr = kr_ref[...]
    for hh in range(N_HEADS):
        base = hh * MLA_QK_PAD
        dq_ref[:, base:base + LANE] = (q[:, base:base + LANE] * scale).astype(BF16)
        dq_ref[:, base + LANE:base + 2 * LANE] = (
            _rope64(q[:, base + LANE:base + 2 * LANE], cos, sin) * scale).astype(BF16)
        dk_ref[:, base:base + LANE] = kv[:, hh * LANE:(hh + 1) * LANE].astype(BF16)
        dk_ref[:, base + LANE:base + 2 * LANE] = kr
    dv_ref[...] = kv[:, N_HEADS * LANE:].astype(BF16)


def _mla_up(up, wq_pad, wkv_perm, tabs, *, layer):
    n = up.shape[0]
    nt = n // TM
    wide = N_HEADS * MLA_QK_PAD
    return pl.pallas_call(
        _mla_up_kernel,
        grid=(nt,),
        in_specs=[pl.BlockSpec((TM, MLA_RANK), lambda i: (i, COL_DQA // MLA_RANK)),
                  pl.BlockSpec((TM, MLA_RANK), lambda i: (i, COL_DKVA // MLA_RANK)),
                  pl.BlockSpec((TM, LANE), lambda i: (i, COL_DKR // LANE)),
                  pl.BlockSpec((None, MLA_RANK, wide), lambda i: (layer, 0, 0)),
                  pl.BlockSpec((None, MLA_RANK, wide), lambda i: (layer, 0, 0)),
                  pl.BlockSpec((N_TAB_PLANES, TM, LANE), lambda i: (0, i, 0))],
        out_specs=[pl.BlockSpec((TM, wide), lambda i: (i, 0)),
                   pl.BlockSpec((TM, wide), lambda i: (i, 0)),
                   pl.BlockSpec((TM, N_HEADS * LANE), lambda i: (i, 0))],
        out_shape=[jax.ShapeDtypeStruct((n, wide), BF16),
                   jax.ShapeDtypeStruct((n, wide), BF16),
                   jax.ShapeDtypeStruct((n, N_HEADS * LANE), BF16)],
        compiler_params=_cp(("arbitrary",), 48),
        name="mla_up",
    )(up, up, up, wq_pad, wkv_perm, tabs)


def _attend(q, pieces, extra=None):
    scores = []
    for k, _, mask in pieces:
        s = _dot_nt(q, k)
        if mask is not None:
            s = jnp.where(mask, s, NEG_INF)
        scores.append(s)
    m = functools.reduce(jnp.maximum, [jnp.max(s, axis=-1, keepdims=True) for s in scores])
    if extra is not None:
        m = jnp.maximum(m, extra)
    den = None
    out = None
    for s, (_, v, _) in zip(scores, pieces):
        e = jnp.exp2(s - m)
        part = jnp.sum(e, axis=-1, keepdims=True)
        pv = _dot(e.astype(BF16), v)
        den = part if den is None else den + part
        out = pv if out is None else out + pv
    if extra is not None:
        den = den + jnp.exp2(extra - m)
    return out / den


def _heads(ref, k, width=LANE):
    return ref[:, k * width:(k + 1) * width]


def _query_steps(run, q_ref, o_ref, qc_ref, oc_ref, nq):
    t = pl.program_id(1)
    if qc_ref is None:
        run(q_ref, o_ref, True)
        return

    @pl.when(t < nq)
    def _():
        run(q_ref, o_ref, True)

    @pl.when(t >= nq)
    def _():
        run(qc_ref, oc_ref, False)


def _split_refs(refs, n_extra):
    with_ctx = len(refs) == 8 + n_extra
    q, rest = refs[0], refs[1:]
    qc = None
    if with_ctx:
        qc, rest = rest[0], rest[1:]
    kl, vl, kc, vc = rest[:4]
    extras = rest[4:4 + n_extra]
    outs = rest[4 + n_extra:]
    return q, qc, kl, vl, kc, vc, extras, outs[0], (outs[1] if with_ctx else None)


def _attn_a_kernel(*refs, nq):
    q_ref, qc_ref, kl_ref, vl_ref, kc_ref, vc_ref, _, o_ref, oc_ref = _split_refs(refs, 0)

    def run(q_ref, o_ref, with_latent):
        for hh in range(N_HEADS):
            kvh = hh // (N_HEADS // KV_HEADS)
            pieces = [(_heads(kc_ref, kvh), _heads(vc_ref, kvh), None)]
            if with_latent:
                pieces.append((_heads(kl_ref, kvh), _heads(vl_ref, kvh), None))
            o_ref[:, hh * LANE:(hh + 1) * LANE] = _attend(_heads(q_ref, hh), pieces).astype(BF16)

    _query_steps(run, q_ref, o_ref, qc_ref, oc_ref, nq)


def _attn_b_kernel(*refs, nq, seq, win_k):
    q_ref, qc_ref, kl_ref, vl_ref, kc_ref, vc_ref, (sink_ref,), o_ref, oc_ref = _split_refs(refs, 1)
    t = pl.program_id(1)

    def run(q_ref, o_ref, with_latent):
        rows = q_ref.shape[0]
        if with_latent:
            q0 = t * rows
            start = pl.multiple_of(jnp.clip(q0 - WINDOW, 0, seq - win_k), LANE)
            r = lax.broadcasted_iota(jnp.int32, (rows, win_k), 0)
            c = lax.broadcasted_iota(jnp.int32, (rows, win_k), 1)
            mask = jnp.abs(q0 + r - (start + c)) <= WINDOW
        for hh in range(N_HEADS):
            kvh = hh // (N_HEADS // KV_HEADS)
            sink = LOG2E * sink_ref[0:1, hh:hh + 1]
            pieces = [(_heads(kc_ref, kvh), _heads(vc_ref, kvh), None)]
            if with_latent:
                kw = kl_ref[pl.ds(start, win_k), kvh * LANE:(kvh + 1) * LANE]
                vw = vl_ref[pl.ds(start, win_k), kvh * LANE:(kvh + 1) * LANE]
                pieces.append((kw, vw, mask))
            o_ref[:, hh * LANE:(hh + 1) * LANE] = _attend(_heads(q_ref, hh), pieces, extra=sink).astype(BF16)

    _query_steps(run, q_ref, o_ref, qc_ref, oc_ref, nq)


def _attn_c_kernel(*refs, nq, lam_init):
    q_ref, qc_ref, kl_ref, vl_ref, kc_ref, vc_ref, (lam_ref, g_ref), o_ref, oc_ref = _split_refs(refs, 2)
    lam = (jnp.exp(jnp.sum(lam_ref[0:1, :] * lam_ref[1:2, :], axis=-1, keepdims=True))
           - jnp.exp(jnp.sum(lam_ref[2:3, :] * lam_ref[3:4, :], axis=-1, keepdims=True)) + lam_init)

    def run(q_ref, o_ref, with_latent):
        rows = q_ref.shape[0]
        lane = lax.broadcasted_iota(jnp.int32, (rows, LANE), 1)
        for hh in range(N_HEADS):
            q = _heads(q_ref, hh)
            zero = jnp.zeros_like(q)
            pieces = [(_heads(kc_ref, hh), _heads(vc_ref, hh), None)]
            if with_latent:
                pieces.append((_heads(kl_ref, hh), _heads(vl_ref, hh), None))
            o1 = _attend(jnp.where(lane < C_QK_DIM, q, zero), pieces)
            o2 = _attend(jnp.where(lane >= C_QK_DIM, q, zero), pieces)
            o = _rms(o1 - lam * o2, g_ref[...]) * (1.0 - lam_init)
            o_ref[:, hh * LANE:(hh + 1) * LANE] = o.astype(BF16)

    _query_steps(run, q_ref, o_ref, qc_ref, oc_ref, nq)


def _attn_d_kernel(*refs, nq):
    q_ref, qc_ref, kl_ref, vl_ref, kc_ref, vc_ref, _, o_ref, oc_ref = _split_refs(refs, 0)

    def run(q_ref, o_ref, with_latent):
        for hh in range(N_HEADS):
            pieces = [(_heads(kc_ref, hh, MLA_QK_PAD), _heads(vc_ref, hh), None)]
            if with_latent:
                pieces.append((_heads(kl_ref, hh, MLA_QK_PAD), _heads(vl_ref, hh), None))
            q = _heads(q_ref, hh, MLA_QK_PAD)
            o_ref[:, hh * LANE:(hh + 1) * LANE] = _attend(q, pieces).astype(BF16)

    _query_steps(run, q_ref, o_ref, qc_ref, oc_ref, nq)


def _attention(kern, name, q_src, k_src, v_src, extras, *, batch, seq, ctx, with_ctx_queries,
               q_w, q_blk, k_w, k_blk, v_w, v_blk, tq=TQ):
    tq = min(tq, seq)
    nq = seq // tq
    steps = nq + (1 if with_ctx_queries else 0)
    ctx_blk0 = batch * seq // ctx

    def qrow(b, t):
        return b * nq + jnp.minimum(t, nq - 1)

    in_specs = [pl.BlockSpec((tq, q_w), lambda b, t: (qrow(b, t), q_blk))]
    args = [q_src]
    if with_ctx_queries:
        in_specs.append(pl.BlockSpec((ctx, q_w), lambda b, t: (ctx_blk0 + b, q_blk)))
        args.append(q_src)
    in_specs += [pl.BlockSpec((seq, k_w), lambda b, t: (b, k_blk)),
                 pl.BlockSpec((seq, v_w), lambda b, t: (b, v_blk)),
                 pl.BlockSpec((ctx, k_w), lambda b, t: (ctx_blk0 + b, k_blk)),
                 pl.BlockSpec((ctx, v_w), lambda b, t: (ctx_blk0 + b, v_blk))]
    args += [k_src, v_src, k_src, v_src]
    for e in extras:
        in_specs.append(pl.BlockSpec(e.shape, lambda b, t: (0, 0)))
        args.append(e)
    out_specs = [pl.BlockSpec((tq, BRANCH_W), lambda b, t: (qrow(b, t), 0))]
    out_shape = [jax.ShapeDtypeStruct((batch * seq, BRANCH_W), BF16)]
    if with_ctx_queries:
        out_specs.append(pl.BlockSpec((ctx, BRANCH_W), lambda b, t: (b, 0)))
        out_shape.append(jax.ShapeDtypeStruct((batch * ctx, BRANCH_W), BF16))
    res = pl.pallas_call(
        functools.partial(kern, nq=nq),
        grid=(batch, steps),
        in_specs=in_specs, out_specs=out_specs, out_shape=out_shape,
        compiler_params=_cp(("arbitrary", "arbitrary"), 56),
        name=name,
    )(*args)
    return (res[0], res[1]) if with_ctx_queries else (res[0], None)


def _merge_kernel(*refs, n_lat_tiles, with_ctx):
    n_src = 2 * N_BRANCH if with_ctx else N_BRANCH
    lat_refs, ctx_refs = refs[:N_BRANCH], refs[N_BRANCH:n_src]
    g_ref, w_ref, y_ref = refs[n_src:]
    d = y_ref.shape[1]

    def body(o_refs):
        for c in range(d // MERGE_TN):
            cols = slice(c * MERGE_TN, (c + 1) * MERGE_TN)
            acc = None
            for m, o_ref in enumerate(o_refs):
                gate = g_ref[:, m * d + c * MERGE_TN:m * d + (c + 1) * MERGE_TN].astype(F32)
                p = gate * _dot(o_ref[...], w_ref[m, :, cols])
                acc = p if acc is None else acc + p
            y_ref[:, cols] = acc.astype(BF16)

    if not with_ctx:
        body(lat_refs)
        return
    i = pl.program_id(0)

    @pl.when(i < n_lat_tiles)
    def _():
        body(lat_refs)

    @pl.when(i >= n_lat_tiles)
    def _():
        body(ctx_refs)


def _merge(lat_outs, ctx_outs, gates, w_branch_bf16, *, layer, n_lat_rows, n_rows):
    d = w_branch_bf16.shape[3]
    n_lat_tiles = n_lat_rows // TO
    with_ctx = ctx_outs is not None
    srcs = list(lat_outs) + (list(ctx_outs) if with_ctx else [])
    specs = [pl.BlockSpec((TO, BRANCH_W), lambda i: (jnp.minimum(i, n_lat_tiles - 1), 0))] * N_BRANCH
    if with_ctx:
        specs += [pl.BlockSpec((TO, BRANCH_W), lambda i: (jnp.maximum(i - n_lat_tiles, 0), 0))] * N_BRANCH
    return pl.pallas_call(
        functools.partial(_merge_kernel, n_lat_tiles=n_lat_tiles, with_ctx=with_ctx),
        grid=(n_rows // TO,),
        in_specs=specs + [pl.BlockSpec((TO, N_BRANCH * d), lambda i: (i, 0)),
                          pl.BlockSpec((None, N_BRANCH, BRANCH_W, d), lambda i: (layer, 0, 0, 0))],
        out_specs=pl.BlockSpec((TO, d), lambda i: (i, 0)),
        out_shape=jax.ShapeDtypeStruct((n_rows, d), BF16),
        compiler_params=_cp(("arbitrary",), 56),
        name="merge",
    )(*srcs, gates, w_branch_bf16)


def _out_ln_kernel(y_ref, w_ref, x_ref, gate_ref, lng_ref, lnb_ref, sc_ref, sh_ref, xo_ref, ho_ref, *, alpha):
    piece = y_ref.shape[0] // 2
    for r in range(2):
        rows = slice(r * piece, (r + 1) * piece)
        mix = _dot(y_ref[rows, :], w_ref[...])
        z = alpha * x_ref[rows, :] + gate_ref[0] * mix
        xn = _standardise(z) * lng_ref[...] + lnb_ref[...]
        xo_ref[rows, :] = xn
        ho_ref[rows, :] = (xn * sc_ref[0] + sh_ref[0]).astype(BF16)


def _out_ln(y, w_out_bf16, x, gate_tbl, ln_g, ln_b, sc_tbl, sh_tbl, *, layer, n_rows, alpha):
    d = x.shape[1]
    tbl = pl.BlockSpec((1, 1, d), lambda i: ((i * TO) // TM, 0, 0))
    row = pl.BlockSpec((TO, d), lambda i: (i, 0))
    vec = pl.BlockSpec((1, d), lambda i: (0, 0))
    return pl.pallas_call(
        functools.partial(_out_ln_kernel, alpha=alpha),
        grid=(n_rows // TO,),
        in_specs=[row, pl.BlockSpec((None, d, d), lambda i: (layer, 0, 0)), row, tbl, vec, vec, tbl, tbl],
        out_specs=[row, row],
        out_shape=[jax.ShapeDtypeStruct((n_rows, d), F32), jax.ShapeDtypeStruct((n_rows, d), BF16)],
        compiler_params=_cp(("arbitrary",), 56),
        name="out_ln",
    )(y, w_out_bf16, x, gate_tbl, ln_g.reshape(1, d), ln_b.reshape(1, d), sc_tbl, sh_tbl)


def _ffn_kernel(h_ref, wg_ref, wu_ref, wo_ref, o_ref, acc_ref):
    f = pl.program_id(1)

    @pl.when(f == 0)
    def _():
        acc_ref[...] = jnp.zeros_like(acc_ref)

    h = h_ref[...]
    g = _dot(h, wg_ref[...].astype(BF16))
    u = _dot(h, wu_ref[...].astype(BF16))
    a = (g * (0.5 * jnp.tanh(0.5 * g) + 0.5) * u).astype(BF16)

    for n in range(acc_ref.shape[1] // FFN_ACC_TN):
        cols = slice(n * FFN_ACC_TN, (n + 1) * FFN_ACC_TN)
        acc_ref[:, cols] += _dot(a, wo_ref[:, cols].astype(BF16))

    @pl.when(f == pl.num_programs(1) - 1)
    def _():
        o_ref[...] = acc_ref[...].astype(BF16)


def _ffn(h, w_ffn_in, w_ffn_out, *, layer, n_tiles):
    d = h.shape[1]
    n = n_tiles * TM
    hidden = w_ffn_out.shape[1]
    tf = 512
    nf = hidden // tf
    return pl.pallas_call(
        _ffn_kernel,
        grid=(n_tiles, nf),
        in_specs=[pl.BlockSpec((TM, d), lambda i, f: (i, 0)),
                  pl.BlockSpec((None, d, tf), lambda i, f: (layer, 0, f)),
                  pl.BlockSpec((None, d, tf), lambda i, f: (layer, 0, nf + f)),
                  pl.BlockSpec((None, tf, d), lambda i, f: (layer, f, 0))],
        out_specs=pl.BlockSpec((TM, d), lambda i, f: (i, 0)),
        out_shape=jax.ShapeDtypeStruct((n, d), BF16),
        scratch_shapes=[pltpu.VMEM((TM, d), F32)],
        compiler_params=_cp(("arbitrary", "arbitrary"), 60),
        name="ffn",
    )(h, w_ffn_in, w_ffn_in, w_ffn_out)


def _rope_tables(seq, batch, ctx_rows):
    t = jnp.arange(seq, dtype=jnp.int32)
    pos_row = (t // GRID_W).astype(F32)[:, None]
    pos_col = (t % GRID_W).astype(F32)[:, None]

    def one(rot_dim, reps):
        axis_dim = rot_dim // 2
        freqs = ROPE_THETA ** (-jnp.arange(0, axis_dim, 2, dtype=F32) / axis_dim)
        ar, ac = pos_row * freqs[None, :], pos_col * freqs[None, :]
        cos = jnp.concatenate([jnp.cos(ar), jnp.cos(ar), jnp.cos(ac), jnp.cos(ac)], axis=-1)
        sin = jnp.concatenate([-jnp.sin(ar), jnp.sin(ar), -jnp.sin(ac), jnp.sin(ac)], axis=-1)
        return jnp.tile(cos, (1, reps)), jnp.tile(sin, (1, reps))

    c128, s128 = one(HEAD_DIM, 1)
    c64, s64 = one(C_QK_DIM, 2)
    ones = jnp.ones((ctx_rows, LANE), F32)
    zeros = jnp.zeros((ctx_rows, LANE), F32)

    def rows(tab, pad):
        return jnp.concatenate([jnp.tile(tab, (batch, 1)), pad])

    return jnp.stack([rows(c128, ones), rows(s128, zeros), rows(c64, ones), rows(s64, zeros)])


def kernel(x, c, ctx, c_ctx, w_ada, b_ada, w_in, a_qk_norm, b_sink, c_lambda, c_subln, mla_q_norm, mla_w_q_up, mla_kv_norm, mla_w_kv_up, w_branch, w_out, ln_mix_g, ln_mix_b, w_ffn_in, w_ffn_out, ln_ffn_g, ln_ffn_b):
    batch, seq, d = x.shape
    ctx_len = ctx.shape[1]
    depth = w_ada.shape[0]
    assert d == D_MODEL and seq % TM == 0 and (batch * ctx_len) % TM == 0 and batch + 1 <= 8
    n_lat_rows = batch * seq
    n_rows = n_lat_rows + batch * ctx_len
    tiles_per_seq = seq // TM
    n_lat_tiles = n_lat_rows // TM
    n_tiles = n_rows // TM
    alpha = (2 * depth) ** 0.25

    c_all = jnp.concatenate([c, c_ctx[None, :], jnp.zeros((8 - batch - 1, d), F32)], axis=0)
    ada = _ada_all(c_all, w_ada, b_ada)
    tile_row = np.array([b for b in range(batch) for _ in range(tiles_per_seq)]
                        + [batch] * (n_tiles - n_lat_tiles), np.int32)

    def table(l, chunk, plus_one=False):
        v = ada[l, :, chunk * d:(chunk + 1) * d][tile_row]
        return ((1.0 + v) if plus_one else v)[:, None, :]

    tabs = _rope_tables(seq, batch, batch * ctx_len)
    x_all, h = _init(x.reshape(n_lat_rows, d), ctx.reshape(batch * ctx_len, d),
                     table(0, 1, True), table(0, 0))

    win_k = min(TQ + 2 * WINDOW, seq)
    sink_pad = jnp.zeros((depth, 8, LANE), F32).at[:, 0, :N_HEADS].set(b_sink)
    wq_pad = jnp.pad(mla_w_q_up.reshape(depth, MLA_RANK, N_HEADS, MLA_NOPE + MLA_ROPE),
                     ((0, 0), (0, 0), (0, 0), (0, MLA_QK_PAD - MLA_NOPE - MLA_ROPE))
                     ).reshape(depth, MLA_RANK, N_HEADS * MLA_QK_PAD)
    wkv_pad = jnp.pad(mla_w_kv_up.reshape(depth, MLA_RANK, N_HEADS, 2, LANE).transpose(0, 1, 3, 2, 4)
                      .reshape(depth, MLA_RANK, 2 * N_HEADS * LANE),
                      ((0, 0), (0, 0), (0, N_HEADS * MLA_QK_PAD - 2 * N_HEADS * LANE)))

    w_in_t = jnp.swapaxes(w_in, 1, 2)
    w_out_bf16 = w_out.astype(BF16)
    w_branch_bf16 = w_branch.astype(BF16)

    for l in range(depth):
        last = l == depth - 1
        act_tiles = n_lat_tiles if last else n_tiles
        act_rows = act_tiles * TM
        lam_init = 0.8 - 0.6 * math.exp(-0.3 * l)

        up, gates = _proj(h, w_in_t, tabs, a_qk_norm[l], mla_q_norm[l], mla_kv_norm[l], layer=l)
        dq, dk, dv = _mla_up(up, wq_pad, wkv_pad, tabs, layer=l)

        common = dict(batch=batch, seq=seq, ctx=ctx_len, with_ctx_queries=not last)
        o_a = _attention(_attn_a_kernel, "attn_a", up, up, up, [], **common, tq=TQ_TALL,
                         q_w=512, q_blk=COL_AQ // 512, k_w=256, k_blk=COL_AK // 256, v_w=256, v_blk=COL_AV // 256)
        o_b = _attention(functools.partial(_attn_b_kernel, seq=seq, win_k=win_k), "attn_b", up, up, up,
                         [sink_pad[l]], **common,
                         q_w=512, q_blk=COL_BQ // 512, k_w=256, k_blk=COL_BK // 256, v_w=256, v_blk=COL_BV // 256)
        o_c = _attention(functools.partial(_attn_c_kernel, lam_init=lam_init), "attn_c", up, up, up,
                         [c_lambda[l], c_subln[l].reshape(1, LANE)], **common,
                         q_w=512, q_blk=COL_CQ // 512, k_w=512, k_blk=COL_CK // 512, v_w=512, v_blk=COL_CV // 512)
        o_d = _attention(_attn_d_kernel, "attn_d", dq, dk, dv, [], **common, tq=TQ_TALL,
                         q_w=N_HEADS * MLA_QK_PAD, q_blk=0, k_w=N_HEADS * MLA_QK_PAD, k_blk=0,
                         v_w=N_HEADS * LANE, v_blk=0)

        lat_outs, ctx_outs = zip(o_a, o_b, o_c, o_d)
        y = _merge(lat_outs, None if last else ctx_outs, gates, w_branch_bf16,
                   layer=l, n_lat_rows=n_lat_rows, n_rows=act_rows)
        x_all, h2 = _out_ln(y, w_out_bf16, x_all, table(l, 2), ln_mix_g[l], ln_mix_b[l],
                            table(l, 4, True), table(l, 3), layer=l, n_rows=act_rows, alpha=alpha)
        ffn = _ffn(h2, w_ffn_in, w_ffn_out, layer=l, n_tiles=act_tiles)
        if last:
            x_all, _ = _post_ln(x_all, ffn, table(l, 5), ln_ffn_g[l], ln_ffn_b[l], None, None,
                                n_rows=act_rows, alpha=alpha, out_rows=n_lat_rows)
        else:
            x_all, h = _post_ln(x_all, ffn, table(l, 5), ln_ffn_g[l], ln_ffn_b[l],
                                table(l + 1, 1, True), table(l + 1, 0),
                                n_rows=act_rows, alpha=alpha, out_rows=n_rows)
    return x_all.reshape(batch, seq, d)
```

```python
import functools
import math

import numpy as np
import jax
import jax.numpy as jnp
from jax import lax
from jax.experimental import pallas as pl
from jax.experimental.pallas import tpu as pltpu

F32 = jnp.float32
BF16 = jnp.bfloat16

D_MODEL = 2048
GRID_W = 64
HEAD_DIM = 128
WINDOW = 128
ROPE_THETA = 10000.0
NORM_EPS = 1e-6
NEG_INF = -1e30
LOG2E = math.log2(math.e)
N_HEADS = 4
KV_HEADS = 2
C_QK_DIM = 64
MLA_RANK = 512
MLA_NOPE = 128
MLA_ROPE = 64
MLA_QK_PAD = 256
BRANCH_W = 512
N_BRANCH = 4
QKV_W = 4672
LANE = 128
GATE_ROW_OFF = QKV_W % LANE

UP_W = 5120
PROJ_TN = 512
N_UP_BLOCKS = UP_W // PROJ_TN
GATE_ROW_SPLIT = 3
PROJ_ROW_SPLIT = 4
COL_AQ, COL_AK, COL_AV = 0, 512, 768
COL_BQ, COL_BK, COL_BV = 1024, 1536, 1792
COL_CQ, COL_CK, COL_CV = 2048, 2560, 3072
COL_DQA, COL_DKVA, COL_DKR = 3584, 4096, 4608

TM = 1024
PROJ_ROW_TILES = (1536, 1024, 512)
TAB_ROPE128, TAB_ROPE64 = 0, 1
N_TAB_PLANES = 4
TO = 512
MERGE_TN = 512
TE = 512
TQ = 512
TQ_TALL = 1024
FFN_ACC_TN = 256
MIB = 1024 * 1024


def _cp(sem, vmem_mib):
    return pltpu.CompilerParams(dimension_semantics=sem, vmem_limit_bytes=vmem_mib * MIB)


def _dot(a, b):
    return jnp.dot(a, b, preferred_element_type=F32)


def _dot_nt(a, b):
    return lax.dot_general(a, b, (((1,), (1,)), ((), ())), preferred_element_type=F32)


def _ada_kernel(c_ref, w_ref, b_ref, o_ref):
    k = pl.program_id(1)
    c = c_ref[...]
    a = (c * (0.5 * jnp.tanh(0.5 * c) + 0.5)).astype(BF16)
    part = _dot(a, w_ref[...].astype(BF16))

    @pl.when(k == 0)
    def _():
        o_ref[...] = part + b_ref[...]

    @pl.when(k > 0)
    def _():
        o_ref[...] += part


def _ada_all(c_all, w_ada, b_ada):
    depth, d, n6 = w_ada.shape
    tk = 256
    return pl.pallas_call(
        _ada_kernel,
        grid=(depth, d // tk),
        in_specs=[pl.BlockSpec((8, tk), lambda l, k: (0, k)),
                  pl.BlockSpec((None, tk, n6), lambda l, k: (l, k, 0)),
                  pl.BlockSpec((None, 1, n6), lambda l, k: (l, 0, 0))],
        out_specs=pl.BlockSpec((None, 8, n6), lambda l, k: (l, 0, 0)),
        out_shape=jax.ShapeDtypeStruct((depth, 8, n6), F32),
        compiler_params=_cp(("arbitrary", "arbitrary"), 48),
        name="ada",
    )(c_all, w_ada, b_ada.reshape(depth, 1, n6))


def _standardise(z):
    mu = jnp.mean(z, axis=-1, keepdims=True)
    zc = z - mu
    var = jnp.mean(zc * zc, axis=-1, keepdims=True)
    return zc * lax.rsqrt(var + NORM_EPS)


def _init_kernel(x_ref, ctx_ref, sc_ref, sh_ref, xo_ref, ho_ref, *, n_lat):
    i = pl.program_id(0)

    def body(src):
        xs = _standardise(src[...])
        xo_ref[...] = xs
        ho_ref[...] = (xs * sc_ref[0] + sh_ref[0]).astype(BF16)

    @pl.when(i < n_lat)
    def _():
        body(x_ref)

    @pl.when(i >= n_lat)
    def _():
        body(ctx_ref)


def _init(x2, ctx2, sc_tbl, sh_tbl):
    n_lat_rows, d = x2.shape
    n_ctx_rows = ctx2.shape[0]
    n_lat = n_lat_rows // TE
    n_all = (n_lat_rows + n_ctx_rows) // TE
    tbl = pl.BlockSpec((1, 1, d), lambda i: ((i * TE) // TM, 0, 0))
    row = pl.BlockSpec((TE, d), lambda i: (i, 0))
    return pl.pallas_call(
        functools.partial(_init_kernel, n_lat=n_lat),
        grid=(n_all,),
        in_specs=[pl.BlockSpec((TE, d), lambda i: (jnp.minimum(i, n_lat - 1), 0)),
                  pl.BlockSpec((TE, d), lambda i: (jnp.maximum(i - n_lat, 0), 0)),
                  tbl, tbl],
        out_specs=[row, row],
        out_shape=[jax.ShapeDtypeStruct((n_lat_rows + n_ctx_rows, d), F32),
                   jax.ShapeDtypeStruct((n_lat_rows + n_ctx_rows, d), BF16)],
        compiler_params=_cp(("arbitrary",), 40),
        name="init_norm",
    )(x2, ctx2, sc_tbl, sh_tbl)


def _ln_kernel(x_ref, d_ref, gate_ref, lng_ref, lnb_ref, *rest, alpha, with_h):
    z = alpha * x_ref[...] + gate_ref[0] * d_ref[...].astype(F32)
    xn = _standardise(z) * lng_ref[...] + lnb_ref[...]
    if with_h:
        sc_ref, sh_ref, xo_ref, ho_ref = rest
        xo_ref[...] = xn
        ho_ref[...] = (xn * sc_ref[0] + sh_ref[0]).astype(BF16)
    else:
        (xo_ref,) = rest
        xo_ref[...] = xn


def _post_ln(x, delta, gate_tbl, ln_g, ln_b, sc_tbl, sh_tbl, *, n_rows, alpha, out_rows):
    d = x.shape[1]
    with_h = sc_tbl is not None
    tbl = pl.BlockSpec((1, 1, d), lambda i: ((i * TE) // TM, 0, 0))
    row = pl.BlockSpec((TE, d), lambda i: (i, 0))
    vec = pl.BlockSpec((1, d), lambda i: (0, 0))
    in_specs = [row, row, tbl, vec, vec]
    args = [x, delta, gate_tbl, ln_g.reshape(1, d), ln_b.reshape(1, d)]
    out_specs = [row]
    out_shape = [jax.ShapeDtypeStruct((out_rows, d), F32)]
    if with_h:
        in_specs += [tbl, tbl]
        args += [sc_tbl, sh_tbl]
        out_specs.append(row)
        out_shape.append(jax.ShapeDtypeStruct((out_rows, d), BF16))
    res = pl.pallas_call(
        functools.partial(_ln_kernel, alpha=alpha, with_h=with_h),
        grid=(n_rows // TE,),
        in_specs=in_specs, out_specs=out_specs, out_shape=out_shape,
        compiler_params=_cp(("arbitrary",), 40),
        name="post_ln",
    )(*args)
    return res if with_h else (res[0], None)


def _swap_select(x, near, far, bit):
    lane = lax.broadcasted_iota(jnp.int32, x.shape, 1)
    return jnp.where((lane & bit) == 0, pltpu.roll(x, far, 1), pltpu.roll(x, near, 1))


def _rope64(x, cos, sin):
    return x * cos + _swap_select(x, 16, LANE - 16, 16) * sin


def _rms(x, g):
    return x * lax.rsqrt(jnp.mean(x * x, axis=-1, keepdims=True) + NORM_EPS) * g


def _proj_kernel(h_ref, wa_ref, wb_ref, tab_ref, aqk_ref, mqkv_ref, up_ref, g_ref):
    j = pl.program_id(1)
    blk = lambda col: col // PROJ_TN

    def for_row_pieces(fn, split=PROJ_ROW_SPLIT):
        piece = h_ref.shape[0] // split
        for r in range(split):
            fn(slice(r * piece, (r + 1) * piece))

    def block(pred, epilogue, width=PROJ_TN):
        @pl.when(pred)
        def _():
            w = wa_ref[:width, :].astype(BF16)

            def one(rows):
                u = _dot_nt(h_ref[rows, :], w)
                cols = [u[:, c * LANE:(c + 1) * LANE] for c in range(width // LANE)]
                outs = epilogue(cols, rows)
                for c, val in enumerate(outs):
                    up_ref[rows, c * LANE:(c + 1) * LANE] = val.astype(BF16)

            for_row_pieces(one)

    def rope(x, kind, near, rows):
        lane = lax.broadcasted_iota(jnp.int32, x.shape, 1)
        swapped = jnp.where((lane & near) == 0, pltpu.roll(x, LANE - near, 1), pltpu.roll(x, near, 1))
        return x * tab_ref[2 * kind, rows, :] + swapped * tab_ref[2 * kind + 1, rows, :]

    def a_heads(cols, rows):
        is_q = j == blk(COL_AQ)
        gain = aqk_ref[pl.ds(j - blk(COL_AQ), 1), :]
        scale = jnp.where(is_q, LOG2E * HEAD_DIM ** -0.5, 1.0)
        outs = []
        for c, x in enumerate(cols):
            y = rope(_rms(x, gain), TAB_ROPE128, 32, rows) * scale
            outs.append(y if c < KV_HEADS else jnp.where(is_q, y, x))
        return outs

    block((j == blk(COL_AQ)) | (j == blk(COL_AK)), a_heads)

    def rope_heads(cols, rows):
        wide = j < blk(COL_CQ)
        near = jnp.where(wide, 32, 16)
        kind = jnp.where(wide, TAB_ROPE128, TAB_ROPE64)
        scale = jnp.where(j == blk(COL_BQ), LOG2E * HEAD_DIM ** -0.5,
                          jnp.where(j == blk(COL_CQ), LOG2E * C_QK_DIM ** -0.5, 1.0))
        all_plain = j == blk(COL_CV)
        tail_plain = all_plain | (j == blk(COL_BK))
        outs = []
        for c, x in enumerate(cols):
            plain = all_plain if c < KV_HEADS else tail_plain
            outs.append(jnp.where(plain, x, rope(x, kind, near, rows) * scale))
        return outs

    block((j >= blk(COL_BQ)) & (j <= blk(COL_CV)), rope_heads)

    def latent_norm(cols, rows):
        gain = mqkv_ref[pl.ds(j - blk(COL_DQA), 1), :]
        ms = sum(jnp.sum(x * x, axis=-1, keepdims=True) for x in cols) * (1.0 / (len(cols) * LANE))
        inv = lax.rsqrt(ms + NORM_EPS)
        return [x * inv * gain[:, c * LANE:(c + 1) * LANE] for c, x in enumerate(cols)]

    block((j == blk(COL_DQA)) | (j == blk(COL_DKVA)), latent_norm)

    def rope_key(cols, rows):
        x = cols[0]
        lane = lax.broadcasted_iota(jnp.int32, x.shape, 1)
        return [jnp.where(lane < MLA_ROPE, rope(x, TAB_ROPE64, 16, rows), 0.0)]

    block(j == blk(COL_DKR), rope_key, width=LANE)

    @pl.when(j == blk(COL_DKR))
    def _():
        up_ref[:, LANE:] = jnp.zeros((up_ref.shape[0], PROJ_TN - LANE), BF16)

    @pl.when(j >= N_UP_BLOCKS)
    def _():
        w = jnp.concatenate([wa_ref[GATE_ROW_OFF:, :], wb_ref[...]], axis=0).astype(BF16)

        def one(rows):
            g = _dot_nt(h_ref[rows, :], w)
            g_ref[rows, :] = (0.5 * jnp.tanh(0.5 * g) + 0.5).astype(BF16)

        for_row_pieces(one, GATE_ROW_SPLIT)


def _row_tile(n, choices):
    return next(t for t in choices if n % t == 0)


def _proj(h, w_in_t, tabs, aqk, mq, mkv, *, layer):
    n, d = h.shape
    tp = _row_tile(n, PROJ_ROW_TILES)
    assert tp % (16 * PROJ_ROW_SPLIT) == 0 and tp % (16 * GATE_ROW_SPLIT) == 0
    nt = n // tp
    n_gate = w_in_t.shape[1] - QKV_W
    gate_blocks = n_gate // PROJ_TN
    first_gate_blk = (QKV_W - GATE_ROW_OFF) // PROJ_TN
    assert (QKV_W - GATE_ROW_OFF) % PROJ_TN == 0 and n_gate % PROJ_TN == 0 and first_gate_blk == N_UP_BLOCKS - 1

    def wa_idx(i, j):
        return (layer, jnp.where(j < N_UP_BLOCKS, j, j - 1), 0)

    def wb_idx(i, j):
        g = jnp.maximum(j - N_UP_BLOCKS, 0)
        return (layer, (QKV_W - GATE_ROW_OFF + (g + 1) * PROJ_TN) // GATE_ROW_OFF, 0)

    return pl.pallas_call(
        _proj_kernel,
        grid=(nt, N_UP_BLOCKS + gate_blocks),
        in_specs=[pl.BlockSpec((tp, d), lambda i, j: (i, 0)),
                  pl.BlockSpec((None, PROJ_TN, d), wa_idx),
                  pl.BlockSpec((None, GATE_ROW_OFF, d), wb_idx),
                  pl.BlockSpec((N_TAB_PLANES, tp, LANE), lambda i, j: (0, i, 0)),
                  pl.BlockSpec((2, LANE), lambda i, j: (0, 0)),
                  pl.BlockSpec((2, MLA_RANK), lambda i, j: (0, 0))],
        out_specs=[pl.BlockSpec((tp, PROJ_TN), lambda i, j: (i, jnp.minimum(j, N_UP_BLOCKS - 1))),
                   pl.BlockSpec((tp, PROJ_TN), lambda i, j: (i, jnp.maximum(j - N_UP_BLOCKS, 0)))],
        out_shape=[jax.ShapeDtypeStruct((n, UP_W), BF16),
                   jax.ShapeDtypeStruct((n, n_gate), BF16)],
        compiler_params=_cp(("arbitrary", "arbitrary"), 58),
        name="proj",
    )(h, w_in_t, w_in_t, tabs, aqk, jnp.stack([mq, mkv]))


def _mla_up_kernel(qa_ref, kva_ref, kr_ref, wq_ref, wkv_ref, tab_ref, dq_ref, dk_ref, dv_ref):
    cos, sin = tab_ref[2 * TAB_ROPE64], tab_ref[2 * TAB_ROPE64 + 1]
    q = _dot(qa_ref[...], wq_ref[...].astype(BF16))
    kv = _dot(kva_ref[...], wkv_ref[...].astype(BF16))
    scale = LOG2E * (MLA_NOPE + MLA_ROPE) ** -0.5
    kr = kr_ref[...]
    for hh in range(N_HEADS):
        base = hh * MLA_QK_PAD
        dq_ref[:, base:base + LANE] = (q[:, base:base + LANE] * scale).astype(BF16)
        dq_ref[:, base + LANE:base + 2 * LANE] = (
            _rope64(q[:, base + LANE:base + 2 * LANE], cos, sin) * scale).astype(BF16)
        dk_ref[:, base:base + LANE] = kv[:, hh * LANE:(hh + 1) * LANE].astype(BF16)
        dk_ref[:, base + LANE:base + 2 * LANE] = kr
    dv_ref[...] = kv[:, N_HEADS * LANE:].astype(BF16)


def _mla_up(up, wq_pad, wkv_perm, tabs, *, layer):
    n = up.shape[0]
    nt = n // TM
    wide = N_HEADS * MLA_QK_PAD
    return pl.pallas_call(
        _mla_up_kernel,
        grid=(nt,),
        in_specs=[pl.BlockSpec((TM, MLA_RANK), lambda i: (i, COL_DQA // MLA_RANK)),
                  pl.BlockSpec((TM, MLA_RANK), lambda i: (i, COL_DKVA // MLA_RANK)),
                  pl.BlockSpec((TM, LANE), lambda i: (i, COL_DKR // LANE)),
                  pl.BlockSpec((None, MLA_RANK, wide), lambda i: (layer, 0, 0)),
                  pl.BlockSpec((None, MLA_RANK, wide), lambda i: (layer, 0, 0)),
                  pl.BlockSpec((N_TAB_PLANES, TM, LANE), lambda i: (0, i, 0))],
        out_specs=[pl.BlockSpec((TM, wide), lambda i: (i, 0)),
                   pl.BlockSpec((TM, wide), lambda i: (i, 0)),
                   pl.BlockSpec((TM, N_HEADS * LANE), lambda i: (i, 0))],
        out_shape=[jax.ShapeDtypeStruct((n, wide), BF16),
                   jax.ShapeDtypeStruct((n, wide), BF16),
                   jax.ShapeDtypeStruct((n, N_HEADS * LANE), BF16)],
        compiler_params=_cp(("arbitrary",), 48),
        name="mla_up",
    )(up, up, up, wq_pad, wkv_perm, tabs)


def _attend(q, pieces, extra=None):
    scores = []
    for k, _, mask in pieces:
        s = _dot_nt(q, k)
        if mask is not None:
            s = jnp.where(mask, s, NEG_INF)
        scores.append(s)
    m = functools.reduce(jnp.maximum, [jnp.max(s, axis=-1, keepdims=True) for s in scores])
    if extra is not None:
        m = jnp.maximum(m, extra)
    den = None
    out = None
    for s, (_, v, _) in zip(scores, pieces):
        e = jnp.exp2(s - m)
        part = jnp.sum(e, axis=-1, keepdims=True)
        pv = _dot(e.astype(BF16), v)
        den = part if den is None else den + part
        out = pv if out is None else out + pv
    if extra is not None:
        den = den + jnp.exp2(extra - m)
    return out / den


def _heads(ref, k, width=LANE):
    return ref[:, k * width:(k + 1) * width]


def _query_steps(run, q_ref, o_ref, qc_ref, oc_ref, nq):
    t = pl.program_id(1)
    if qc_ref is None:
        run(q_ref, o_ref, True)
        return

    @pl.when(t < nq)
    def _():
        run(q_ref, o_ref, True)

    @pl.when(t >= nq)
    def _():
        run(qc_ref, oc_ref, False)


def _split_refs(refs, n_extra):
    with_ctx = len(refs) == 8 + n_extra
    q, rest = refs[0], refs[1:]
    qc = None
    if with_ctx:
        qc, rest = rest[0], rest[1:]
    kl, vl, kc, vc = rest[:4]
    extras = rest[4:4 + n_extra]
    outs = rest[4 + n_extra:]
    return q, qc, kl, vl, kc, vc, extras, outs[0], (outs[1] if with_ctx else None)


def _attn_a_kernel(*refs, nq):
    q_ref, qc_ref, kl_ref, vl_ref, kc_ref, vc_ref, _, o_ref, oc_ref = _split_refs(refs, 0)

    def run(q_ref, o_ref, with_latent):
        for hh in range(N_HEADS):
            kvh = hh // (N_HEADS // KV_HEADS)
            pieces = [(_heads(kc_ref, kvh), _heads(vc_ref, kvh), None)]
            if with_latent:
                pieces.append((_heads(kl_ref, kvh), _heads(vl_ref, kvh), None))
            o_ref[:, hh * LANE:(hh + 1) * LANE] = _attend(_heads(q_ref, hh), pieces).astype(BF16)

    _query_steps(run, q_ref, o_ref, qc_ref, oc_ref, nq)


def _attn_b_kernel(*refs, nq, seq, win_k):
    q_ref, qc_ref, kl_ref, vl_ref, kc_ref, vc_ref, (sink_ref,), o_ref, oc_ref = _split_refs(refs, 1)
    t = pl.program_id(1)

    def run(q_ref, o_ref, with_latent):
        rows = q_ref.shape[0]
        if with_latent:
            q0 = t * rows
            start = pl.multiple_of(jnp.clip(q0 - WINDOW, 0, seq - win_k), LANE)
            r = lax.broadcasted_iota(jnp.int32, (rows, win_k), 0)
            c = lax.broadcasted_iota(jnp.int32, (rows, win_k), 1)
            mask = jnp.abs(q0 + r - (start + c)) <= WINDOW
        for hh in range(N_HEADS):
            kvh = hh // (N_HEADS // KV_HEADS)
            sink = LOG2E * sink_ref[0:1, hh:hh + 1]
            pieces = [(_heads(kc_ref, kvh), _heads(vc_ref, kvh), None)]
            if with_latent:
                kw = kl_ref[pl.ds(start, win_k), kvh * LANE:(kvh + 1) * LANE]
                vw = vl_ref[pl.ds(start, win_k), kvh * LANE:(kvh + 1) * LANE]
                pieces.append((kw, vw, mask))
            o_ref[:, hh * LANE:(hh + 1) * LANE] = _attend(_heads(q_ref, hh), pieces, extra=sink).astype(BF16)

    _query_steps(run, q_ref, o_ref, qc_ref, oc_ref, nq)


def _attn_c_kernel(*refs, nq, lam_init):
    q_ref, qc_ref, kl_ref, vl_ref, kc_ref, vc_ref, (lam_ref, g_ref), o_ref, oc_ref = _split_refs(refs, 2)
    lam = (jnp.exp(jnp.sum(lam_ref[0:1, :] * lam_ref[1:2, :], axis=-1, keepdims=True))
           - jnp.exp(jnp.sum(lam_ref[2:3, :] * lam_ref[3:4, :], axis=-1, keepdims=True)) + lam_init)

    def run(q_ref, o_ref, with_latent):
        rows = q_ref.shape[0]
        lane = lax.broadcasted_iota(jnp.int32, (rows, LANE), 1)
        for hh in range(N_HEADS):
            q = _heads(q_ref, hh)
            zero = jnp.zeros_like(q)
            pieces = [(_heads(kc_ref, hh), _heads(vc_ref, hh), None)]
            if with_latent:
                pieces.append((_heads(kl_ref, hh), _heads(vl_ref, hh), None))
            o1 = _attend(jnp.where(lane < C_QK_DIM, q, zero), pieces)
            o2 = _attend(jnp.where(lane >= C_QK_DIM, q, zero), pieces)
            o = _rms(o1 - lam * o2, g_ref[...]) * (1.0 - lam_init)
            o_ref[:, hh * LANE:(hh + 1) * LANE] = o.astype(BF16)

    _query_steps(run, q_ref, o_ref, qc_ref, oc_ref, nq)


def _attn_d_kernel(*refs, nq):
    q_ref, qc_ref, kl_ref, vl_ref, kc_ref, vc_ref, _, o_ref, oc_ref = _split_refs(refs, 0)

    def run(q_ref, o_ref, with_latent):
        for hh in range(N_HEADS):
            pieces = [(_heads(kc_ref, hh, MLA_QK_PAD), _heads(vc_ref, hh), None)]
            if with_latent:
                pieces.append((_heads(kl_ref, hh, MLA_QK_PAD), _heads(vl_ref, hh), None))
            q = _heads(q_ref, hh, MLA_QK_PAD)
            o_ref[:, hh * LANE:(hh + 1) * LANE] = _attend(q, pieces).astype(BF16)

    _query_steps(run, q_ref, o_ref, qc_ref, oc_ref, nq)


def _attention(kern, name, q_src, k_src, v_src, extras, *, batch, seq, ctx, with_ctx_queries,
               q_w, q_blk, k_w, k_blk, v_w, v_blk, tq=TQ):
    tq = min(tq, seq)
    nq = seq // tq
    steps = nq + (1 if with_ctx_queries else 0)
    ctx_blk0 = batch * seq // ctx

    def qrow(b, t):
        return b * nq + jnp.minimum(t, nq - 1)

    in_specs = [pl.BlockSpec((tq, q_w), lambda b, t: (qrow(b, t), q_blk))]
    args = [q_src]
    if with_ctx_queries:
        in_specs.append(pl.BlockSpec((ctx, q_w), lambda b, t: (ctx_blk0 + b, q_blk)))
        args.append(q_src)
    in_specs += [pl.BlockSpec((seq, k_w), lambda b, t: (b, k_blk)),
                 pl.BlockSpec((seq, v_w), lambda b, t: (b, v_blk)),
                 pl.BlockSpec((ctx, k_w), lambda b, t: (ctx_blk0 + b, k_blk)),
                 pl.BlockSpec((ctx, v_w), lambda b, t: (ctx_blk0 + b, v_blk))]
    args += [k_src, v_src, k_src, v_src]
    for e in extras:
        in_specs.append(pl.BlockSpec(e.shape, lambda b, t: (0, 0)))
        args.append(e)
    out_specs = [pl.BlockSpec((tq, BRANCH_W), lambda b, t: (qrow(b, t), 0))]
    out_shape = [jax.ShapeDtypeStruct((batch * seq, BRANCH_W), BF16)]
    if with_ctx_queries:
        out_specs.append(pl.BlockSpec((ctx, BRANCH_W), lambda b, t: (b, 0)))
        out_shape.append(jax.ShapeDtypeStruct((batch * ctx, BRANCH_W), BF16))
    res = pl.pallas_call(
        functools.partial(kern, nq=nq),
        grid=(batch, steps),
        in_specs=in_specs, out_specs=out_specs, out_shape=out_shape,
        compiler_params=_cp(("arbitrary", "arbitrary"), 60),
        name=name,
    )(*args)
    return (res[0], res[1]) if with_ctx_queries else (res[0], None)


def _merge_kernel(*refs, n_lat_tiles, with_ctx):
    n_src = 2 * N_BRANCH if with_ctx else N_BRANCH
    lat_refs, ctx_refs = refs[:N_BRANCH], refs[N_BRANCH:n_src]
    g_ref, w_ref, y_ref = refs[n_src:]
    d = y_ref.shape[1]

    def body(o_refs):
        for c in range(d // MERGE_TN):
            cols = slice(c * MERGE_TN, (c + 1) * MERGE_TN)
            acc = None
            for m, o_ref in enumerate(o_refs):
                gate = g_ref[:, m * d + c * MERGE_TN:m * d + (c + 1) * MERGE_TN].astype(F32)
                p = gate * _dot(o_ref[...], w_ref[m, :, cols])
                acc = p if acc is None else acc + p
            y_ref[:, cols] = acc.astype(BF16)

    if not with_ctx:
        body(lat_refs)
        return
    i = pl.program_id(0)

    @pl.when(i < n_lat_tiles)
    def _():
        body(lat_refs)

    @pl.when(i >= n_lat_tiles)
    def _():
        body(ctx_refs)


def _merge(lat_outs, ctx_outs, gates, w_branch_bf16, *, layer, n_lat_rows, n_rows):
    d = w_branch_bf16.shape[3]
    n_lat_tiles = n_lat_rows // TO
    with_ctx = ctx_outs is not None
    srcs = list(lat_outs) + (list(ctx_outs) if with_ctx else [])
    specs = [pl.BlockSpec((TO, BRANCH_W), lambda i: (jnp.minimum(i, n_lat_tiles - 1), 0))] * N_BRANCH
    if with_ctx:
        specs += [pl.BlockSpec((TO, BRANCH_W), lambda i: (jnp.maximum(i - n_lat_tiles, 0), 0))] * N_BRANCH
    return pl.pallas_call(
        functools.partial(_merge_kernel, n_lat_tiles=n_lat_tiles, with_ctx=with_ctx),
        grid=(n_rows // TO,),
        in_specs=specs + [pl.BlockSpec((TO, N_BRANCH * d), lambda i: (i, 0)),
                          pl.BlockSpec((None, N_BRANCH, BRANCH_W, d), lambda i: (layer, 0, 0, 0))],
        out_specs=pl.BlockSpec((TO, d), lambda i: (i, 0)),
        out_shape=jax.ShapeDtypeStruct((n_rows, d), BF16),
        compiler_params=_cp(("arbitrary",), 56),
        name="merge",
    )(*srcs, gates, w_branch_bf16)


def _out_ln_kernel(y_ref, w_ref, x_ref, gate_ref, lng_ref, lnb_ref, sc_ref, sh_ref, xo_ref, ho_ref, *, alpha):
    piece = y_ref.shape[0] // 2
    for r in range(2):
        rows = slice(r * piece, (r + 1) * piece)
        mix = _dot(y_ref[rows, :], w_ref[...])
        z = alpha * x_ref[rows, :] + gate_ref[0] * mix
        xn = _standardise(z) * lng_ref[...] + lnb_ref[...]
        xo_ref[rows, :] = xn
        ho_ref[rows, :] = (xn * sc_ref[0] + sh_ref[0]).astype(BF16)


def _out_ln(y, w_out_bf16, x, gate_tbl, ln_g, ln_b, sc_tbl, sh_tbl, *, layer, n_rows, alpha):
    d = x.shape[1]
    tbl = pl.BlockSpec((1, 1, d), lambda i: ((i * TO) // TM, 0, 0))
    row = pl.BlockSpec((TO, d), lambda i: (i, 0))
    vec = pl.BlockSpec((1, d), lambda i: (0, 0))
    return pl.pallas_call(
        functools.partial(_out_ln_kernel, alpha=alpha),
        grid=(n_rows // TO,),
        in_specs=[row, pl.BlockSpec((None, d, d), lambda i: (layer, 0, 0)), row, tbl, vec, vec, tbl, tbl],
        out_specs=[row, row],
        out_shape=[jax.ShapeDtypeStruct((n_rows, d), F32), jax.ShapeDtypeStruct((n_rows, d), BF16)],
        compiler_params=_cp(("arbitrary",), 56),
        name="out_ln",
    )(y, w_out_bf16, x, gate_tbl, ln_g.reshape(1, d), ln_b.reshape(1, d), sc_tbl, sh_tbl)


def _ffn_kernel(h_ref, wg_ref, wu_ref, wo_ref, o_ref, acc_ref):
    f = pl.program_id(1)

    @pl.when(f == 0)
    def _():
        acc_ref[...] = jnp.zeros_like(acc_ref)

    h = h_ref[...]
    g = _dot(h, wg_ref[...].astype(BF16))
    u = _dot(h, wu_ref[...].astype(BF16))
    a = (g * (0.5 * jnp.tanh(0.5 * g) + 0.5) * u).astype(BF16)

    for n in range(acc_ref.shape[1] // FFN_ACC_TN):
        cols = slice(n * FFN_ACC_TN, (n + 1) * FFN_ACC_TN)
        acc_ref[:, cols] += _dot(a, wo_ref[:, cols].astype(BF16))

    @pl.when(f == pl.num_programs(1) - 1)
    def _():
        o_ref[...] = acc_ref[...].astype(BF16)


def _ffn(h, w_ffn_in, w_ffn_out, *, layer, n_tiles):
    d = h.shape[1]
    n = n_tiles * TM
    hidden = w_ffn_out.shape[1]
    tf = 512
    nf = hidden // tf
    return pl.pallas_call(
        _ffn_kernel,
        grid=(n_tiles, nf),
        in_specs=[pl.BlockSpec((TM, d), lambda i, f: (i, 0)),
                  pl.BlockSpec((None, d, tf), lambda i, f: (layer, 0, f)),
                  pl.BlockSpec((None, d, tf), lambda i, f: (layer, 0, nf + f)),
                  pl.BlockSpec((None, tf, d), lambda i, f: (layer, f, 0))],
        out_specs=pl.BlockSpec((TM, d), lambda i, f: (i, 0)),
        out_shape=jax.ShapeDtypeStruct((n, d), BF16),
        scratch_shapes=[pltpu.VMEM((TM, d), F32)],
        compiler_params=_cp(("arbitrary", "arbitrary"), 60),
        name="ffn",
    )(h, w_ffn_in, w_ffn_in, w_ffn_out)


def _rope_tables(seq, batch, ctx_rows):
    t = jnp.arange(seq, dtype=jnp.int32)
    pos_row = (t // GRID_W).astype(F32)[:, None]
    pos_col = (t % GRID_W).astype(F32)[:, None]

    def one(rot_dim, reps):
        axis_dim = rot_dim // 2
        freqs = ROPE_THETA ** (-jnp.arange(0, axis_dim, 2, dtype=F32) / axis_dim)
        ar, ac = pos_row * freqs[None, :], pos_col * freqs[None, :]
        cos = jnp.concatenate([jnp.cos(ar), jnp.cos(ar), jnp.cos(ac), jnp.cos(ac)], axis=-1)
        sin = jnp.concatenate([-jnp.sin(ar), jnp.sin(ar), -jnp.sin(ac), jnp.sin(ac)], axis=-1)
        return jnp.tile(cos, (1, reps)), jnp.tile(sin, (1, reps))

    c128, s128 = one(HEAD_DIM, 1)
    c64, s64 = one(C_QK_DIM, 2)
    ones = jnp.ones((ctx_rows, LANE), F32)
    zeros = jnp.zeros((ctx_rows, LANE), F32)

    def rows(tab, pad):
        return jnp.concatenate([jnp.tile(tab, (batch, 1)), pad])

    return jnp.stack([rows(c128, ones), rows(s128, zeros), rows(c64, ones), rows(s64, zeros)])


def kernel(x, c, ctx, c_ctx, w_ada, b_ada, w_in, a_qk_norm, b_sink, c_lambda, c_subln, mla_q_norm, mla_w_q_up, mla_kv_norm, mla_w_kv_up, w_branch, w_out, ln_mix_g, ln_mix_b, w_ffn_in, w_ffn_out, ln_ffn_g, ln_ffn_b):
    batch, seq, d = x.shape
    ctx_len = ctx.shape[1]
    depth = w_ada.shape[0]
    assert d == D_MODEL and seq % TM == 0 and (batch * ctx_len) % TM == 0 and batch + 1 <= 8
    n_lat_rows = batch * seq
    n_rows = n_lat_rows + batch * ctx_len
    tiles_per_seq = seq // TM
    n_lat_tiles = n_lat_rows // TM
    n_tiles = n_rows // TM
    alpha = (2 * depth) ** 0.25

    c_all = jnp.concatenate([c, c_ctx[None, :], jnp.zeros((8 - batch - 1, d), F32)], axis=0)
    ada = _ada_all(c_all, w_ada, b_ada)
    tile_row = np.array([b for b in range(batch) for _ in range(tiles_per_seq)]
                        + [batch] * (n_tiles - n_lat_tiles), np.int32)

    def table(l, chunk, plus_one=False):
        v = ada[l, :, chunk * d:(chunk + 1) * d][tile_row]
        return ((1.0 + v) if plus_one else v)[:, None, :]

    tabs = _rope_tables(seq, batch, batch * ctx_len)
    x_all, h = _init(x.reshape(n_lat_rows, d), ctx.reshape(batch * ctx_len, d),
                     table(0, 1, True), table(0, 0))

    win_k = min(TQ + 2 * WINDOW, seq)
    sink_pad = jnp.zeros((depth, 8, LANE), F32).at[:, 0, :N_HEADS].set(b_sink)
    wq_pad = jnp.pad(mla_w_q_up.reshape(depth, MLA_RANK, N_HEADS, MLA_NOPE + MLA_ROPE),
                     ((0, 0), (0, 0), (0, 0), (0, MLA_QK_PAD - MLA_NOPE - MLA_ROPE))
                     ).reshape(depth, MLA_RANK, N_HEADS * MLA_QK_PAD)
    wkv_pad = jnp.pad(mla_w_kv_up.reshape(depth, MLA_RANK, N_HEADS, 2, LANE).transpose(0, 1, 3, 2, 4)
                      .reshape(depth, MLA_RANK, 2 * N_HEADS * LANE),
                      ((0, 0), (0, 0), (0, N_HEADS * MLA_QK_PAD - 2 * N_HEADS * LANE)))

    w_in_t = jnp.swapaxes(w_in, 1, 2)
    w_out_bf16 = w_out.astype(BF16)
    w_branch_bf16 = w_branch.astype(BF16)

    for l in range(depth):
        last = l == depth - 1
        act_tiles = n_lat_tiles if last else n_tiles
        act_rows = act_tiles * TM
        lam_init = 0.8 - 0.6 * math.exp(-0.3 * l)

        up, gates = _proj(h, w_in_t, tabs, a_qk_norm[l], mla_q_norm[l], mla_kv_norm[l], layer=l)
        dq, dk, dv = _mla_up(up, wq_pad, wkv_pad, tabs, layer=l)

        common = dict(batch=batch, seq=seq, ctx=ctx_len, with_ctx_queries=not last)
        o_a = _attention(_attn_a_kernel, "attn_a", up, up, up, [], **common, tq=TQ_TALL,
                         q_w=512, q_blk=COL_AQ // 512, k_w=256, k_blk=COL_AK // 256, v_w=256, v_blk=COL_AV // 256)
        o_b = _attention(functools.partial(_attn_b_kernel, seq=seq, win_k=win_k), "attn_b", up, up, up,
                         [sink_pad[l]], **common,
                         q_w=512, q_blk=COL_BQ // 512, k_w=256, k_blk=COL_BK // 256, v_w=256, v_blk=COL_BV // 256)
        o_c = _attention(functools.partial(_attn_c_kernel, lam_init=lam_init), "attn_c", up, up, up,
                         [c_lambda[l], c_subln[l].reshape(1, LANE)], **common,
                         q_w=512, q_blk=COL_CQ // 512, k_w=512, k_blk=COL_CK // 512, v_w=512, v_blk=COL_CV // 512)
        o_d = _attention(_attn_d_kernel, "attn_d", dq, dk, dv, [], **common, tq=TQ_TALL,
                         q_w=N_HEADS * MLA_QK_PAD, q_blk=0, k_w=N_HEADS * MLA_QK_PAD, k_blk=0,
                         v_w=N_HEADS * LANE, v_blk=0)

        lat_outs, ctx_outs = zip(o_a, o_b, o_c, o_d)
        y = _merge(lat_outs, None if last else ctx_outs, gates, w_branch_bf16,
                   layer=l, n_lat_rows=n_lat_rows, n_rows=act_rows)
        x_all, h2 = _out_ln(y, w_out_bf16, x_all, table(l, 2), ln_mix_g[l], ln_mix_b[l],
                            table(l, 4, True), table(l, 3), layer=l, n_rows=act_rows, alpha=alpha)
        ffn = _ffn(h2, w_ffn_in, w_ffn_out, layer=l, n_tiles=act_tiles)
        if last:
            x_all, _ = _post_ln(x_all, ffn, table(l, 5), ln_ffn_g[l], ln_ffn_b[l], None, None,
                                n_rows=act_rows, alpha=alpha, out_rows=n_lat_rows)
        else:
            x_all, h = _post_ln(x_all, ffn, table(l, 5), ln_ffn_g[l], ln_ffn_b[l],
                                table(l + 1, 1, True), table(l + 1, 0),
                                n_rows=act_rows, alpha=alpha, out_rows=n_rows)
    return x_all.reshape(batch, seq, d)
```

```python
import functools
import math

import numpy as np
import jax
import jax.numpy as jnp
from jax import lax
from jax.experimental import pallas as pl
from jax.experimental.pallas import tpu as pltpu

F32 = jnp.float32
BF16 = jnp.bfloat16

D_MODEL = 2048
GRID_W = 64
HEAD_DIM = 128
WINDOW = 128
ROPE_THETA = 10000.0
NORM_EPS = 1e-6
NEG_INF = -1e30
LOG2E = math.log2(math.e)
N_HEADS = 4
KV_HEADS = 2
C_QK_DIM = 64
MLA_RANK = 512
MLA_NOPE = 128
MLA_ROPE = 64
MLA_QK_PAD = 256
BRANCH_W = 512
N_BRANCH = 4
QKV_W = 4672
LANE = 128
GATE_ROW_OFF = QKV_W % LANE

UP_W = 5120
PROJ_TN = 512
N_UP_BLOCKS = UP_W // PROJ_TN
PROJ_ROW_SPLIT = 4
COL_AQ, COL_AK, COL_AV = 0, 512, 768
COL_BQ, COL_BK, COL_BV = 1024, 1536, 1792
COL_CQ, COL_CK, COL_CV = 2048, 2560, 3072
COL_DQA, COL_DKVA, COL_DKR = 3584, 4096, 4608

TM = 1024
PROJ_ROW_TILES = (1536, 1024, 512)
TAB_ROPE128, TAB_ROPE64 = 0, 1
N_TAB_PLANES = 4
TO = 512
MERGE_TN = 512
TE = 512
TQ = 512
TQ_TALL = 1024
FFN_ACC_TN = 512
MIB = 1024 * 1024


def _cp(sem, vmem_mib):
    return pltpu.CompilerParams(dimension_semantics=sem, vmem_limit_bytes=vmem_mib * MIB)


def _dot(a, b):
    return jnp.dot(a, b, preferred_element_type=F32)


def _dot_nt(a, b):
    return lax.dot_general(a, b, (((1,), (1,)), ((), ())), preferred_element_type=F32)


def _ada_kernel(c_ref, w_ref, b_ref, o_ref):
    k = pl.program_id(1)
    c = c_ref[...]
    a = (c * (0.5 * jnp.tanh(0.5 * c) + 0.5)).astype(BF16)
    part = _dot(a, w_ref[...].astype(BF16))

    @pl.when(k == 0)
    def _():
        o_ref[...] = part + b_ref[...]

    @pl.when(k > 0)
    def _():
        o_ref[...] += part


def _ada_all(c_all, w_ada, b_ada):
    depth, d, n6 = w_ada.shape
    tk = 128
    return pl.pallas_call(
        _ada_kernel,
        grid=(depth, d // tk),
        in_specs=[pl.BlockSpec((8, tk), lambda l, k: (0, k)),
                  pl.BlockSpec((None, tk, n6), lambda l, k: (l, k, 0)),
                  pl.BlockSpec((None, 1, n6), lambda l, k: (l, 0, 0))],
        out_specs=pl.BlockSpec((None, 8, n6), lambda l, k: (l, 0, 0)),
        out_shape=jax.ShapeDtypeStruct((depth, 8, n6), F32),
        compiler_params=_cp(("arbitrary", "arbitrary"), 40),
        name="ada",
    )(c_all, w_ada, b_ada.reshape(depth, 1, n6))


def _standardise(z):
    mu = jnp.mean(z, axis=-1, keepdims=True)
    zc = z - mu
    var = jnp.mean(zc * zc, axis=-1, keepdims=True)
    return zc * lax.rsqrt(var + NORM_EPS)


def _init_kernel(x_ref, ctx_ref, sc_ref, sh_ref, xo_ref, ho_ref, *, n_lat):
    i = pl.program_id(0)

    def body(src):
        xs = _standardise(src[...])
        xo_ref[...] = xs
        ho_ref[...] = (xs * sc_ref[0] + sh_ref[0]).astype(BF16)

    @pl.when(i < n_lat)
    def _():
        body(x_ref)

    @pl.when(i >= n_lat)
    def _():
        body(ctx_ref)


def _init(x2, ctx2, sc_tbl, sh_tbl):
    n_lat_rows, d = x2.shape
    n_ctx_rows = ctx2.shape[0]
    n_lat = n_lat_rows // TE
    n_all = (n_lat_rows + n_ctx_rows) // TE
    tbl = pl.BlockSpec((1, 1, d), lambda i: ((i * TE) // TM, 0, 0))
    row = pl.BlockSpec((TE, d), lambda i: (i, 0))
    return pl.pallas_call(
        functools.partial(_init_kernel, n_lat=n_lat),
        grid=(n_all,),
        in_specs=[pl.BlockSpec((TE, d), lambda i: (jnp.minimum(i, n_lat - 1), 0)),
                  pl.BlockSpec((TE, d), lambda i: (jnp.maximum(i - n_lat, 0), 0)),
                  tbl, tbl],
        out_specs=[row, row],
        out_shape=[jax.ShapeDtypeStruct((n_lat_rows + n_ctx_rows, d), F32),
                   jax.ShapeDtypeStruct((n_lat_rows + n_ctx_rows, d), BF16)],
        compiler_params=_cp(("arbitrary",), 40),
        name="init_norm",
    )(x2, ctx2, sc_tbl, sh_tbl)


def _ln_kernel(x_ref, d_ref, gate_ref, lng_ref, lnb_ref, *rest, alpha, with_h):
    z = alpha * x_ref[...] + gate_ref[0] * d_ref[...].astype(F32)
    xn = _standardise(z) * lng_ref[...] + lnb_ref[...]
    if with_h:
        sc_ref, sh_ref, xo_ref, ho_ref = rest
        xo_ref[...] = xn
        ho_ref[...] = (xn * sc_ref[0] + sh_ref[0]).astype(BF16)
    else:
        (xo_ref,) = rest
        xo_ref[...] = xn


def _post_ln(x, delta, gate_tbl, ln_g, ln_b, sc_tbl, sh_tbl, *, n_rows, alpha, out_rows):
    d = x.shape[1]
    with_h = sc_tbl is not None
    tbl = pl.BlockSpec((1, 1, d), lambda i: ((i * TE) // TM, 0, 0))
    row = pl.BlockSpec((TE, d), lambda i: (i, 0))
    vec = pl.BlockSpec((1, d), lambda i: (0, 0))
    in_specs = [row, row, tbl, vec, vec]
    args = [x, delta, gate_tbl, ln_g.reshape(1, d), ln_b.reshape(1, d)]
    out_specs = [row]
    out_shape = [jax.ShapeDtypeStruct((out_rows, d), F32)]
    if with_h:
        in_specs += [tbl, tbl]
        args += [sc_tbl, sh_tbl]
        out_specs.append(row)
        out_shape.append(jax.ShapeDtypeStruct((out_rows, d), BF16))
    res = pl.pallas_call(
        functools.partial(_ln_kernel, alpha=alpha, with_h=with_h),
        grid=(n_rows // TE,),
        in_specs=in_specs, out_specs=out_specs, out_shape=out_shape,
        compiler_params=_cp(("arbitrary",), 40),
        name="post_ln",
    )(*args)
    return res if with_h else (res[0], None)


def _swap_select(x, near, far, bit):
    lane = lax.broadcasted_iota(jnp.int32, x.shape, 1)
    return jnp.where((lane & bit) == 0, pltpu.roll(x, far, 1), pltpu.roll(x, near, 1))


def _rope64(x, cos, sin):
    return x * cos + _swap_select(x, 16, LANE - 16, 16) * sin


def _rms(x, g):
    return x * lax.rsqrt(jnp.mean(x * x, axis=-1, keepdims=True) + NORM_EPS) * g


def _proj_kernel(h_ref, wa_ref, wb_ref, tab_ref, aqk_ref, mqkv_ref, up_ref, g_ref):
    j = pl.program_id(1)
    piece = h_ref.shape[0] // PROJ_ROW_SPLIT
    blk = lambda col: col // PROJ_TN

    def for_row_pieces(fn):
        for r in range(PROJ_ROW_SPLIT):
            fn(slice(r * piece, (r + 1) * piece))

    def block(pred, epilogue, width=PROJ_TN):
        @pl.when(pred)
        def _():
            w = wa_ref[:width, :].astype(BF16)

            def one(rows):
                u = _dot_nt(h_ref[rows, :], w)
                cols = [u[:, c * LANE:(c + 1) * LANE] for c in range(width // LANE)]
                outs = epilogue(cols, rows)
                for c, val in enumerate(outs):
                    up_ref[rows, c * LANE:(c + 1) * LANE] = val.astype(BF16)

            for_row_pieces(one)

    def rope(x, kind, near, rows):
        lane = lax.broadcasted_iota(jnp.int32, x.shape, 1)
        swapped = jnp.where((lane & near) == 0, pltpu.roll(x, LANE - near, 1), pltpu.roll(x, near, 1))
        return x * tab_ref[2 * kind, rows, :] + swapped * tab_ref[2 * kind + 1, rows, :]

    def a_heads(cols, rows):
        is_q = j == blk(COL_AQ)
        gain = aqk_ref[pl.ds(j - blk(COL_AQ), 1), :]
        scale = jnp.where(is_q, LOG2E * HEAD_DIM ** -0.5, 1.0)
        outs = []
        for c, x in enumerate(cols):
            y = rope(_rms(x, gain), TAB_ROPE128, 32, rows) * scale
            outs.append(y if c < KV_HEADS else jnp.where(is_q, y, x))
        return outs

    block((j == blk(COL_AQ)) | (j == blk(COL_AK)), a_heads)

    def rope_heads(cols, rows):
        wide = j < blk(COL_CQ)
        near = jnp.where(wide, 32, 16)
        kind = jnp.where(wide, TAB_ROPE128, TAB_ROPE64)
        scale = jnp.where(j == blk(COL_BQ), LOG2E * HEAD_DIM ** -0.5,
                          jnp.where(j == blk(COL_CQ), LOG2E * C_QK_DIM ** -0.5, 1.0))
        all_plain = j == blk(COL_CV)
        tail_plain = all_plain | (j == blk(COL_BK))
        outs = []
        for c, x in enumerate(cols):
            plain = all_plain if c < KV_HEADS else tail_plain
            outs.append(jnp.where(plain, x, rope(x, kind, near, rows) * scale))
        return outs

    block((j >= blk(COL_BQ)) & (j <= blk(COL_CV)), rope_heads)

    def latent_norm(cols, rows):
        gain = mqkv_ref[pl.ds(j - blk(COL_DQA), 1), :]
        ms = sum(jnp.sum(x * x, axis=-1, keepdims=True) for x in cols) * (1.0 / (len(cols) * LANE))
        inv = lax.rsqrt(ms + NORM_EPS)
        return [x * inv * gain[:, c * LANE:(c + 1) * LANE] for c, x in enumerate(cols)]

    block((j == blk(COL_DQA)) | (j == blk(COL_DKVA)), latent_norm)

    def rope_key(cols, rows):
        x = cols[0]
        lane = lax.broadcasted_iota(jnp.int32, x.shape, 1)
        return [jnp.where(lane < MLA_ROPE, rope(x, TAB_ROPE64, 16, rows), 0.0)]

    block(j == blk(COL_DKR), rope_key, width=LANE)

    @pl.when(j == blk(COL_DKR))
    def _():
        up_ref[:, LANE:] = jnp.zeros((up_ref.shape[0], PROJ_TN - LANE), BF16)

    @pl.when(j >= N_UP_BLOCKS)
    def _():
        w = jnp.concatenate([wa_ref[GATE_ROW_OFF:, :], wb_ref[...]], axis=0).astype(BF16)

        def one(rows):
            g = _dot_nt(h_ref[rows, :], w)
            g_ref[rows, :] = (0.5 * jnp.tanh(0.5 * g) + 0.5).astype(BF16)

        for_row_pieces(one)


def _row_tile(n, choices):
    return next(t for t in choices if n % t == 0)


def _proj(h, w_in_t, tabs, aqk, mq, mkv, *, layer):
    n, d = h.shape
    tp = _row_tile(n, PROJ_ROW_TILES)
    nt = n // tp
    n_gate = w_in_t.shape[1] - QKV_W
    gate_blocks = n_gate // PROJ_TN
    first_gate_blk = (QKV_W - GATE_ROW_OFF) // PROJ_TN
    assert (QKV_W - GATE_ROW_OFF) % PROJ_TN == 0 and n_gate % PROJ_TN == 0 and first_gate_blk == N_UP_BLOCKS - 1

    def wa_idx(i, j):
        return (layer, jnp.where(j < N_UP_BLOCKS, j, j - 1), 0)

    def wb_idx(i, j):
        g = jnp.maximum(j - N_UP_BLOCKS, 0)
        return (layer, (QKV_W - GATE_ROW_OFF + (g + 1) * PROJ_TN) // GATE_ROW_OFF, 0)

    return pl.pallas_call(
        _proj_kernel,
        grid=(nt, N_UP_BLOCKS + gate_blocks),
        in_specs=[pl.BlockSpec((tp, d), lambda i, j: (i, 0)),
                  pl.BlockSpec((None, PROJ_TN, d), wa_idx),
                  pl.BlockSpec((None, GATE_ROW_OFF, d), wb_idx),
                  pl.BlockSpec((N_TAB_PLANES, tp, LANE), lambda i, j: (0, i, 0)),
                  pl.BlockSpec((2, LANE), lambda i, j: (0, 0)),
                  pl.BlockSpec((2, MLA_RANK), lambda i, j: (0, 0))],
        out_specs=[pl.BlockSpec((tp, PROJ_TN), lambda i, j: (i, jnp.minimum(j, N_UP_BLOCKS - 1))),
                   pl.BlockSpec((tp, PROJ_TN), lambda i, j: (i, jnp.maximum(j - N_UP_BLOCKS, 0)))],
        out_shape=[jax.ShapeDtypeStruct((n, UP_W), BF16),
                   jax.ShapeDtypeStruct((n, n_gate), BF16)],
        compiler_params=_cp(("arbitrary", "arbitrary"), 58),
        name="proj",
    )(h, w_in_t, w_in_t, tabs, aqk, jnp.stack([mq, mkv]))


def _mla_up_kernel(qa_ref, kva_ref, kr_ref, wq_ref, wkv_ref, tab_ref, dq_ref, dk_ref, dv_ref):
    cos, sin = tab_ref[2 * TAB_ROPE64], tab_ref[2 * TAB_ROPE64 + 1]
    q = _dot(qa_ref[...], wq_ref[...].astype(BF16))
    kv = _dot(kva_ref[...], wkv_ref[...].astype(BF16))
    scale = LOG2E * (MLA_NOPE + MLA_ROPE) ** -0.5
    kr = kr_ref[...]
    for hh in range(N_HEADS):
        base = hh * MLA_QK_PAD
        dq_ref[:, base:base + LANE] = (q[:, base:base + LANE] * scale).astype(BF16)
        dq_ref[:, base + LANE:base + 2 * LANE] = (
            _rope64(q[:, base + LANE:base + 2 * LANE], cos, sin) * scale).astype(BF16)
        dk_ref[:, base:base + LANE] = kv[:, hh * LANE:(hh + 1) * LANE].astype(BF16)
        dk_ref[:, base + LANE:base + 2 * LANE] = kr
    dv_ref[...] = kv[:, N_HEADS * LANE:].astype(BF16)


def _mla_up(up, wq_pad, wkv_perm, tabs, *, layer):
    n = up.shape[0]
    nt = n // TM
    wide = N_HEADS * MLA_QK_PAD
    return pl.pallas_call(
        _mla_up_kernel,
        grid=(nt,),
        in_specs=[pl.BlockSpec((TM, MLA_RANK), lambda i: (i, COL_DQA // MLA_RANK)),
                  pl.BlockSpec((TM, MLA_RANK), lambda i: (i, COL_DKVA // MLA_RANK)),
                  pl.BlockSpec((TM, LANE), lambda i: (i, COL_DKR // LANE)),
                  pl.BlockSpec((None, MLA_RANK, wide), lambda i: (layer, 0, 0)),
                  pl.BlockSpec((None, MLA_RANK, wide), lambda i: (layer, 0, 0)),
                  pl.BlockSpec((N_TAB_PLANES, TM, LANE), lambda i: (0, i, 0))],
        out_specs=[pl.BlockSpec((TM, wide), lambda i: (i, 0)),
                   pl.BlockSpec((TM, wide), lambda i: (i, 0)),
                   pl.BlockSpec((TM, N_HEADS * LANE), lambda i: (i, 0))],
        out_shape=[jax.ShapeDtypeStruct((n, wide), BF16),
                   jax.ShapeDtypeStruct((n, wide), BF16),
                   jax.ShapeDtypeStruct((n, N_HEADS * LANE), BF16)],
        compiler_params=_cp(("arbitrary",), 48),
        name="mla_up",
    )(up, up, up, wq_pad, wkv_perm, tabs)


def _attend(q, pieces, extra=None):
    scores = []
    for k, _, mask in pieces:
        s = _dot_nt(q, k)
        if mask is not None:
            s = jnp.where(mask, s, NEG_INF)
        scores.append(s)
    m = functools.reduce(jnp.maximum, [jnp.max(s, axis=-1, keepdims=True) for s in scores])
    if extra is not None:
        m = jnp.maximum(m, extra)
    den = None
    out = None
    for s, (_, v, _) in zip(scores, pieces):
        e = jnp.exp2(s - m)
        part = jnp.sum(e, axis=-1, keepdims=True)
        pv = _dot(e.astype(BF16), v)
        den = part if den is None else den + part
        out = pv if out is None else out + pv
    if extra is not None:
        den = den + jnp.exp2(extra - m)
    return out / den


def _heads(ref, k, width=LANE):
    return ref[:, k * width:(k + 1) * width]


def _query_steps(run, q_ref, o_ref, qc_ref, oc_ref, nq):
    t = pl.program_id(1)
    if qc_ref is None:
        run(q_ref, o_ref, True)
        return

    @pl.when(t < nq)
    def _():
        run(q_ref, o_ref, True)

    @pl.when(t >= nq)
    def _():
        run(qc_ref, oc_ref, False)


def _split_refs(refs, n_extra):
    with_ctx = len(refs) == 8 + n_extra
    q, rest = refs[0], refs[1:]
    qc = None
    if with_ctx:
        qc, rest = rest[0], rest[1:]
    kl, vl, kc, vc = rest[:4]
    extras = rest[4:4 + n_extra]
    outs = rest[4 + n_extra:]
    return q, qc, kl, vl, kc, vc, extras, outs[0], (outs[1] if with_ctx else None)


def _attn_a_kernel(*refs, nq):
    q_ref, qc_ref, kl_ref, vl_ref, kc_ref, vc_ref, _, o_ref, oc_ref = _split_refs(refs, 0)

    def run(q_ref, o_ref, with_latent):
        for hh in range(N_HEADS):
            kvh = hh // (N_HEADS // KV_HEADS)
            pieces = [(_heads(kc_ref, kvh), _heads(vc_ref, kvh), None)]
            if with_latent:
                pieces.append((_heads(kl_ref, kvh), _heads(vl_ref, kvh), None))
            o_ref[:, hh * LANE:(hh + 1) * LANE] = _attend(_heads(q_ref, hh), pieces).astype(BF16)

    _query_steps(run, q_ref, o_ref, qc_ref, oc_ref, nq)


def _attn_b_kernel(*refs, nq, seq, win_k):
    q_ref, qc_ref, kl_ref, vl_ref, kc_ref, vc_ref, (sink_ref,), o_ref, oc_ref = _split_refs(refs, 1)
    t = pl.program_id(1)

    def run(q_ref, o_ref, with_latent):
        rows = q_ref.shape[0]
        if with_latent:
            q0 = t * rows
            start = pl.multiple_of(jnp.clip(q0 - WINDOW, 0, seq - win_k), LANE)
            r = lax.broadcasted_iota(jnp.int32, (rows, win_k), 0)
            c = lax.broadcasted_iota(jnp.int32, (rows, win_k), 1)
            mask = jnp.abs(q0 + r - (start + c)) <= WINDOW
        for hh in range(N_HEADS):
            kvh = hh // (N_HEADS // KV_HEADS)
            sink = LOG2E * sink_ref[0:1, hh:hh + 1]
            pieces = [(_heads(kc_ref, kvh), _heads(vc_ref, kvh), None)]
            if with_latent:
                kw = kl_ref[pl.ds(start, win_k), kvh * LANE:(kvh + 1) * LANE]
                vw = vl_ref[pl.ds(start, win_k), kvh * LANE:(kvh + 1) * LANE]
                pieces.append((kw, vw, mask))
            o_ref[:, hh * LANE:(hh + 1) * LANE] = _attend(_heads(q_ref, hh), pieces, extra=sink).astype(BF16)

    _query_steps(run, q_ref, o_ref, qc_ref, oc_ref, nq)


def _attn_c_kernel(*refs, nq, lam_init):
    q_ref, qc_ref, kl_ref, vl_ref, kc_ref, vc_ref, (lam_ref, g_ref), o_ref, oc_ref = _split_refs(refs, 2)
    lam = (jnp.exp(jnp.sum(lam_ref[0:1, :] * lam_ref[1:2, :], axis=-1, keepdims=True))
           - jnp.exp(jnp.sum(lam_ref[2:3, :] * lam_ref[3:4, :], axis=-1, keepdims=True)) + lam_init)

    def run(q_ref, o_ref, with_latent):
        rows = q_ref.shape[0]
        lane = lax.broadcasted_iota(jnp.int32, (rows, LANE), 1)
        for hh in range(N_HEADS):
            q = _heads(q_ref, hh)
            zero = jnp.zeros_like(q)
            pieces = [(_heads(kc_ref, hh), _heads(vc_ref, hh), None)]
            if with_latent:
                pieces.append((_heads(kl_ref, hh), _heads(vl_ref, hh), None))
            o1 = _attend(jnp.where(lane < C_QK_DIM, q, zero), pieces)
            o2 = _attend(jnp.where(lane >= C_QK_DIM, q, zero), pieces)
            o = _rms(o1 - lam * o2, g_ref[...]) * (1.0 - lam_init)
            o_ref[:, hh * LANE:(hh + 1) * LANE] = o.astype(BF16)

    _query_steps(run, q_ref, o_ref, qc_ref, oc_ref, nq)


def _attn_d_kernel(*refs, nq):
    q_ref, qc_ref, kl_ref, vl_ref, kc_ref, vc_ref, _, o_ref, oc_ref = _split_refs(refs, 0)

    def run(q_ref, o_ref, with_latent):
        for hh in range(N_HEADS):
            pieces = [(_heads(kc_ref, hh, MLA_QK_PAD), _heads(vc_ref, hh), None)]
            if with_latent:
                pieces.append((_heads(kl_ref, hh, MLA_QK_PAD), _heads(vl_ref, hh), None))
            q = _heads(q_ref, hh, MLA_QK_PAD)
            o_ref[:, hh * LANE:(hh + 1) * LANE] = _attend(q, pieces).astype(BF16)

    _query_steps(run, q_ref, o_ref, qc_ref, oc_ref, nq)


def _attention(kern, name, q_src, k_src, v_src, extras, *, batch, seq, ctx, with_ctx_queries,
               q_w, q_blk, k_w, k_blk, v_w, v_blk, tq=TQ):
    tq = min(tq, seq)
    nq = seq // tq
    steps = nq + (1 if with_ctx_queries else 0)
    ctx_blk0 = batch * seq // ctx

    def qrow(b, t):
        return b * nq + jnp.minimum(t, nq - 1)

    in_specs = [pl.BlockSpec((tq, q_w), lambda b, t: (qrow(b, t), q_blk))]
    args = [q_src]
    if with_ctx_queries:
        in_specs.append(pl.BlockSpec((ctx, q_w), lambda b, t: (ctx_blk0 + b, q_blk)))
        args.append(q_src)
    in_specs += [pl.BlockSpec((seq, k_w), lambda b, t: (b, k_blk)),
                 pl.BlockSpec((seq, v_w), lambda b, t: (b, v_blk)),
                 pl.BlockSpec((ctx, k_w), lambda b, t: (ctx_blk0 + b, k_blk)),
                 pl.BlockSpec((ctx, v_w), lambda b, t: (ctx_blk0 + b, v_blk))]
    args += [k_src, v_src, k_src, v_src]
    for e in extras:
        in_specs.append(pl.BlockSpec(e.shape, lambda b, t: (0, 0)))
        args.append(e)
    out_specs = [pl.BlockSpec((tq, BRANCH_W), lambda b, t: (qrow(b, t), 0))]
    out_shape = [jax.ShapeDtypeStruct((batch * seq, BRANCH_W), BF16)]
    if with_ctx_queries:
        out_specs.append(pl.BlockSpec((ctx, BRANCH_W), lambda b, t: (b, 0)))
        out_shape.append(jax.ShapeDtypeStruct((batch * ctx, BRANCH_W), BF16))
    res = pl.pallas_call(
        functools.partial(kern, nq=nq),
        grid=(batch, steps),
        in_specs=in_specs, out_specs=out_specs, out_shape=out_shape,
        compiler_params=_cp(("arbitrary", "arbitrary"), 56),
        name=name,
    )(*args)
    return (res[0], res[1]) if with_ctx_queries else (res[0], None)


def _merge_kernel(*refs, n_lat_tiles, with_ctx):
    n_src = 2 * N_BRANCH if with_ctx else N_BRANCH
    lat_refs, ctx_refs = refs[:N_BRANCH], refs[N_BRANCH:n_src]
    g_ref, w_ref, y_ref = refs[n_src:]
    d = y_ref.shape[1]

    def body(o_refs):
        for c in range(d // MERGE_TN):
            cols = slice(c * MERGE_TN, (c + 1) * MERGE_TN)
            acc = None
            for m, o_ref in enumerate(o_refs):
                gate = g_ref[:, m * d + c * MERGE_TN:m * d + (c + 1) * MERGE_TN].astype(F32)
                p = gate * _dot(o_ref[...], w_ref[m, :, cols])
                acc = p if acc is None else acc + p
            y_ref[:, cols] = acc.astype(BF16)

    if not with_ctx:
        body(lat_refs)
        return
    i = pl.program_id(0)

    @pl.when(i < n_lat_tiles)
    def _():
        body(lat_refs)

    @pl.when(i >= n_lat_tiles)
    def _():
        body(ctx_refs)


def _merge(lat_outs, ctx_outs, gates, w_branch_bf16, *, layer, n_lat_rows, n_rows):
    d = w_branch_bf16.shape[3]
    n_lat_tiles = n_lat_rows // TO
    with_ctx = ctx_outs is not None
    srcs = list(lat_outs) + (list(ctx_outs) if with_ctx else [])
    specs = [pl.BlockSpec((TO, BRANCH_W), lambda i: (jnp.minimum(i, n_lat_tiles - 1), 0))] * N_BRANCH
    if with_ctx:
        specs += [pl.BlockSpec((TO, BRANCH_W), lambda i: (jnp.maximum(i - n_lat_tiles, 0), 0))] * N_BRANCH
    return pl.pallas_call(
        functools.partial(_merge_kernel, n_lat_tiles=n_lat_tiles, with_ctx=with_ctx),
        grid=(n_rows // TO,),
        in_specs=specs + [pl.BlockSpec((TO, N_BRANCH * d), lambda i: (i, 0)),
                          pl.BlockSpec((None, N_BRANCH, BRANCH_W, d), lambda i: (layer, 0, 0, 0))],
        out_specs=pl.BlockSpec((TO, d), lambda i: (i, 0)),
        out_shape=jax.ShapeDtypeStruct((n_rows, d), BF16),
        compiler_params=_cp(("arbitrary",), 56),
        name="merge",
    )(*srcs, gates, w_branch_bf16)


def _out_ln_kernel(y_ref, w_ref, x_ref, gate_ref, lng_ref, lnb_ref, sc_ref, sh_ref, xo_ref, ho_ref, *, alpha):
    piece = y_ref.shape[0] // 2
    for r in range(2):
        rows = slice(r * piece, (r + 1) * piece)
        mix = _dot(y_ref[rows, :], w_ref[...])
        z = alpha * x_ref[rows, :] + gate_ref[0] * mix
        xn = _standardise(z) * lng_ref[...] + lnb_ref[...]
        xo_ref[rows, :] = xn
        ho_ref[rows, :] = (xn * sc_ref[0] + sh_ref[0]).astype(BF16)


def _out_ln(y, w_out_bf16, x, gate_tbl, ln_g, ln_b, sc_tbl, sh_tbl, *, layer, n_rows, alpha):
    d = x.shape[1]
    tbl = pl.BlockSpec((1, 1, d), lambda i: ((i * TO) // TM, 0, 0))
    row = pl.BlockSpec((TO, d), lambda i: (i, 0))
    vec = pl.BlockSpec((1, d), lambda i: (0, 0))
    return pl.pallas_call(
        functools.partial(_out_ln_kernel, alpha=alpha),
        grid=(n_rows // TO,),
        in_specs=[row, pl.BlockSpec((None, d, d), lambda i: (layer, 0, 0)), row, tbl, vec, vec, tbl, tbl],
        out_specs=[row, row],
        out_shape=[jax.ShapeDtypeStruct((n_rows, d), F32), jax.ShapeDtypeStruct((n_rows, d), BF16)],
        compiler_params=_cp(("arbitrary",), 56),
        name="out_ln",
    )(y, w_out_bf16, x, gate_tbl, ln_g.reshape(1, d), ln_b.reshape(1, d), sc_tbl, sh_tbl)


def _ffn_kernel(h_ref, wg_ref, wu_ref, wo_ref, o_ref, acc_ref):
    f = pl.program_id(1)

    @pl.when(f == 0)
    def _():
        acc_ref[...] = jnp.zeros_like(acc_ref)

    h = h_ref[...]
    g = _dot(h, wg_ref[...].astype(BF16))
    u = _dot(h, wu_ref[...].astype(BF16))
    a = (g * (0.5 * jnp.tanh(0.5 * g) + 0.5) * u).astype(BF16)

    for n in range(acc_ref.shape[1] // FFN_ACC_TN):
        cols = slice(n * FFN_ACC_TN, (n + 1) * FFN_ACC_TN)
        acc_ref[:, cols] += _dot(a, wo_ref[:, cols].astype(BF16))

    @pl.when(f == pl.num_programs(1) - 1)
    def _():
        o_ref[...] = acc_ref[...].astype(BF16)


def _ffn(h, w_ffn_in, w_ffn_out, *, layer, n_tiles):
    d = h.shape[1]
    n = n_tiles * TM
    hidden = w_ffn_out.shape[1]
    tf = 512
    nf = hidden // tf
    return pl.pallas_call(
        _ffn_kernel,
        grid=(n_tiles, nf),
        in_specs=[pl.BlockSpec((TM, d), lambda i, f: (i, 0)),
                  pl.BlockSpec((None, d, tf), lambda i, f: (layer, 0, f)),
                  pl.BlockSpec((None, d, tf), lambda i, f: (layer, 0, nf + f)),
                  pl.BlockSpec((None, tf, d), lambda i, f: (layer, f, 0))],
        out_specs=pl.BlockSpec((TM, d), lambda i, f: (i, 0)),
        out_shape=jax.ShapeDtypeStruct((n, d), BF16),
        scratch_shapes=[pltpu.VMEM((TM, d), F32)],
        compiler_params=_cp(("arbitrary", "arbitrary"), 60),
        name="ffn",
    )(h, w_ffn_in, w_ffn_in, w_ffn_out)


def _rope_tables(seq, batch, ctx_rows):
    t = jnp.arange(seq, dtype=jnp.int32)
    pos_row = (t // GRID_W).astype(F32)[:, None]
    pos_col = (t % GRID_W).astype(F32)[:, None]

    def one(rot_dim, reps):
        axis_dim = rot_dim // 2
        freqs = ROPE_THETA ** (-jnp.arange(0, axis_dim, 2, dtype=F32) / axis_dim)
        ar, ac = pos_row * freqs[None, :], pos_col * freqs[None, :]
        cos = jnp.concatenate([jnp.cos(ar), jnp.cos(ar), jnp.cos(ac), jnp.cos(ac)], axis=-1)
        sin = jnp.concatenate([-jnp.sin(ar), jnp.sin(ar), -jnp.sin(ac), jnp.sin(ac)], axis=-1)
        return jnp.tile(cos, (1, reps)), jnp.tile(sin, (1, reps))

    c128, s128 = one(HEAD_DIM, 1)
    c64, s64 = one(C_QK_DIM, 2)
    ones = jnp.ones((ctx_rows, LANE), F32)
    zeros = jnp.zeros((ctx_rows, LANE), F32)

    def rows(tab, pad):
        return jnp.concatenate([jnp.tile(tab, (batch, 1)), pad])

    return jnp.stack([rows(c128, ones), rows(s128, zeros), rows(c64, ones), rows(s64, zeros)])


def kernel(x, c, ctx, c_ctx, w_ada, b_ada, w_in, a_qk_norm, b_sink, c_lambda, c_subln, mla_q_norm, mla_w_q_up, mla_kv_norm, mla_w_kv_up, w_branch, w_out, ln_mix_g, ln_mix_b, w_ffn_in, w_ffn_out, ln_ffn_g, ln_ffn_b):
    batch, seq, d = x.shape
    ctx_len = ctx.shape[1]
    depth = w_ada.shape[0]
    assert d == D_MODEL and seq % TM == 0 and (batch * ctx_len) % TM == 0 and batch + 1 <= 8
    n_lat_rows = batch * seq
    n_rows = n_lat_rows + batch * ctx_len
    tiles_per_seq = seq // TM
    n_lat_tiles = n_lat_rows // TM
    n_tiles = n_rows // TM
    alpha = (2 * depth) ** 0.25

    c_all = jnp.concatenate([c, c_ctx[None, :], jnp.zeros((8 - batch - 1, d), F32)], axis=0)
    ada = _ada_all(c_all, w_ada, b_ada)
    tile_row = np.array([b for b in range(batch) for _ in range(tiles_per_seq)]
                        + [batch] * (n_tiles - n_lat_tiles), np.int32)

    def table(l, chunk, plus_one=False):
        v = ada[l, :, chunk * d:(chunk + 1) * d][tile_row]
        return ((1.0 + v) if plus_one else v)[:, None, :]

    tabs = _rope_tables(seq, batch, batch * ctx_len)
    x_all, h = _init(x.reshape(n_lat_rows, d), ctx.reshape(batch * ctx_len, d),
                     table(0, 1, True), table(0, 0))

    win_k = min(TQ + 2 * WINDOW, seq)
    sink_pad = jnp.zeros((depth, 8, LANE), F32).at[:, 0, :N_HEADS].set(b_sink)
    wq_pad = jnp.pad(mla_w_q_up.reshape(depth, MLA_RANK, N_HEADS, MLA_NOPE + MLA_ROPE),
                     ((0, 0), (0, 0), (0, 0), (0, MLA_QK_PAD - MLA_NOPE - MLA_ROPE))
                     ).reshape(depth, MLA_RANK, N_HEADS * MLA_QK_PAD)
    wkv_pad = jnp.pad(mla_w_kv_up.reshape(depth, MLA_RANK, N_HEADS, 2, LANE).transpose(0, 1, 3, 2, 4)
                      .reshape(depth, MLA_RANK, 2 * N_HEADS * LANE),
                      ((0, 0), (0, 0), (0, N_HEADS * MLA_QK_PAD - 2 * N_HEADS * LANE)))

    w_in_t = jnp.swapaxes(w_in, 1, 2)
    w_out_bf16 = w_out.astype(BF16)
    w_branch_bf16 = w_branch.astype(BF16)

    for l in range(depth):
        last = l == depth - 1
        act_tiles = n_lat_tiles if last else n_tiles
        act_rows = act_tiles * TM
        lam_init = 0.8 - 0.6 * math.exp(-0.3 * l)

        up, gates = _proj(h, w_in_t, tabs, a_qk_norm[l], mla_q_norm[l], mla_kv_norm[l], layer=l)
        dq, dk, dv = _mla_up(up, wq_pad, wkv_pad, tabs, layer=l)

        common = dict(batch=batch, seq=seq, ctx=ctx_len, with_ctx_queries=not last)
        o_a = _attention(_attn_a_kernel, "attn_a", up, up, up, [], **common, tq=TQ_TALL,
                         q_w=512, q_blk=COL_AQ // 512, k_w=256, k_blk=COL_AK // 256, v_w=256, v_blk=COL_AV // 256)
        o_b = _attention(functools.partial(_attn_b_kernel, seq=seq, win_k=win_k), "attn_b", up, up, up,
                         [sink_pad[l]], **common,
                         q_w=512, q_blk=COL_BQ // 512, k_w=256, k_blk=COL_BK // 256, v_w=256, v_blk=COL_BV // 256)
        o_c = _attention(functools.partial(_attn_c_kernel, lam_init=lam_init), "attn_c", up, up, up,
                         [c_lambda[l], c_subln[l].reshape(1, LANE)], **common,
                         q_w=512, q_blk=COL_CQ // 512, k_w=512, k_blk=COL_CK // 512, v_w=512, v_blk=COL_CV // 512)
        o_d = _attention(_attn_d_kernel, "attn_d", dq, dk, dv, [], **common, tq=TQ_TALL,
                         q_w=N_HEADS * MLA_QK_PAD, q_blk=0, k_w=N_HEADS * MLA_QK_PAD, k_blk=0,
                         v_w=N_HEADS * LANE, v_blk=0)

        lat_outs, ctx_outs = zip(o_a, o_b, o_c, o_d)
        y = _merge(lat_outs, None if last else ctx_outs, gates, w_branch_bf16,
                   layer=l, n_lat_rows=n_lat_rows, n_rows=act_rows)
        x_all, h2 = _out_ln(y, w_out_bf16, x_all, table(l, 2), ln_mix_g[l], ln_mix_b[l],
                            table(l, 4, True), table(l, 3), layer=l, n_rows=act_rows, alpha=alpha)
        ffn = _ffn(h2, w_ffn_in, w_ffn_out, layer=l, n_tiles=act_tiles)
        if last:
            x_all, _ = _post_ln(x_all, ffn, table(l, 5), ln_ffn_g[l], ln_ffn_b[l], None, None,
                                n_rows=act_rows, alpha=alpha, out_rows=n_lat_rows)
        else:
            x_all, h = _post_ln(x_all, ffn, table(l, 5), ln_ffn_g[l], ln_ffn_b[l],
                                table(l + 1, 1, True), table(l + 1, 0),
                                n_rows=act_rows, alpha=alpha, out_rows=n_rows)
    return x_all.reshape(batch, seq, d)
```

```python
import functools
import math

import numpy as np
import jax
import jax.numpy as jnp
from jax import lax
from jax.experimental import pallas as pl
from jax.experimental.pallas import tpu as pltpu

F32 = jnp.float32
BF16 = jnp.bfloat16

D_MODEL = 2048
GRID_W = 64
HEAD_DIM = 128
WINDOW = 128
ROPE_THETA = 10000.0
NORM_EPS = 1e-6
NEG_INF = -1e30
LOG2E = math.log2(math.e)
N_HEADS = 4
KV_HEADS = 2
C_QK_DIM = 64
MLA_RANK = 512
MLA_NOPE = 128
MLA_ROPE = 64
MLA_QK_PAD = 256
BRANCH_W = 512
N_BRANCH = 4
QKV_W = 4672
LANE = 128
GATE_ROW_OFF = QKV_W % LANE

UP_W = 5120
PROJ_TN = 512
N_UP_BLOCKS = UP_W // PROJ_TN
GATE_ROW_SPLIT = 2
PROJ_ROW_SPLIT = 4
COL_AQ, COL_AK, COL_AV = 0, 512, 768
COL_BQ, COL_BK, COL_BV = 1024, 1536, 1792
COL_CQ, COL_CK, COL_CV = 2048, 2560, 3072
COL_DQA, COL_DKVA, COL_DKR = 3584, 4096, 4608

TM = 1024
PROJ_ROW_TILES = (1536, 1024, 512)
TAB_ROPE128, TAB_ROPE64 = 0, 1
N_TAB_PLANES = 4
TO = 512
MERGE_TN = 512
TE = 512
TQ = 512
TQ_TALL = 1024
FFN_ACC_TN = 256
MIB = 1024 * 1024


def _cp(sem, vmem_mib):
    return pltpu.CompilerParams(dimension_semantics=sem, vmem_limit_bytes=vmem_mib * MIB)


def _dot(a, b):
    return jnp.dot(a, b, preferred_element_type=F32)


def _dot_nt(a, b):
    return lax.dot_general(a, b, (((1,), (1,)), ((), ())), preferred_element_type=F32)


def _ada_kernel(c_ref, w_ref, b_ref, o_ref):
    k = pl.program_id(1)
    c = c_ref[...]
    a = (c * (0.5 * jnp.tanh(0.5 * c) + 0.5)).astype(BF16)
    part = _dot(a, w_ref[...].astype(BF16))

    @pl.when(k == 0)
    def _():
        o_ref[...] = part + b_ref[...]

    @pl.when(k > 0)
    def _():
        o_ref[...] += part


def _ada_all(c_all, w_ada, b_ada):
    depth, d, n6 = w_ada.shape
    tk = 128
    return pl.pallas_call(
        _ada_kernel,
        grid=(depth, d // tk),
        in_specs=[pl.BlockSpec((8, tk), lambda l, k: (0, k)),
                  pl.BlockSpec((None, tk, n6), lambda l, k: (l, k, 0)),
                  pl.BlockSpec((None, 1, n6), lambda l, k: (l, 0, 0))],
        out_specs=pl.BlockSpec((None, 8, n6), lambda l, k: (l, 0, 0)),
        out_shape=jax.ShapeDtypeStruct((depth, 8, n6), F32),
        compiler_params=_cp(("arbitrary", "arbitrary"), 40),
        name="ada",
    )(c_all, w_ada, b_ada.reshape(depth, 1, n6))


def _standardise(z):
    mu = jnp.mean(z, axis=-1, keepdims=True)
    zc = z - mu
    var = jnp.mean(zc * zc, axis=-1, keepdims=True)
    return zc * lax.rsqrt(var + NORM_EPS)


def _init_kernel(x_ref, ctx_ref, sc_ref, sh_ref, xo_ref, ho_ref, *, n_lat):
    i = pl.program_id(0)

    def body(src):
        xs = _standardise(src[...])
        xo_ref[...] = xs
        ho_ref[...] = (xs * sc_ref[0] + sh_ref[0]).astype(BF16)

    @pl.when(i < n_lat)
    def _():
        body(x_ref)

    @pl.when(i >= n_lat)
    def _():
        body(ctx_ref)


def _init(x2, ctx2, sc_tbl, sh_tbl):
    n_lat_rows, d = x2.shape
    n_ctx_rows = ctx2.shape[0]
    n_lat = n_lat_rows // TE
    n_all = (n_lat_rows + n_ctx_rows) // TE
    tbl = pl.BlockSpec((1, 1, d), lambda i: ((i * TE) // TM, 0, 0))
    row = pl.BlockSpec((TE, d), lambda i: (i, 0))
    return pl.pallas_call(
        functools.partial(_init_kernel, n_lat=n_lat),
        grid=(n_all,),
        in_specs=[pl.BlockSpec((TE, d), lambda i: (jnp.minimum(i, n_lat - 1), 0)),
                  pl.BlockSpec((TE, d), lambda i: (jnp.maximum(i - n_lat, 0), 0)),
                  tbl, tbl],
        out_specs=[row, row],
        out_shape=[jax.ShapeDtypeStruct((n_lat_rows + n_ctx_rows, d), F32),
                   jax.ShapeDtypeStruct((n_lat_rows + n_ctx_rows, d), BF16)],
        compiler_params=_cp(("arbitrary",), 40),
        name="init_norm",
    )(x2, ctx2, sc_tbl, sh_tbl)


def _ln_kernel(x_ref, d_ref, gate_ref, lng_ref, lnb_ref, *rest, alpha, with_h):
    z = alpha * x_ref[...] + gate_ref[0] * d_ref[...].astype(F32)
    xn = _standardise(z) * lng_ref[...] + lnb_ref[...]
    if with_h:
        sc_ref, sh_ref, xo_ref, ho_ref = rest
        xo_ref[...] = xn
        ho_ref[...] = (xn * sc_ref[0] + sh_ref[0]).astype(BF16)
    else:
        (xo_ref,) = rest
        xo_ref[...] = xn


def _post_ln(x, delta, gate_tbl, ln_g, ln_b, sc_tbl, sh_tbl, *, n_rows, alpha, out_rows):
    d = x.shape[1]
    with_h = sc_tbl is not None
    tbl = pl.BlockSpec((1, 1, d), lambda i: ((i * TE) // TM, 0, 0))
    row = pl.BlockSpec((TE, d), lambda i: (i, 0))
    vec = pl.BlockSpec((1, d), lambda i: (0, 0))
    in_specs = [row, row, tbl, vec, vec]
    args = [x, delta, gate_tbl, ln_g.reshape(1, d), ln_b.reshape(1, d)]
    out_specs = [row]
    out_shape = [jax.ShapeDtypeStruct((out_rows, d), F32)]
    if with_h:
        in_specs += [tbl, tbl]
        args += [sc_tbl, sh_tbl]
        out_specs.append(row)
        out_shape.append(jax.ShapeDtypeStruct((out_rows, d), BF16))
    res = pl.pallas_call(
        functools.partial(_ln_kernel, alpha=alpha, with_h=with_h),
        grid=(n_rows // TE,),
        in_specs=in_specs, out_specs=out_specs, out_shape=out_shape,
        compiler_params=_cp(("arbitrary",), 40),
        name="post_ln",
    )(*args)
    return res if with_h else (res[0], None)


def _swap_select(x, near, far, bit):
    lane = lax.broadcasted_iota(jnp.int32, x.shape, 1)
    return jnp.where((lane & bit) == 0, pltpu.roll(x, far, 1), pltpu.roll(x, near, 1))


def _rope64(x, cos, sin):
    return x * cos + _swap_select(x, 16, LANE - 16, 16) * sin


def _rms(x, g):
    return x * lax.rsqrt(jnp.mean(x * x, axis=-1, keepdims=True) + NORM_EPS) * g


def _proj_kernel(h_ref, wa_ref, wb_ref, tab_ref, aqk_ref, mqkv_ref, up_ref, g_ref):
    j = pl.program_id(1)
    blk = lambda col: col // PROJ_TN

    def for_row_pieces(fn, split=PROJ_ROW_SPLIT):
        piece = h_ref.shape[0] // split
        for r in range(split):
            fn(slice(r * piece, (r + 1) * piece))

    def block(pred, epilogue, width=PROJ_TN):
        @pl.when(pred)
        def _():
            w = wa_ref[:width, :].astype(BF16)

            def one(rows):
                u = _dot_nt(h_ref[rows, :], w)
                cols = [u[:, c * LANE:(c + 1) * LANE] for c in range(width // LANE)]
                outs = epilogue(cols, rows)
                for c, val in enumerate(outs):
                    up_ref[rows, c * LANE:(c + 1) * LANE] = val.astype(BF16)

            for_row_pieces(one)

    def rope(x, kind, near, rows):
        lane = lax.broadcasted_iota(jnp.int32, x.shape, 1)
        swapped = jnp.where((lane & near) == 0, pltpu.roll(x, LANE - near, 1), pltpu.roll(x, near, 1))
        return x * tab_ref[2 * kind, rows, :] + swapped * tab_ref[2 * kind + 1, rows, :]

    def a_heads(cols, rows):
        is_q = j == blk(COL_AQ)
        gain = aqk_ref[pl.ds(j - blk(COL_AQ), 1), :]
        scale = jnp.where(is_q, LOG2E * HEAD_DIM ** -0.5, 1.0)
        outs = []
        for c, x in enumerate(cols):
            y = rope(_rms(x, gain), TAB_ROPE128, 32, rows) * scale
            outs.append(y if c < KV_HEADS else jnp.where(is_q, y, x))
        return outs

    block((j == blk(COL_AQ)) | (j == blk(COL_AK)), a_heads)

    def rope_heads(cols, rows):
        wide = j < blk(COL_CQ)
        near = jnp.where(wide, 32, 16)
        kind = jnp.where(wide, TAB_ROPE128, TAB_ROPE64)
        scale = jnp.where(j == blk(COL_BQ), LOG2E * HEAD_DIM ** -0.5,
                          jnp.where(j == blk(COL_CQ), LOG2E * C_QK_DIM ** -0.5, 1.0))
        all_plain = j == blk(COL_CV)
        tail_plain = all_plain | (j == blk(COL_BK))
        outs = []
        for c, x in enumerate(cols):
            plain = all_plain if c < KV_HEADS else tail_plain
            outs.append(jnp.where(plain, x, rope(x, kind, near, rows) * scale))
        return outs

    block((j >= blk(COL_BQ)) & (j <= blk(COL_CV)), rope_heads)

    def latent_norm(cols, rows):
        gain = mqkv_ref[pl.ds(j - blk(COL_DQA), 1), :]
        ms = sum(jnp.sum(x * x, axis=-1, keepdims=True) for x in cols) * (1.0 / (len(cols) * LANE))
        inv = lax.rsqrt(ms + NORM_EPS)
        return [x * inv * gain[:, c * LANE:(c + 1) * LANE] for c, x in enumerate(cols)]

    block((j == blk(COL_DQA)) | (j == blk(COL_DKVA)), latent_norm)

    def rope_key(cols, rows):
        x = cols[0]
        lane = lax.broadcasted_iota(jnp.int32, x.shape, 1)
        return [jnp.where(lane < MLA_ROPE, rope(x, TAB_ROPE64, 16, rows), 0.0)]

    block(j == blk(COL_DKR), rope_key, width=LANE)

    @pl.when(j == blk(COL_DKR))
    def _():
        up_ref[:, LANE:] = jnp.zeros((up_ref.shape[0], PROJ_TN - LANE), BF16)

    @pl.when(j >= N_UP_BLOCKS)
    def _():
        w = jnp.concatenate([wa_ref[GATE_ROW_OFF:, :], wb_ref[...]], axis=0).astype(BF16)

        def one(rows):
            g = _dot_nt(h_ref[rows, :], w)
            g_ref[rows, :] = (0.5 * jnp.tanh(0.5 * g) + 0.5).astype(BF16)

        for_row_pieces(one, GATE_ROW_SPLIT)


def _row_tile(n, choices):
    return next(t for t in choices if n % t == 0)


def _proj(h, w_in_t, tabs, aqk, mq, mkv, *, layer):
    n, d = h.shape
    tp = _row_tile(n, PROJ_ROW_TILES)
    nt = n // tp
    n_gate = w_in_t.shape[1] - QKV_W
    gate_blocks = n_gate // PROJ_TN
    first_gate_blk = (QKV_W - GATE_ROW_OFF) // PROJ_TN
    assert (QKV_W - GATE_ROW_OFF) % PROJ_TN == 0 and n_gate % PROJ_TN == 0 and first_gate_blk == N_UP_BLOCKS - 1

    def wa_idx(i, j):
        return (layer, jnp.where(j < N_UP_BLOCKS, j, j - 1), 0)

    def wb_idx(i, j):
        g = jnp.maximum(j - N_UP_BLOCKS, 0)
        return (layer, (QKV_W - GATE_ROW_OFF + (g + 1) * PROJ_TN) // GATE_ROW_OFF, 0)

    return pl.pallas_call(
        _proj_kernel,
        grid=(nt, N_UP_BLOCKS + gate_blocks),
        in_specs=[pl.BlockSpec((tp, d), lambda i, j: (i, 0)),
                  pl.BlockSpec((None, PROJ_TN, d), wa_idx),
                  pl.BlockSpec((None, GATE_ROW_OFF, d), wb_idx),
                  pl.BlockSpec((N_TAB_PLANES, tp, LANE), lambda i, j: (0, i, 0)),
                  pl.BlockSpec((2, LANE), lambda i, j: (0, 0)),
                  pl.BlockSpec((2, MLA_RANK), lambda i, j: (0, 0))],
        out_specs=[pl.BlockSpec((tp, PROJ_TN), lambda i, j: (i, jnp.minimum(j, N_UP_BLOCKS - 1))),
                   pl.BlockSpec((tp, PROJ_TN), lambda i, j: (i, jnp.maximum(j - N_UP_BLOCKS, 0)))],
        out_shape=[jax.ShapeDtypeStruct((n, UP_W), BF16),
                   jax.ShapeDtypeStruct((n, n_gate), BF16)],
        compiler_params=_cp(("arbitrary", "arbitrary"), 58),
        name="proj",
    )(h, w_in_t, w_in_t, tabs, aqk, jnp.stack([mq, mkv]))


def _mla_up_kernel(qa_ref, kva_ref, kr_ref, wq_ref, wkv_ref, tab_ref, dq_ref, dk_ref, dv_ref):
    cos, sin = tab_ref[2 * TAB_ROPE64], tab_ref[2 * TAB_ROPE64 + 1]
    q = _dot(qa_ref[...], wq_ref[...].astype(BF16))
    kv = _dot(kva_ref[...], wkv_ref[...].astype(BF16))
    scale = LOG2E * (MLA_NOPE + MLA_ROPE) ** -0.5
    kr = kr_ref[...]
    for hh in range(N_HEADS):
        base = hh * MLA_QK_PAD
        dq_ref[:, base:base + LANE] = (q[:, base:base + LANE] * scale).astype(BF16)
        dq_ref[:, base + LANE:base + 2 * LANE] = (
            _rope64(q[:, base + LANE:base + 2 * LANE], cos, sin) * scale).astype(BF16)
        dk_ref[:, base:base + LANE] = kv[:, hh * LANE:(hh + 1) * LANE].astype(BF16)
        dk_ref[:, base + LANE:base + 2 * LANE] = kr
    dv_ref[...] = kv[:, N_HEADS * LANE:].astype(BF16)


def _mla_up(up, wq_pad, wkv_perm, tabs, *, layer):
    n = up.shape[0]
    nt = n // TM
    wide = N_HEADS * MLA_QK_PAD
    return pl.pallas_call(
        _mla_up_kernel,
        grid=(nt,),
        in_specs=[pl.BlockSpec((TM, MLA_RANK), lambda i: (i, COL_DQA // MLA_RANK)),
                  pl.BlockSpec((TM, MLA_RANK), lambda i: (i, COL_DKVA // MLA_RANK)),
                  pl.BlockSpec((TM, LANE), lambda i: (i, COL_DKR // LANE)),
                  pl.BlockSpec((None, MLA_RANK, wide), lambda i: (layer, 0, 0)),
                  pl.BlockSpec((None, MLA_RANK, wide), lambda i: (layer, 0, 0)),
                  pl.BlockSpec((N_TAB_PLANES, TM, LANE), lambda i: (0, i, 0))],
        out_specs=[pl.BlockSpec((TM, wide), lambda i: (i, 0)),
                   pl.BlockSpec((TM, wide), lambda i: (i, 0)),
                   pl.BlockSpec((TM, N_HEADS * LANE), lambda i: (i, 0))],
        out_shape=[jax.ShapeDtypeStruct((n, wide), BF16),
                   jax.ShapeDtypeStruct((n, wide), BF16),
                   jax.ShapeDtypeStruct((n, N_HEADS * LANE), BF16)],
        compiler_params=_cp(("arbitrary",), 48),
        name="mla_up",
    )(up, up, up, wq_pad, wkv_perm, tabs)


def _attend(q, pieces, extra=None):
    scores = []
    for k, _, mask in pieces:
        s = _dot_nt(q, k)
        if mask is not None:
            s = jnp.where(mask, s, NEG_INF)
        scores.append(s)
    m = functools.reduce(jnp.maximum, [jnp.max(s, axis=-1, keepdims=True) for s in scores])
    if extra is not None:
        m = jnp.maximum(m, extra)
    den = None
    out = None
    for s, (_, v, _) in zip(scores, pieces):
        e = jnp.exp2(s - m)
        part = jnp.sum(e, axis=-1, keepdims=True)
        pv = _dot(e.astype(BF16), v)
        den = part if den is None else den + part
        out = pv if out is None else out + pv
    if extra is not None:
        den = den + jnp.exp2(extra - m)
    return out / den


def _heads(ref, k, width=LANE):
    return ref[:, k * width:(k + 1) * width]


def _query_steps(run, q_ref, o_ref, qc_ref, oc_ref, nq):
    t = pl.program_id(1)
    if qc_ref is None:
        run(q_ref, o_ref, True)
        return

    @pl.when(t < nq)
    def _():
        run(q_ref, o_ref, True)

    @pl.when(t >= nq)
    def _():
        run(qc_ref, oc_ref, False)


def _split_refs(refs, n_extra):
    with_ctx = len(refs) == 8 + n_extra
    q, rest = refs[0], refs[1:]
    qc = None
    if with_ctx:
        qc, rest = rest[0], rest[1:]
    kl, vl, kc, vc = rest[:4]
    extras = rest[4:4 + n_extra]
    outs = rest[4 + n_extra:]
    return q, qc, kl, vl, kc, vc, extras, outs[0], (outs[1] if with_ctx else None)


def _attn_a_kernel(*refs, nq):
    q_ref, qc_ref, kl_ref, vl_ref, kc_ref, vc_ref, _, o_ref, oc_ref = _split_refs(refs, 0)

    def run(q_ref, o_ref, with_latent):
        for hh in range(N_HEADS):
            kvh = hh // (N_HEADS // KV_HEADS)
            pieces = [(_heads(kc_ref, kvh), _heads(vc_ref, kvh), None)]
            if with_latent:
                pieces.append((_heads(kl_ref, kvh), _heads(vl_ref, kvh), None))
            o_ref[:, hh * LANE:(hh + 1) * LANE] = _attend(_heads(q_ref, hh), pieces).astype(BF16)

    _query_steps(run, q_ref, o_ref, qc_ref, oc_ref, nq)


def _attn_b_kernel(*refs, nq, seq, win_k):
    q_ref, qc_ref, kl_ref, vl_ref, kc_ref, vc_ref, (sink_ref,), o_ref, oc_ref = _split_refs(refs, 1)
    t = pl.program_id(1)

    def run(q_ref, o_ref, with_latent):
        rows = q_ref.shape[0]
        if with_latent:
            q0 = t * rows
            start = pl.multiple_of(jnp.clip(q0 - WINDOW, 0, seq - win_k), LANE)
            r = lax.broadcasted_iota(jnp.int32, (rows, win_k), 0)
            c = lax.broadcasted_iota(jnp.int32, (rows, win_k), 1)
            mask = jnp.abs(q0 + r - (start + c)) <= WINDOW
        for hh in range(N_HEADS):
            kvh = hh // (N_HEADS // KV_HEADS)
            sink = LOG2E * sink_ref[0:1, hh:hh + 1]
            pieces = [(_heads(kc_ref, kvh), _heads(vc_ref, kvh), None)]
            if with_latent:
                kw = kl_ref[pl.ds(start, win_k), kvh * LANE:(kvh + 1) * LANE]
                vw = vl_ref[pl.ds(start, win_k), kvh * LANE:(kvh + 1) * LANE]
                pieces.append((kw, vw, mask))
            o_ref[:, hh * LANE:(hh + 1) * LANE] = _attend(_heads(q_ref, hh), pieces, extra=sink).astype(BF16)

    _query_steps(run, q_ref, o_ref, qc_ref, oc_ref, nq)


def _attn_c_kernel(*refs, nq, lam_init):
    q_ref, qc_ref, kl_ref, vl_ref, kc_ref, vc_ref, (lam_ref, g_ref), o_ref, oc_ref = _split_refs(refs, 2)
    lam = (jnp.exp(jnp.sum(lam_ref[0:1, :] * lam_ref[1:2, :], axis=-1, keepdims=True))
           - jnp.exp(jnp.sum(lam_ref[2:3, :] * lam_ref[3:4, :], axis=-1, keepdims=True)) + lam_init)

    def run(q_ref, o_ref, with_latent):
        rows = q_ref.shape[0]
        lane = lax.broadcasted_iota(jnp.int32, (rows, LANE), 1)
        for hh in range(N_HEADS):
            q = _heads(q_ref, hh)
            zero = jnp.zeros_like(q)
            pieces = [(_heads(kc_ref, hh), _heads(vc_ref, hh), None)]
            if with_latent:
                pieces.append((_heads(kl_ref, hh), _heads(vl_ref, hh), None))
            o1 = _attend(jnp.where(lane < C_QK_DIM, q, zero), pieces)
            o2 = _attend(jnp.where(lane >= C_QK_DIM, q, zero), pieces)
            o = _rms(o1 - lam * o2, g_ref[...]) * (1.0 - lam_init)
            o_ref[:, hh * LANE:(hh + 1) * LANE] = o.astype(BF16)

    _query_steps(run, q_ref, o_ref, qc_ref, oc_ref, nq)


def _attn_d_kernel(*refs, nq):
    q_ref, qc_ref, kl_ref, vl_ref, kc_ref, vc_ref, _, o_ref, oc_ref = _split_refs(refs, 0)

    def run(q_ref, o_ref, with_latent):
        for hh in range(N_HEADS):
            pieces = [(_heads(kc_ref, hh, MLA_QK_PAD), _heads(vc_ref, hh), None)]
            if with_latent:
                pieces.append((_heads(kl_ref, hh, MLA_QK_PAD), _heads(vl_ref, hh), None))
            q = _heads(q_ref, hh, MLA_QK_PAD)
            o_ref[:, hh * LANE:(hh + 1) * LANE] = _attend(q, pieces).astype(BF16)

    _query_steps(run, q_ref, o_ref, qc_ref, oc_ref, nq)


def _attention(kern, name, q_src, k_src, v_src, extras, *, batch, seq, ctx, with_ctx_queries,
               q_w, q_blk, k_w, k_blk, v_w, v_blk, tq=TQ):
    tq = min(tq, seq)
    nq = seq // tq
    steps = nq + (1 if with_ctx_queries else 0)
    ctx_blk0 = batch * seq // ctx

    def qrow(b, t):
        return b * nq + jnp.minimum(t, nq - 1)

    in_specs = [pl.BlockSpec((tq, q_w), lambda b, t: (qrow(b, t), q_blk))]
    args = [q_src]
    if with_ctx_queries:
        in_specs.append(pl.BlockSpec((ctx, q_w), lambda b, t: (ctx_blk0 + b, q_blk)))
        args.append(q_src)
    in_specs += [pl.BlockSpec((seq, k_w), lambda b, t: (b, k_blk)),
                 pl.BlockSpec((seq, v_w), lambda b, t: (b, v_blk)),
                 pl.BlockSpec((ctx, k_w), lambda b, t: (ctx_blk0 + b, k_blk)),
                 pl.BlockSpec((ctx, v_w), lambda b, t: (ctx_blk0 + b, v_blk))]
    args += [k_src, v_src, k_src, v_src]
    for e in extras:
        in_specs.append(pl.BlockSpec(e.shape, lambda b, t: (0, 0)))
        args.append(e)
    out_specs = [pl.BlockSpec((tq, BRANCH_W), lambda b, t: (qrow(b, t), 0))]
    out_shape = [jax.ShapeDtypeStruct((batch * seq, BRANCH_W), BF16)]
    if with_ctx_queries:
        out_specs.append(pl.BlockSpec((ctx, BRANCH_W), lambda b, t: (b, 0)))
        out_shape.append(jax.ShapeDtypeStruct((batch * ctx, BRANCH_W), BF16))
    res = pl.pallas_call(
        functools.partial(kern, nq=nq),
        grid=(batch, steps),
        in_specs=in_specs, out_specs=out_specs, out_shape=out_shape,
        compiler_params=_cp(("arbitrary", "arbitrary"), 56),
        name=name,
    )(*args)
    return (res[0], res[1]) if with_ctx_queries else (res[0], None)


def _merge_kernel(*refs, n_lat_tiles, with_ctx):
    n_src = 2 * N_BRANCH if with_ctx else N_BRANCH
    lat_refs, ctx_refs = refs[:N_BRANCH], refs[N_BRANCH:n_src]
    g_ref, w_ref, y_ref = refs[n_src:]
    d = y_ref.shape[1]

    def body(o_refs):
        for c in range(d // MERGE_TN):
            cols = slice(c * MERGE_TN, (c + 1) * MERGE_TN)
            acc = None
            for m, o_ref in enumerate(o_refs):
                gate = g_ref[:, m * d + c * MERGE_TN:m * d + (c + 1) * MERGE_TN].astype(F32)
                p = gate * _dot(o_ref[...], w_ref[m, :, cols])
                acc = p if acc is None else acc + p
            y_ref[:, cols] = acc.astype(BF16)

    if not with_ctx:
        body(lat_refs)
        return
    i = pl.program_id(0)

    @pl.when(i < n_lat_tiles)
    def _():
        body(lat_refs)

    @pl.when(i >= n_lat_tiles)
    def _():
        body(ctx_refs)


def _merge(lat_outs, ctx_outs, gates, w_branch_bf16, *, layer, n_lat_rows, n_rows):
    d = w_branch_bf16.shape[3]
    n_lat_tiles = n_lat_rows // TO
    with_ctx = ctx_outs is not None
    srcs = list(lat_outs) + (list(ctx_outs) if with_ctx else [])
    specs = [pl.BlockSpec((TO, BRANCH_W), lambda i: (jnp.minimum(i, n_lat_tiles - 1), 0))] * N_BRANCH
    if with_ctx:
        specs += [pl.BlockSpec((TO, BRANCH_W), lambda i: (jnp.maximum(i - n_lat_tiles, 0), 0))] * N_BRANCH
    return pl.pallas_call(
        functools.partial(_merge_kernel, n_lat_tiles=n_lat_tiles, with_ctx=with_ctx),
        grid=(n_rows // TO,),
        in_specs=specs + [pl.BlockSpec((TO, N_BRANCH * d), lambda i: (i, 0)),
                          pl.BlockSpec((None, N_BRANCH, BRANCH_W, d), lambda i: (layer, 0, 0, 0))],
        out_specs=pl.BlockSpec((TO, d), lambda i: (i, 0)),
        out_shape=jax.ShapeDtypeStruct((n_rows, d), BF16),
        compiler_params=_cp(("arbitrary",), 56),
        name="merge",
    )(*srcs, gates, w_branch_bf16)


def _out_ln_kernel(y_ref, w_ref, x_ref, gate_ref, lng_ref, lnb_ref, sc_ref, sh_ref, xo_ref, ho_ref, *, alpha):
    piece = y_ref.shape[0] // 2
    for r in range(2):
        rows = slice(r * piece, (r + 1) * piece)
        mix = _dot(y_ref[rows, :], w_ref[...])
        z = alpha * x_ref[rows, :] + gate_ref[0] * mix
        xn = _standardise(z) * lng_ref[...] + lnb_ref[...]
        xo_ref[rows, :] = xn
        ho_ref[rows, :] = (xn * sc_ref[0] + sh_ref[0]).astype(BF16)


def _out_ln(y, w_out_bf16, x, gate_tbl, ln_g, ln_b, sc_tbl, sh_tbl, *, layer, n_rows, alpha):
    d = x.shape[1]
    tbl = pl.BlockSpec((1, 1, d), lambda i: ((i * TO) // TM, 0, 0))
    row = pl.BlockSpec((TO, d), lambda i: (i, 0))
    vec = pl.BlockSpec((1, d), lambda i: (0, 0))
    return pl.pallas_call(
        functools.partial(_out_ln_kernel, alpha=alpha),
        grid=(n_rows // TO,),
        in_specs=[row, pl.BlockSpec((None, d, d), lambda i: (layer, 0, 0)), row, tbl, vec, vec, tbl, tbl],
        out_specs=[row, row],
        out_shape=[jax.ShapeDtypeStruct((n_rows, d), F32), jax.ShapeDtypeStruct((n_rows, d), BF16)],
        compiler_params=_cp(("arbitrary",), 56),
        name="out_ln",
    )(y, w_out_bf16, x, gate_tbl, ln_g.reshape(1, d), ln_b.reshape(1, d), sc_tbl, sh_tbl)


def _ffn_kernel(h_ref, wg_ref, wu_ref, wo_ref, o_ref, acc_ref):
    f = pl.program_id(1)

    @pl.when(f == 0)
    def _():
        acc_ref[...] = jnp.zeros_like(acc_ref)

    h = h_ref[...]
    g = _dot(h, wg_ref[...].astype(BF16))
    u = _dot(h, wu_ref[...].astype(BF16))
    a = (g * (0.5 * jnp.tanh(0.5 * g) + 0.5) * u).astype(BF16)

    for n in range(acc_ref.shape[1] // FFN_ACC_TN):
        cols = slice(n * FFN_ACC_TN, (n + 1) * FFN_ACC_TN)
        acc_ref[:, cols] += _dot(a, wo_ref[:, cols].astype(BF16))

    @pl.when(f == pl.num_programs(1) - 1)
    def _():
        o_ref[...] = acc_ref[...].astype(BF16)


def _ffn(h, w_ffn_in, w_ffn_out, *, layer, n_tiles):
    d = h.shape[1]
    n = n_tiles * TM
    hidden = w_ffn_out.shape[1]
    tf = 512
    nf = hidden // tf
    return pl.pallas_call(
        _ffn_kernel,
        grid=(n_tiles, nf),
        in_specs=[pl.BlockSpec((TM, d), lambda i, f: (i, 0)),
                  pl.BlockSpec((None, d, tf), lambda i, f: (layer, 0, f)),
                  pl.BlockSpec((None, d, tf), lambda i, f: (layer, 0, nf + f)),
                  pl.BlockSpec((None, tf, d), lambda i, f: (layer, f, 0))],
        out_specs=pl.BlockSpec((TM, d), lambda i, f: (i, 0)),
        out_shape=jax.ShapeDtypeStruct((n, d), BF16),
        scratch_shapes=[pltpu.VMEM((TM, d), F32)],
        compiler_params=_cp(("arbitrary", "arbitrary"), 60),
        name="ffn",
    )(h, w_ffn_in, w_ffn_in, w_ffn_out)


def _rope_tables(seq, batch, ctx_rows):
    t = jnp.arange(seq, dtype=jnp.int32)
    pos_row = (t // GRID_W).astype(F32)[:, None]
    pos_col = (t % GRID_W).astype(F32)[:, None]

    def one(rot_dim, reps):
        axis_dim = rot_dim // 2
        freqs = ROPE_THETA ** (-jnp.arange(0, axis_dim, 2, dtype=F32) / axis_dim)
        ar, ac = pos_row * freqs[None, :], pos_col * freqs[None, :]
        cos = jnp.concatenate([jnp.cos(ar), jnp.cos(ar), jnp.cos(ac), jnp.cos(ac)], axis=-1)
        sin = jnp.concatenate([-jnp.sin(ar), jnp.sin(ar), -jnp.sin(ac), jnp.sin(ac)], axis=-1)
        return jnp.tile(cos, (1, reps)), jnp.tile(sin, (1, reps))

    c128, s128 = one(HEAD_DIM, 1)
    c64, s64 = one(C_QK_DIM, 2)
    ones = jnp.ones((ctx_rows, LANE), F32)
    zeros = jnp.zeros((ctx_rows, LANE), F32)

    def rows(tab, pad):
        return jnp.concatenate([jnp.tile(tab, (batch, 1)), pad])

    return jnp.stack([rows(c128, ones), rows(s128, zeros), rows(c64, ones), rows(s64, zeros)])


def kernel(x, c, ctx, c_ctx, w_ada, b_ada, w_in, a_qk_norm, b_sink, c_lambda, c_subln, mla_q_norm, mla_w_q_up, mla_kv_norm, mla_w_kv_up, w_branch, w_out, ln_mix_g, ln_mix_b, w_ffn_in, w_ffn_out, ln_ffn_g, ln_ffn_b):
    batch, seq, d = x.shape
    ctx_len = ctx.shape[1]
    depth = w_ada.shape[0]
    assert d == D_MODEL and seq % TM == 0 and (batch * ctx_len) % TM == 0 and batch + 1 <= 8
    n_lat_rows = batch * seq
    n_rows = n_lat_rows + batch * ctx_len
    tiles_per_seq = seq // TM
    n_lat_tiles = n_lat_rows // TM
    n_tiles = n_rows // TM
    alpha = (2 * depth) ** 0.25

    c_all = jnp.concatenate([c, c_ctx[None, :], jnp.zeros((8 - batch - 1, d), F32)], axis=0)
    ada = _ada_all(c_all, w_ada, b_ada)
    tile_row = np.array([b for b in range(batch) for _ in range(tiles_per_seq)]
                        + [batch] * (n_tiles - n_lat_tiles), np.int32)

    def table(l, chunk, plus_one=False):
        v = ada[l, :, chunk * d:(chunk + 1) * d][tile_row]
        return ((1.0 + v) if plus_one else v)[:, None, :]

    tabs = _rope_tables(seq, batch, batch * ctx_len)
    x_all, h = _init(x.reshape(n_lat_rows, d), ctx.reshape(batch * ctx_len, d),
                     table(0, 1, True), table(0, 0))

    win_k = min(TQ + 2 * WINDOW, seq)
    sink_pad = jnp.zeros((depth, 8, LANE), F32).at[:, 0, :N_HEADS].set(b_sink)
    wq_pad = jnp.pad(mla_w_q_up.reshape(depth, MLA_RANK, N_HEADS, MLA_NOPE + MLA_ROPE),
                     ((0, 0), (0, 0), (0, 0), (0, MLA_QK_PAD - MLA_NOPE - MLA_ROPE))
                     ).reshape(depth, MLA_RANK, N_HEADS * MLA_QK_PAD)
    wkv_pad = jnp.pad(mla_w_kv_up.reshape(depth, MLA_RANK, N_HEADS, 2, LANE).transpose(0, 1, 3, 2, 4)
                      .reshape(depth, MLA_RANK, 2 * N_HEADS * LANE),
                      ((0, 0), (0, 0), (0, N_HEADS * MLA_QK_PAD - 2 * N_HEADS * LANE)))

    w_in_t = jnp.swapaxes(w_in, 1, 2)
    w_out_bf16 = w_out.astype(BF16)
    w_branch_bf16 = w_branch.astype(BF16)

    for l in range(depth):
        last = l == depth - 1
        act_tiles = n_lat_tiles if last else n_tiles
        act_rows = act_tiles * TM
        lam_init = 0.8 - 0.6 * math.exp(-0.3 * l)

        up, gates = _proj(h, w_in_t, tabs, a_qk_norm[l], mla_q_norm[l], mla_kv_norm[l], layer=l)
        dq, dk, dv = _mla_up(up, wq_pad, wkv_pad, tabs, layer=l)

        common = dict(batch=batch, seq=seq, ctx=ctx_len, with_ctx_queries=not last)
        o_a = _attention(_attn_a_kernel, "attn_a", up, up, up, [], **common, tq=TQ_TALL,
                         q_w=512, q_blk=COL_AQ // 512, k_w=256, k_blk=COL_AK // 256, v_w=256, v_blk=COL_AV // 256)
        o_b = _attention(functools.partial(_attn_b_kernel, seq=seq, win_k=win_k), "attn_b", up, up, up,
                         [sink_pad[l]], **common,
                         q_w=512, q_blk=COL_BQ // 512, k_w=256, k_blk=COL_BK // 256, v_w=256, v_blk=COL_BV // 256)
        o_c = _attention(functools.partial(_attn_c_kernel, lam_init=lam_init), "attn_c", up, up, up,
                         [c_lambda[l], c_subln[l].reshape(1, LANE)], **common,
                         q_w=512, q_blk=COL_CQ // 512, k_w=512, k_blk=COL_CK // 512, v_w=512, v_blk=COL_CV // 512)
        o_d = _attention(_attn_d_kernel, "attn_d", dq, dk, dv, [], **common, tq=TQ_TALL,
                         q_w=N_HEADS * MLA_QK_PAD, q_blk=0, k_w=N_HEADS * MLA_QK_PAD, k_blk=0,
                         v_w=N_HEADS * LANE, v_blk=0)

        lat_outs, ctx_outs = zip(o_a, o_b, o_c, o_d)
        y = _merge(lat_outs, None if last else ctx_outs, gates, w_branch_bf16,
                   layer=l, n_lat_rows=n_lat_rows, n_rows=act_rows)
        x_all, h2 = _out_ln(y, w_out_bf16, x_all, table(l, 2), ln_mix_g[l], ln_mix_b[l],
                            table(l, 4, True), table(l, 3), layer=l, n_rows=act_rows, alpha=alpha)
        ffn = _ffn(h2, w_ffn_in, w_ffn_out, layer=l, n_tiles=act_tiles)
        if last:
            x_all, _ = _post_ln(x_all, ffn, table(l, 5), ln_ffn_g[l], ln_ffn_b[l], None, None,
                                n_rows=act_rows, alpha=alpha, out_rows=n_lat_rows)
        else:
            x_all, h = _post_ln(x_all, ffn, table(l, 5), ln_ffn_g[l], ln_ffn_b[l],
                                table(l + 1, 1, True), table(l + 1, 0),
                                n_rows=act_rows, alpha=alpha, out_rows=n_rows)
    return x_all.reshape(batch, seq, d)
```
